```python
import math
import jax, jax.numpy as jnp
from jax import lax
import numpy as np

D_MODEL = 2048
BATCH = 4
SEQ = 2048
DEPTH = 2

GRID_W = 64
CTX_LEN = 256
N_MOD = 6
SSM_WIDTH = D_MODEL // 2
SSM_GROUP = 16
SSM_GROUPS = SSM_WIDTH // SSM_GROUP
SSM_STATE = 64
DT_MIN = 0.001
DT_MAX = 0.1
NA_HEADS = 16
NA_HEAD_DIM = 64
NA_WIDTH = NA_HEADS * NA_HEAD_DIM
WIN_R = 8
WIN_C = 16
ROPE_BASE = 10000.0
D_FF = 4 * D_MODEL
NORM_EPS = 1e-6
PROJ_WIDTH = SSM_WIDTH + 3 * NA_WIDTH + 2 * D_MODEL

kernel_name = "hybrid_s5_natten_prefix_dit_block"


def rms_norm(x, g):
    xf = x.astype(jnp.float32)
    y = xf * lax.rsqrt(jnp.mean(xf * xf, axis=-1, keepdims=True) + NORM_EPS)
    return (y * g.astype(jnp.float32)).astype(x.dtype)


def modulate(x, shift, scale):
    return x * (1.0 + scale) + shift


def axial_rope(x, rows, cols):
    nf = NA_HEAD_DIM // 4
    half = NA_HEAD_DIM // 2
    inv = ROPE_BASE ** (-jnp.arange(nf, dtype=jnp.float32) / nf)

    def rot(xp, pos):
        ang = pos.astype(jnp.float32)[:, None] * inv
        cos = jnp.cos(ang)[None, :, None, :]
        sin = jnp.sin(ang)[None, :, None, :]
        x1, x2 = xp[..., :nf], xp[..., nf:]
        return jnp.concatenate([x1 * cos - x2 * sin, x1 * sin + x2 * cos], axis=-1)

    xf = x.astype(jnp.float32)
    out = jnp.concatenate([rot(xf[..., :half], rows), rot(xf[..., half:], cols)], axis=-1)
    return out.astype(x.dtype)


def s5_discretise(lam_re, lam_im, log_dt, b_re, b_im):
    lam = lax.complex(lam_re.astype(jnp.float32), lam_im.astype(jnp.float32))
    dt = jnp.exp(log_dt.astype(jnp.float32))[:, None]
    lam_bar = jnp.exp(lam * dt)
    b = lax.complex(b_re.astype(jnp.float32), b_im.astype(jnp.float32))
    b_bar = ((lam_bar - 1.0) / lam)[..., None] * b
    return lam_bar, b_bar


def _lin_comb(e_i, e_j):
    a_i, b_i = e_i
    a_j, b_j = e_j
    return a_j * a_i, a_j * b_i + b_j


def s5_scan(u, lam_bar, b_bar, s0, reverse):
    bu = jnp.einsum('gpc,btgc->btgp', b_bar, u.astype(jnp.complex64))
    if s0 is not None:
        first = -1 if reverse else 0
        bu = bu.at[:, first].add(lam_bar * s0)
    a = jnp.broadcast_to(lam_bar, bu.shape)
    _, states = lax.associative_scan(_lin_comb, (a, bu), axis=1, reverse=reverse)
    return states


def s5_readout(states, c_re, c_im):
    cmat = lax.complex(c_re.astype(jnp.float32), c_im.astype(jnp.float32))
    return jnp.real(jnp.einsum('gcp,btgp->btgc', cmat, states))


def s5_mixer(u_lat, u_ctx, lam_re, lam_im, log_dt, b_re, b_im, c_re, c_im, d_skip, with_ctx_out):
    bsz, t_len, _ = u_lat.shape
    c_len = u_ctx.shape[1]
    ul = u_lat.astype(jnp.float32).reshape(bsz, t_len, SSM_GROUPS, SSM_GROUP)
    uc = u_ctx.astype(jnp.float32).reshape(bsz, c_len, SSM_GROUPS, SSM_GROUP)
    dg = d_skip.astype(jnp.float32).reshape(SSM_GROUPS, SSM_GROUP)
    y_lat = ul * dg
    y_ctx = uc * dg
    for dirn, reverse in ((0, False), (1, True)):
        lam_bar, b_bar = s5_discretise(lam_re[dirn], lam_im[dirn], log_dt[dirn], b_re[dirn], b_im[dirn])
        st_ctx = s5_scan(uc, lam_bar, b_bar, None, reverse)
        s0 = st_ctx[:, 0] if reverse else st_ctx[:, -1]
        st_lat = s5_scan(ul, lam_bar, b_bar, s0, reverse)
        y_lat = y_lat + s5_readout(st_lat, c_re[dirn], c_im[dirn])
        if with_ctx_out:
            y_ctx = y_ctx + s5_readout(st_ctx, c_re[dirn], c_im[dirn])
    y_lat = y_lat.reshape(bsz, t_len, SSM_WIDTH).astype(u_lat.dtype)
    y_ctx = y_ctx.reshape(bsz, c_len, SSM_WIDTH).astype(u_ctx.dtype)
    return y_lat, y_ctx


def s5_glu(y, w_val, w_glu):
    a = jax.nn.gelu(y)
    return (a @ w_val) * jax.nn.sigmoid(a @ w_glu)


def neighbourhood_attention(q, k, v, k_ctx, v_ctx, rpb):
    bsz, t_len, n_h, dh = q.shape
    rows = t_len // GRID_W
    kr = min(WIN_R, rows)
    kc = min(WIN_C, GRID_W)
    nk = kr * kc
    qg = q.reshape(bsz, rows, GRID_W, n_h, dh)
    kg = k.reshape(bsz, rows, GRID_W, n_h, dh)
    vg = v.reshape(bsz, rows, GRID_W, n_h, dh)
    cols = np.arange(GRID_W)
    c0 = np.clip(cols - kc // 2, 0, GRID_W - kc)
    col_idx = c0[:, None] + np.arange(kc)[None, :]
    col_off = col_idx - cols[:, None] + (WIN_C - 1)
    bias_c = rpb.astype(jnp.float32)[:, :, col_off]
    scale = dh ** -0.5

    def one_row(r):
        r0 = jnp.clip(r - kr // 2, 0, rows - kr)
        q_r = lax.dynamic_index_in_dim(qg, r, axis=1, keepdims=False)
        k_rows = lax.dynamic_slice_in_dim(kg, r0, kr, axis=1)
        v_rows = lax.dynamic_slice_in_dim(vg, r0, kr, axis=1)
        k_win = k_rows[:, :, col_idx].transpose(0, 2, 1, 3, 4, 5).reshape(bsz, GRID_W, nk, n_h, dh)
        v_win = v_rows[:, :, col_idx].transpose(0, 2, 1, 3, 4, 5).reshape(bsz, GRID_W, nk, n_h, dh)
        row_off = r0 + jnp.arange(kr) - r + (WIN_R - 1)
        bias = jnp.take(bias_c, row_off, axis=1)
        bias = bias.transpose(0, 2, 1, 3).reshape(n_h, GRID_W, nk)
        s_loc = jnp.einsum('bwhd,bwkhd->bhwk', q_r, k_win).astype(jnp.float32) * scale + bias
        s_ctx = jnp.einsum('bwhd,blhd->bhwl', q_r, k_ctx).astype(jnp.float32) * scale
        p = jax.nn.softmax(jnp.concatenate([s_loc, s_ctx], axis=-1), axis=-1).astype(v.dtype)
        return (jnp.einsum('bhwk,bwkhd->bwhd', p[..., :nk], v_win)
                + jnp.einsum('bhwl,blhd->bwhd', p[..., nk:], v_ctx))

    out = lax.map(one_row, jnp.arange(rows))
    return out.transpose(1, 0, 2, 3, 4).reshape(bsz, t_len, n_h * dh)


def context_attention(q, k, v):
    s = jnp.einsum('blhd,bmhd->bhlm', q, k).astype(jnp.float32) * (NA_HEAD_DIM ** -0.5)
    p = jax.nn.softmax(s, axis=-1).astype(v.dtype)
    o = jnp.einsum('bhlm,bmhd->blhd', p, v)
    return o.reshape(q.shape[0], q.shape[1], NA_WIDTH)


def mixer_sublayer(h, hc, w_in, lam_re, lam_im, log_dt, b_re, b_im, c_re, c_im, d_skip,
                   w_ssm_val, w_ssm_glu, rpb, w_na_proj, w_out, with_ctx_out):
    bsz, t_len, _ = h.shape
    c_len = hc.shape[1]
    s1 = SSM_WIDTH
    s2 = s1 + NA_WIDTH
    s3 = s2 + NA_WIDTH
    s4 = s3 + NA_WIDTH
    s5 = s4 + D_MODEL
    proj = h @ w_in
    projc = hc @ w_in
    u, q, k, v, g_s, g_n = (proj[..., :s1], proj[..., s1:s2], proj[..., s2:s3],
                            proj[..., s3:s4], proj[..., s4:s5], proj[..., s5:])
    uc, qc, kc_, vc, g_sc, g_nc = (projc[..., :s1], projc[..., s1:s2], projc[..., s2:s3],
                                   projc[..., s3:s4], projc[..., s4:s5], projc[..., s5:])
    y_s, y_sc = s5_mixer(u, uc, lam_re, lam_im, log_dt, b_re, b_im, c_re, c_im, d_skip, with_ctx_out)
    br_s = s5_glu(y_s, w_ssm_val, w_ssm_glu)
    pos = jnp.arange(t_len)
    rows_pos, cols_pos = pos // GRID_W, pos % GRID_W
    qh = axial_rope(q.reshape(bsz, t_len, NA_HEADS, NA_HEAD_DIM), rows_pos, cols_pos)
    kh = axial_rope(k.reshape(bsz, t_len, NA_HEADS, NA_HEAD_DIM), rows_pos, cols_pos)
    vh = v.reshape(bsz, t_len, NA_HEADS, NA_HEAD_DIM)
    kch = kc_.reshape(bsz, c_len, NA_HEADS, NA_HEAD_DIM)
    vch = vc.reshape(bsz, c_len, NA_HEADS, NA_HEAD_DIM)
    br_n = neighbourhood_attention(qh, kh, vh, kch, vch, rpb) @ w_na_proj
    out = (jax.nn.sigmoid(g_s) * br_s + jax.nn.sigmoid(g_n) * br_n) @ w_out
    if not with_ctx_out:
        return out, None
    br_sc = s5_glu(y_sc, w_ssm_val, w_ssm_glu)
    qch = qc.reshape(bsz, c_len, NA_HEADS, NA_HEAD_DIM)
    br_nc = context_attention(qch, kch, vch) @ w_na_proj
    outc = (jax.nn.sigmoid(g_sc) * br_sc + jax.nn.sigmoid(g_nc) * br_nc) @ w_out
    return out, outc


def sq_relu_mlp(h, w_fc1, w_fc2):
    a = jax.nn.relu(h @ w_fc1)
    return (a * a) @ w_fc2


def setup_inputs(seed: int = 0) -> dict:
    key = jax.random.key(seed)
    ks = jax.random.split(key, 32)
    f32 = jnp.float32
    G, P, CG = SSM_GROUPS, SSM_STATE, SSM_GROUP

    def nrm(k, shape, std):
        return jax.random.normal(k, shape, f32) * std

    n_idx = jnp.arange(P, dtype=f32)
    return {
        "x": nrm(ks[0], (BATCH, SEQ, D_MODEL), 1.0),
        "c": nrm(ks[1], (BATCH, D_MODEL), 1.0),
        "ctx": nrm(ks[2], (BATCH, CTX_LEN, D_MODEL), 1.0),
        "c_ctx": nrm(ks[3], (D_MODEL,), 1.0),
        "w_mod": nrm(ks[4], (DEPTH, D_MODEL, N_MOD * D_MODEL), 0.5 * D_MODEL ** -0.5),
        "b_mod": nrm(ks[5], (DEPTH, N_MOD * D_MODEL), 0.02),
        "g_pre_mix": 1.0 + nrm(ks[6], (DEPTH, D_MODEL), 0.02),
        "g_post_mix": 1.0 + nrm(ks[7], (DEPTH, D_MODEL), 0.02),
        "g_pre_mlp": 1.0 + nrm(ks[8], (DEPTH, D_MODEL), 0.02),
        "g_post_mlp": 1.0 + nrm(ks[9], (DEPTH, D_MODEL), 0.02),
        "w_in": nrm(ks[10], (DEPTH, D_MODEL, PROJ_WIDTH), D_MODEL ** -0.5),
        "ssm_lam_re": -0.5 + nrm(ks[11], (DEPTH, 2, G, P), 0.01),
        "ssm_lam_im": math.pi * n_idx + nrm(ks[12], (DEPTH, 2, G, P), 0.01),
        "ssm_log_dt": jax.random.uniform(ks[13], (DEPTH, 2, G), f32, math.log(DT_MIN), math.log(DT_MAX)),
        "ssm_b_re": nrm(ks[14], (DEPTH, 2, G, P, CG), (2 * CG) ** -0.5),
        "ssm_b_im": nrm(ks[15], (DEPTH, 2, G, P, CG), (2 * CG) ** -0.5),
        "ssm_c_re": nrm(ks[16], (DEPTH, 2, G, CG, P), (2 * P) ** -0.5),
        "ssm_c_im": nrm(ks[17], (DEPTH, 2, G, CG, P), (2 * P) ** -0.5),
        "ssm_d": nrm(ks[18], (DEPTH, SSM_WIDTH), 0.5),
        "w_ssm_val": nrm(ks[19], (DEPTH, SSM_WIDTH, D_MODEL), SSM_WIDTH ** -0.5),
        "w_ssm_glu": nrm(ks[20], (DEPTH, SSM_WIDTH, D_MODEL), SSM_WIDTH ** -0.5),
        "na_rpb": nrm(ks[21], (DEPTH, NA_HEADS, 2 * WIN_R - 1, 2 * WIN_C - 1), 0.1),
        "w_na_proj": nrm(ks[22], (DEPTH, NA_WIDTH, D_MODEL), NA_WIDTH ** -0.5),
        "w_out": nrm(ks[23], (DEPTH, D_MODEL, D_MODEL), D_MODEL ** -0.5),
        "w_fc1": nrm(ks[24], (DEPTH, D_MODEL, D_FF), D_MODEL ** -0.5),
        "w_fc2": nrm(ks[25], (DEPTH, D_FF, D_MODEL), D_FF ** -0.5),
    }


def reference(x, c, ctx, c_ctx, w_mod, b_mod, g_pre_mix, g_post_mix, g_pre_mlp, g_post_mlp, w_in,
              ssm_lam_re, ssm_lam_im, ssm_log_dt, ssm_b_re, ssm_b_im, ssm_c_re, ssm_c_im, ssm_d,
              w_ssm_val, w_ssm_glu, na_rpb, w_na_proj, w_out, w_fc1, w_fc2):
    xc = ctx
    for l in range(DEPTH):
        with_ctx_out = l < DEPTH - 1
        mod = jax.nn.silu(c) @ w_mod[l] + b_mod[l]
        sh1, sc1, gt1, sh2, sc2, gt2 = [m[:, None, :] for m in jnp.split(mod, N_MOD, axis=-1)]
        modc = jax.nn.silu(c_ctx) @ w_mod[l] + b_mod[l]
        csh1, csc1, cgt1, csh2, csc2, cgt2 = jnp.split(modc, N_MOD, axis=-1)
        h = modulate(rms_norm(x, g_pre_mix[l]), sh1, sc1)
        hc = modulate(rms_norm(xc, g_pre_mix[l]), csh1, csc1)
        out, outc = mixer_sublayer(h, hc, w_in[l], ssm_lam_re[l], ssm_lam_im[l], ssm_log_dt[l],
                                   ssm_b_re[l], ssm_b_im[l], ssm_c_re[l], ssm_c_im[l], ssm_d[l],
                                   w_ssm_val[l], w_ssm_glu[l], na_rpb[l], w_na_proj[l], w_out[l],
                                   with_ctx_out)
        x = x + gt1 * rms_norm(out, g_post_mix[l])
        h2 = modulate(rms_norm(x, g_pre_mlp[l]), sh2, sc2)
        x = x + gt2 * rms_norm(sq_relu_mlp(h2, w_fc1[l], w_fc2[l]), g_post_mlp[l])
        if with_ctx_out:
            xc = xc + cgt1 * rms_norm(outc, g_post_mix[l])
            h2c = modulate(rms_norm(xc, g_pre_mlp[l]), csh2, csc2)
            xc = xc + cgt2 * rms_norm(sq_relu_mlp(h2c, w_fc1[l], w_fc2[l]), g_post_mlp[l])
    return x
```

```python
import functools
import math

import numpy as np
import jax
import jax.numpy as jnp
from jax import lax
from jax.experimental import pallas as pl
from jax.experimental.pallas import tpu as pltpu

D_MODEL = 2048
DEPTH = 2
GRID_W = 64
N_MOD = 6
SSM_WIDTH = D_MODEL // 2
SSM_GROUP = 16
SSM_GROUPS = SSM_WIDTH // SSM_GROUP
SSM_STATE = 64
NA_HEADS = 16
NA_HEAD_DIM = 64
NA_WIDTH = NA_HEADS * NA_HEAD_DIM
WIN_R = 8
WIN_C = 16
ROPE_BASE = 10000.0
D_FF = 4 * D_MODEL
NORM_EPS = 1e-6
PROJ_WIDTH = SSM_WIDTH + 3 * NA_WIDTH + 2 * D_MODEL

V7X_LANES = 128
V7X_SUBLANES = 8
V7X_VMEM_BYTES = 64 * 1024 * 1024
HEAD_PAIRS = NA_HEADS * NA_HEAD_DIM // V7X_LANES
S5_CHUNK = 16
S5_TILE = S5_CHUNK * SSM_GROUP
S5_GROUPS_PER_STEP = 4
MASK_VALUE = -1e30

F32 = jnp.float32
BF16 = jnp.bfloat16


def _cparams(n_axes, vmem_bytes):
    assert vmem_bytes <= V7X_VMEM_BYTES
    return pltpu.CompilerParams(dimension_semantics=("arbitrary",) * n_axes, vmem_limit_bytes=vmem_bytes)


def _sigmoid(x):
    return 1.0 / (1.0 + jnp.exp(-x))


def _gelu_tanh(x):
    return 0.5 * x * (1.0 + jnp.tanh(math.sqrt(2.0 / math.pi) * (x + 0.044715 * (x * x * x))))


def _rms(x, g):
    return x * lax.rsqrt(jnp.mean(x * x, axis=-1, keepdims=True) + NORM_EPS) * g


def _mod_kernel(c_ref, w_ref, b_ref, o_ref):
    c = c_ref[...]
    s = (c * _sigmoid(c)).astype(BF16)
    o_ref[...] = jnp.dot(s, w_ref[...].astype(BF16), preferred_element_type=F32) + b_ref[...]


def _modulation(c_rows, w_mod, b_mod, tn=1024):
    n = w_mod.shape[-1]
    return pl.pallas_call(
        _mod_kernel,
        grid=(DEPTH, n // tn),
        in_specs=[pl.BlockSpec((V7X_SUBLANES, D_MODEL), lambda l, j: (0, 0)),
                  pl.BlockSpec((None, D_MODEL, tn), lambda l, j: (l, 0, j)),
                  pl.BlockSpec((None, 1, tn), lambda l, j: (l, 0, j))],
        out_specs=pl.BlockSpec((None, V7X_SUBLANES, tn), lambda l, j: (l, 0, j)),
        out_shape=jax.ShapeDtypeStruct((DEPTH, V7X_SUBLANES, n), F32),
        compiler_params=_cparams(2, 40 * 1024 * 1024),
        name="modulation",
    )(c_rows, w_mod, b_mod.reshape(DEPTH, 1, n))


def _norm_mod_kernel(x_ref, g_ref, sh_ref, sc_ref, o_ref):
    y = _rms(x_ref[...], g_ref[...])
    o_ref[...] = (y * (1.0 + sc_ref[...]) + sh_ref[...]).astype(o_ref.dtype)


def _norm_mod(x, g, sh, sc, rows_per_batch, tm=512):
    m = x.shape[0]
    bidx = lambda i: (i * tm // rows_per_batch, 0, 0)
    return pl.pallas_call(
        _norm_mod_kernel,
        grid=(m // tm,),
        in_specs=[pl.BlockSpec((tm, D_MODEL), lambda i: (i, 0)),
                  pl.BlockSpec((1, D_MODEL), lambda i: (0, 0)),
                  pl.BlockSpec((None, 1, D_MODEL), bidx),
                  pl.BlockSpec((None, 1, D_MODEL), bidx)],
        out_specs=pl.BlockSpec((tm, D_MODEL), lambda i: (i, 0)),
        out_shape=jax.ShapeDtypeStruct((m, D_MODEL), BF16),
        compiler_params=_cparams(1, 32 * 1024 * 1024),
        name="norm_mod",
    )(x, g.reshape(1, D_MODEL), sh, sc)


def _matmul_kernel(a_ref, w_ref, o_ref):
    o_ref[...] = jnp.dot(a_ref[...], w_ref[...], preferred_element_type=F32).astype(o_ref.dtype)


def _matmul(a, w, tm, tn, out_dtype):
    m, k = a.shape
    n = w.shape[1]
    return pl.pallas_call(
        _matmul_kernel,
        grid=(n // tn, m // tm),
        in_specs=[pl.BlockSpec((tm, k), lambda j, i: (i, 0)),
                  pl.BlockSpec((k, tn), lambda j, i: (0, j))],
        out_specs=pl.BlockSpec((tm, tn), lambda j, i: (i, j)),
        out_shape=jax.ShapeDtypeStruct((m, n), out_dtype),
        compiler_params=_cparams(2, 40 * 1024 * 1024),
        name="in_proj",
    )(a, w)


def _s5_matrices(lam_re, lam_im, log_dt, b_re, b_im, c_re, c_im, d_skip):
    L = S5_CHUNK
    lam = lax.complex(lam_re.astype(F32), lam_im.astype(F32))
    dt = jnp.exp(log_dt.astype(F32))[..., None]
    lam_dt = lam * dt
    lam_bar = jnp.exp(lam_dt)
    b_bar = ((lam_bar - 1.0) / lam)[..., None] * lax.complex(b_re.astype(F32), b_im.astype(F32))
    cmat = lax.complex(c_re.astype(F32), c_im.astype(F32))
    tau = jnp.arange(L + 1, dtype=F32)
    pw = jnp.exp(lam_dt[:, :, None, :] * tau[None, None, :, None].astype(jnp.complex64))

    c_lag = cmat[:, :, None, :, :] * pw[:, :, :L, None, :]
    hi = lax.Precision.HIGHEST
    kern = (jnp.einsum('dglcp,dgpi->dglci', jnp.real(c_lag), jnp.real(b_bar), precision=hi)
            - jnp.einsum('dglcp,dgpi->dglci', jnp.imag(c_lag), jnp.imag(b_bar), precision=hi))
    ti = np.arange(L)[:, None]
    to = np.arange(L)[None, :]
    lag_f = np.clip(to - ti, 0, L - 1)
    lag_r = np.clip(ti - to, 0, L - 1)
    a_f = kern[0][:, lag_f] * jnp.asarray((to >= ti), F32)[None, :, :, None, None]
    a_r = kern[1][:, lag_r] * jnp.asarray((ti >= to), F32)[None, :, :, None, None]
    m_intra = (a_f + a_r).transpose(0, 1, 4, 2, 3).reshape(SSM_GROUPS, S5_TILE, S5_TILE)
    m_intra = m_intra + jax.vmap(jnp.diag)(jnp.tile(d_skip.astype(F32).reshape(SSM_GROUPS, 1, SSM_GROUP),
                                                    (1, L, 1)).reshape(SSM_GROUPS, S5_TILE))

    bf = pw[0][:, ::-1][:, 1:, None, :] * jnp.swapaxes(b_bar[0], -1, -2)[:, None, :, :]
    br = pw[1][:, :L, None, :] * jnp.swapaxes(b_bar[1], -1, -2)[:, None, :, :]
    b_in = jnp.concatenate([jnp.real(bf), jnp.real(br), jnp.imag(bf), jnp.imag(br)], axis=-1)
    b_in = b_in.reshape(SSM_GROUPS, S5_TILE, 4 * SSM_STATE)

    zf = cmat[0][:, None, :, :] * pw[0][:, 1:, None, :]
    zr = cmat[1][:, None, :, :] * pw[1][:, ::-1][:, :L, None, :]
    c_out = jnp.concatenate([jnp.real(zf), jnp.real(zr), -jnp.imag(zf), -jnp.imag(zr)], axis=-1)
    c_out = c_out.transpose(0, 3, 1, 2).reshape(SSM_GROUPS, 4 * SSM_STATE, S5_TILE)

    lam_l = pw[:, :, L, :]
    lam_chunk = jnp.stack([jnp.concatenate([jnp.real(lam_l[0]), jnp.real(lam_l[1])], axis=-1),
                           jnp.concatenate([jnp.imag(lam_l[0]), jnp.imag(lam_l[1])], axis=-1)], axis=1)
    return m_intra.astype(BF16), b_in.astype(BF16), c_out.astype(BF16), lam_chunk


def _s5_kernel(n_ctx_chunks, n_chunks, u_ref, m_ref, bin_ref, cout_ref, lam_ref, o_ref, s_scr, ea_scr, eb_scr):
    gb = u_ref.shape[0]
    rows = V7X_SUBLANES
    half = 2 * SSM_STATE
    for g in range(gb):
        s_scr[g] = jnp.dot(u_ref[g], bin_ref[g], preferred_element_type=F32)
    lane = lax.broadcasted_iota(jnp.int32, (rows, 2 * half), 1)
    is_fwd = (lane % half) < SSM_STATE
    a_re = [jnp.broadcast_to(lam_ref[g, 0:1, :], (rows, half)) for g in range(gb)]
    a_im = [jnp.broadcast_to(lam_ref[g, 1:2, :], (rows, half)) for g in range(gb)]

    def make_body(rev_base):
        def body(i, carry):
            row_f = pl.multiple_of(i * rows, rows)
            row_r = pl.multiple_of((rev_base - i) * rows, rows)
            out = []
            for g in range(gb):
                s_re, s_im = carry[2 * g], carry[2 * g + 1]
                enter = jnp.concatenate([s_re, s_im], axis=-1)
                ea_scr[g, pl.ds(row_f, rows), :] = enter
                eb_scr[g, pl.ds(row_r, rows), :] = enter
                contrib = jnp.where(is_fwd, s_scr[g, pl.ds(row_f, rows), :], s_scr[g, pl.ds(row_r, rows), :])
                c_re, c_im = contrib[:, :half], contrib[:, half:]
                out.append(a_re[g] * s_re - a_im[g] * s_im + c_re)
                out.append(a_re[g] * s_im + a_im[g] * s_re + c_im)
            return tuple(out)
        return body

    zero = jnp.zeros((rows, half), F32)
    carry = tuple(zero for _ in range(2 * gb))
    carry = lax.fori_loop(0, n_ctx_chunks, make_body(n_ctx_chunks - 1), carry)
    lax.fori_loop(n_ctx_chunks, n_chunks, make_body(n_chunks - 1 + n_ctx_chunks), carry)

    lane_all = lax.broadcasted_iota(jnp.int32, (u_ref.shape[1], 2 * half), 1)
    is_fwd_all = (lane_all % half) < SSM_STATE
    for g in range(gb):
        enter = jnp.where(is_fwd_all, ea_scr[g], eb_scr[g]).astype(BF16)
        y = (jnp.dot(u_ref[g], m_ref[g], preferred_element_type=F32)
             + jnp.dot(enter, cout_ref[g], preferred_element_type=F32))
        o_ref[g] = _gelu_tanh(y).astype(o_ref.dtype)


def _s5_branch(u_lat, u_ctx, mats):
    m_intra, b_in, c_out, lam_chunk = mats
    bsz, t_len, _ = u_lat.shape
    c_len = u_ctx.shape[1]
    L, G, C = S5_CHUNK, SSM_GROUPS, SSM_GROUP
    nc_ctx, nc_lat = c_len // L, t_len // L
    nc = nc_ctx + nc_lat
    rows = nc * V7X_SUBLANES
    gb = S5_GROUPS_PER_STEP

    def to_groups(u, n):
        return u.astype(BF16).reshape(bsz, n, L, G, C).transpose(3, 1, 0, 2, 4).reshape(G, n, bsz, L * C)

    ug = jnp.concatenate([to_groups(u_ctx, nc_ctx), to_groups(u_lat, nc_lat)], axis=1)
    ug = jnp.pad(ug, ((0, 0), (0, 0), (0, V7X_SUBLANES - bsz), (0, 0))).reshape(G, rows, S5_TILE)

    mat_spec = pl.BlockSpec((gb, S5_TILE, S5_TILE), lambda i: (i, 0, 0))
    act_spec = pl.BlockSpec((gb, rows, S5_TILE), lambda i: (i, 0, 0))
    a = pl.pallas_call(
        functools.partial(_s5_kernel, nc_ctx, nc),
        grid=(G // gb,),
        in_specs=[act_spec, mat_spec, mat_spec, mat_spec,
                  pl.BlockSpec((gb, 2, 2 * SSM_STATE), lambda i: (i, 0, 0))],
        out_specs=act_spec,
        out_shape=jax.ShapeDtypeStruct((G, rows, S5_TILE), BF16),
        scratch_shapes=[pltpu.VMEM((gb, rows, S5_TILE), F32)] * 3,
        compiler_params=_cparams(1, 48 * 1024 * 1024),
        name="s5",
    )(ug, m_intra, b_in, c_out, lam_chunk)

    a = a.reshape(G, nc, V7X_SUBLANES, L, C)[:, :, :bsz]

    def from_groups(x, n):
        return x.transpose(2, 1, 3, 0, 4).reshape(bsz, n * L, G * C)

    return from_groups(a[:, nc_ctx:], nc_lat), from_groups(a[:, :nc_ctx], nc_ctx)


def _rope_tables(t_len):
    nf = NA_HEAD_DIM // 4
    inv = ROPE_BASE ** (-jnp.arange(nf, dtype=F32) / nf)
    pos = jnp.arange(t_len, dtype=jnp.int32)
    ang_r = (pos // GRID_W).astype(F32)[:, None] * inv
    ang_c = (pos % GRID_W).astype(F32)[:, None] * inv
    cos = jnp.concatenate([jnp.cos(ang_r)] * 2 + [jnp.cos(ang_c)] * 2, axis=-1)
    sin = jnp.concatenate([-jnp.sin(ang_r), jnp.sin(ang_r), -jnp.sin(ang_c), jnp.sin(ang_c)], axis=-1)
    reps = V7X_LANES // NA_HEAD_DIM
    return jnp.tile(cos, (1, reps)), jnp.tile(sin, (1, reps))


def _bias_table(rpb, rows):
    kr, kc = min(WIN_R, rows), min(WIN_C, GRID_W)
    w = np.arange(GRID_W)
    c0 = np.clip(w - kc // 2, 0, GRID_W - kc)
    kcol = np.arange(GRID_W)
    valid = (kcol[None, :] >= c0[:, None]) & (kcol[None, :] < c0[:, None] + kc)
    col_off = np.clip(kcol[None, :] - w[:, None] + (WIN_C - 1), 0, 2 * WIN_C - 2)
    row_off = np.arange(kr)[None, :] - np.arange(kr)[:, None] + (WIN_R - 1)
    t = rpb.astype(F32)[:, row_off][:, :, :, col_off]
    t = jnp.where(jnp.asarray(valid)[None, None, None], t, MASK_VALUE)
    t = t.transpose(1, 0, 3, 2, 4).reshape(kr, HEAD_PAIRS, V7X_LANES // NA_HEAD_DIM, GRID_W, kr * GRID_W)
    return t


def _rope(x, cos, sin_signed):
    nf = NA_HEAD_DIM // 4
    lane = lax.broadcasted_iota(jnp.int32, x.shape, 1)
    first = (lane % (2 * nf)) < nf
    swapped = jnp.where(first, pltpu.roll(x, V7X_LANES - nf, 1), pltpu.roll(x, nf, 1))
    return x * cos + swapped * sin_signed


def _dot_nt(a, b):
    return lax.dot_general(a, b, (((1,), (1,)), ((), ())), preferred_element_type=F32)


def _na_kernel(rows, kr, q_ref, k_ref, v_ref, kc_ref, vc_ref, cos_ref, sin_ref, bias_ref, o_ref, krope_scr):
    r = pl.program_id(2)
    dh = NA_HEAD_DIM
    t_len = k_ref.shape[0]

    @pl.when(r == 0)
    def _():
        step = 512
        for s in range(t_len // step):
            sl = pl.ds(s * step, step)
            krope_scr[sl, :] = _rope(k_ref[sl, :], cos_ref[sl, :], sin_ref[sl, :]).astype(BF16)

    r0 = jnp.clip(r - kr // 2, 0, rows - kr)
    win = pl.ds(pl.multiple_of(r0 * GRID_W, GRID_W), kr * GRID_W)
    qrow = pl.ds(pl.multiple_of(r * GRID_W, GRID_W), GRID_W)
    q = (_rope(q_ref[...], cos_ref[qrow, :], sin_ref[qrow, :]) * (dh ** -0.5)).astype(BF16)
    kw = krope_scr[win, :]
    vw = v_ref[win, :].astype(BF16)
    kc = kc_ref[...].astype(BF16)
    vc = vc_ref[...].astype(BF16)
    for h in range(V7X_LANES // dh):
        hs = slice(h * dh, (h + 1) * dh)
        s_loc = _dot_nt(q[:, hs], kw[:, hs]) + bias_ref[h]
        s_ctx = _dot_nt(q[:, hs], kc[:, hs])
        m = jnp.maximum(jnp.max(s_loc, axis=-1, keepdims=True), jnp.max(s_ctx, axis=-1, keepdims=True))
        p_loc = jnp.exp(s_loc - m)
        p_ctx = jnp.exp(s_ctx - m)
        denom = jnp.sum(p_loc, axis=-1, keepdims=True) + jnp.sum(p_ctx, axis=-1, keepdims=True)
        o = (jnp.dot(p_loc.astype(BF16), vw[:, hs], preferred_element_type=F32)
             + jnp.dot(p_ctx.astype(BF16), vc[:, hs], preferred_element_type=F32))
        o_ref[:, hs] = (o / denom).astype(o_ref.dtype)


def _neighbourhood_attention(proj, projc, rope_tabs, bias_tab):
    bsz, t_len, _ = proj.shape
    c_len = projc.shape[1]
    rows = t_len // GRID_W
    kr = min(WIN_R, rows)
    qb, kb, vb = (SSM_WIDTH // V7X_LANES, (SSM_WIDTH + NA_WIDTH) // V7X_LANES,
                  (SSM_WIDTH + 2 * NA_WIDTH) // V7X_LANES)
    cos, sin = rope_tabs
    seq_spec = lambda off: pl.BlockSpec((None, t_len, V7X_LANES), lambda b, hp, r: (b, 0, off + hp))
    ctx_spec = lambda off: pl.BlockSpec((None, c_len, V7X_LANES), lambda b, hp, r: (b, 0, off + hp))
    tab_spec = pl.BlockSpec((t_len, V7X_LANES), lambda b, hp, r: (0, 0))
    bias_spec = pl.BlockSpec((None, None, V7X_LANES // NA_HEAD_DIM, GRID_W, kr * GRID_W),
                             lambda b, hp, r: (r - jnp.clip(r - kr // 2, 0, rows - kr), hp, 0, 0, 0))
    return pl.pallas_call(
        functools.partial(_na_kernel, rows, kr),
        grid=(bsz, HEAD_PAIRS, rows),
        in_specs=[pl.BlockSpec((None, GRID_W, V7X_LANES), lambda b, hp, r: (b, r, qb + hp)),
                  seq_spec(kb), seq_spec(vb), ctx_spec(kb), ctx_spec(vb), tab_spec, tab_spec, bias_spec],
        out_specs=pl.BlockSpec((None, GRID_W, V7X_LANES), lambda b, hp, r: (b, r, hp)),
        out_shape=jax.ShapeDtypeStruct((bsz, t_len, NA_WIDTH), BF16),
        scratch_shapes=[pltpu.VMEM((t_len, V7X_LANES), BF16)],
        compiler_params=_cparams(3, 32 * 1024 * 1024),
        name="neighbourhood_attention",
    )(proj, proj, proj, projc, projc, cos, sin, bias_tab)


def _ctx_attn_kernel(q_ref, k_ref, v_ref, o_ref):
    dh = NA_HEAD_DIM
    q = (q_ref[...] * (dh ** -0.5)).astype(BF16)
    k = k_ref[...].astype(BF16)
    v = v_ref[...].astype(BF16)
    for h in range(V7X_LANES // dh):
        hs = slice(h * dh, (h + 1) * dh)
        s = _dot_nt(q[:, hs], k[:, hs])
        p = jnp.exp(s - jnp.max(s, axis=-1, keepdims=True))
        o = jnp.dot(p.astype(BF16), v[:, hs], preferred_element_type=F32)
        o_ref[:, hs] = (o / jnp.sum(p, axis=-1, keepdims=True)).astype(o_ref.dtype)


def _context_attention(projc):
    bsz, c_len, _ = projc.shape
    qb, kb, vb = (SSM_WIDTH // V7X_LANES, (SSM_WIDTH + NA_WIDTH) // V7X_LANES,
                  (SSM_WIDTH + 2 * NA_WIDTH) // V7X_LANES)
    spec = lambda off: pl.BlockSpec((None, c_len, V7X_LANES), lambda b, hp: (b, 0, off + hp))
    return pl.pallas_call(
        _ctx_attn_kernel,
        grid=(bsz, HEAD_PAIRS),
        in_specs=[spec(qb), spec(kb), spec(vb)],
        out_specs=spec(0),
        out_shape=jax.ShapeDtypeStruct((bsz, c_len, NA_WIDTH), BF16),
        compiler_params=_cparams(2, 32 * 1024 * 1024),
        name="context_attention",
    )(projc, projc, projc)


def _merge_kernel(a_ref, att_ref, gs_ref, gn_ref, wv_ref, wg_ref, wn_ref, o_ref):
    a = a_ref[...]
    val = jnp.dot(a, wv_ref[...], preferred_element_type=F32)
    glu = jnp.dot(a, wg_ref[...], preferred_element_type=F32)
    br_n = jnp.dot(att_ref[...], wn_ref[...], preferred_element_type=F32)
    z = _sigmoid(gs_ref[...]) * (val * _sigmoid(glu)) + _sigmoid(gn_ref[...]) * br_n
    o_ref[...] = z.astype(o_ref.dtype)


def _merge(a, att, proj, w_val, w_glu, w_na, tm=512, tn=512):
    m = a.shape[0]
    gs_off = (SSM_WIDTH + 3 * NA_WIDTH) // tn
    gn_off = gs_off + D_MODEL // tn
    act = pl.BlockSpec((tm, SSM_WIDTH), lambda j, i: (i, 0))
    wsp = pl.BlockSpec((SSM_WIDTH, tn), lambda j, i: (0, j))
    return pl.pallas_call(
        _merge_kernel,
        grid=(D_MODEL // tn, m // tm),
        in_specs=[act, act,
                  pl.BlockSpec((tm, tn), lambda j, i: (i, gs_off + j)),
                  pl.BlockSpec((tm, tn), lambda j, i: (i, gn_off + j)),
                  wsp, wsp, wsp],
        out_specs=pl.BlockSpec((tm, tn), lambda j, i: (i, j)),
        out_shape=jax.ShapeDtypeStruct((m, D_MODEL), BF16),
        compiler_params=_cparams(2, 40 * 1024 * 1024),
        name="merge",
    )(a, att, proj, proj, w_val, w_glu, w_na)


def _outproj_kernel(z_ref, w_ref, x_ref, gt_ref, gpost_ref, gpre_ref, sh_ref, sc_ref, x_out, h_out):
    out = jnp.dot(z_ref[...], w_ref[...], preferred_element_type=F32)
    x1 = x_ref[...] + gt_ref[...] * _rms(out, gpost_ref[...])
    x_out[...] = x1
    h_out[...] = (_rms(x1, gpre_ref[...]) * (1.0 + sc_ref[...]) + sh_ref[...]).astype(h_out.dtype)


def _outproj(z, w_out, x, gt, g_post, g_pre, sh, sc, rows_per_batch, tm=256):
    m = z.shape[0]
    row = pl.BlockSpec((tm, D_MODEL), lambda i: (i, 0))
    vec = pl.BlockSpec((1, D_MODEL), lambda i: (0, 0))
    bvec = pl.BlockSpec((None, 1, D_MODEL), lambda i: (i * tm // rows_per_batch, 0, 0))
    return pl.pallas_call(
        _outproj_kernel,
        grid=(m // tm,),
        in_specs=[row, pl.BlockSpec((D_MODEL, D_MODEL), lambda i: (0, 0)), row, bvec, vec, vec, bvec, bvec],
        out_specs=[row, row],
        out_shape=[jax.ShapeDtypeStruct((m, D_MODEL), F32), jax.ShapeDtypeStruct((m, D_MODEL), BF16)],
        compiler_params=_cparams(1, 48 * 1024 * 1024),
        name="out_proj",
    )(z, w_out, x, gt, g_post.reshape(1, D_MODEL), g_pre.reshape(1, D_MODEL), sh, sc)


def _mlp_kernel(emit_next, h_ref, w1_ref, w2_ref, x_ref, gt_ref, gpost_ref, *rest):
    if emit_next:
        gpre_ref, sh_ref, sc_ref, x_out, h_out, acc = rest
    else:
        x_out, acc = rest
    f = pl.program_id(1)

    @pl.when(f == 0)
    def _():
        acc[...] = jnp.zeros_like(acc)

    a = jnp.maximum(jnp.dot(h_ref[...], w1_ref[...], preferred_element_type=F32), 0.0)
    acc[...] += jnp.dot((a * a).astype(BF16), w2_ref[...], preferred_element_type=F32)

    @pl.when(f == pl.num_programs(1) - 1)
    def _():
        x2 = x_ref[...] + gt_ref[...] * _rms(acc[...], gpost_ref[...])
        x_out[...] = x2
        if emit_next:
            h_out[...] = (_rms(x2, gpre_ref[...]) * (1.0 + sc_ref[...]) + sh_ref[...]).astype(h_out.dtype)


def _mlp(h2, w1, w2, x, gt, g_post, nxt, rows_per_batch, tm=512, tf=512):
    m = h2.shape[0]
    emit_next = nxt is not None
    row = pl.BlockSpec((tm, D_MODEL), lambda i, f: (i, 0))
    vec = pl.BlockSpec((1, D_MODEL), lambda i, f: (0, 0))
    bvec = pl.BlockSpec((None, 1, D_MODEL), lambda i, f: (i * tm // rows_per_batch, 0, 0))
    in_specs = [row, pl.BlockSpec((D_MODEL, tf), lambda i, f: (0, f)),
                pl.BlockSpec((tf, D_MODEL), lambda i, f: (f, 0)), row, bvec, vec]
    args = [h2, w1, w2, x, gt, g_post.reshape(1, D_MODEL)]
    out_specs = [row]
    out_shape = [jax.ShapeDtypeStruct((m, D_MODEL), F32)]
    if emit_next:
        g_pre, sh, sc = nxt
        in_specs += [vec, bvec, bvec]
        args += [g_pre.reshape(1, D_MODEL), sh, sc]
        out_specs.append(row)
        out_shape.append(jax.ShapeDtypeStruct((m, D_MODEL), BF16))
    res = pl.pallas_call(
        functools.partial(_mlp_kernel, emit_next),
        grid=(m // tm, D_FF // tf),
        in_specs=in_specs,
        out_specs=out_specs,
        out_shape=out_shape,
        scratch_shapes=[pltpu.VMEM((tm, D_MODEL), F32)],
        compiler_params=_cparams(2, 48 * 1024 * 1024),
        name="mlp",
    )(*args)
    return (res[0], res[1]) if emit_next else (res[0], None)


def kernel(x, c, ctx, c_ctx, w_mod, b_mod, g_pre_mix, g_post_mix, g_pre_mlp, g_post_mlp, w_in,
           ssm_lam_re, ssm_lam_im, ssm_log_dt, ssm_b_re, ssm_b_im, ssm_c_re, ssm_c_im, ssm_d,
           w_ssm_val, w_ssm_glu, na_rpb, w_na_proj, w_out, w_fc1, w_fc2):
    bsz, t_len, _ = x.shape
    c_len = ctx.shape[1]
    rows = t_len // GRID_W
    m_lat, m_ctx = bsz * t_len, bsz * c_len
    assert bsz < V7X_SUBLANES and t_len % (GRID_W * 8) == 0 and c_len % S5_CHUNK == 0

    c_rows = jnp.concatenate([c, c_ctx[None, :], jnp.zeros((V7X_SUBLANES - bsz - 1, D_MODEL), F32)], axis=0)
    mod = _modulation(c_rows, w_mod, b_mod)

    def mod_vectors(l):
        lat = [v[:, None, :] for v in jnp.split(mod[l, :bsz], N_MOD, axis=-1)]
        cvec = [jnp.broadcast_to(v[None, :, :], (bsz, 1, D_MODEL))
                for v in jnp.split(mod[l, bsz:bsz + 1], N_MOD, axis=-1)]
        return lat, cvec

    mods = [mod_vectors(l) for l in range(DEPTH)]
    rope_tabs = _rope_tables(t_len)

    xl = x.reshape(m_lat, D_MODEL)
    xc = ctx.reshape(m_ctx, D_MODEL)
    h = _norm_mod(xl, g_pre_mix[0], mods[0][0][0], mods[0][0][1], t_len)
    hc = _norm_mod(xc, g_pre_mix[0], mods[0][1][0], mods[0][1][1], c_len)

    for l in range(DEPTH):
        with_ctx_out = l < DEPTH - 1
        (sh1, sc1, gt1, sh2, sc2, gt2), (csh1, csc1, cgt1, csh2, csc2, cgt2) = mods[l]
        w_in_l = w_in[l].astype(BF16)
        w_val_l, w_glu_l = w_ssm_val[l].astype(BF16), w_ssm_glu[l].astype(BF16)
        w_na_l, w_out_l = w_na_proj[l].astype(BF16), w_out[l].astype(BF16)
        w1_l, w2_l = w_fc1[l].astype(BF16), w_fc2[l].astype(BF16)

        proj = _matmul(h, w_in_l, 1024, 1024, F32)
        projc = _matmul(hc, w_in_l, 1024, 1024, F32)
        proj3 = proj.reshape(bsz, t_len, PROJ_WIDTH)
        projc3 = projc.reshape(bsz, c_len, PROJ_WIDTH)

        mats = _s5_matrices(ssm_lam_re[l], ssm_lam_im[l], ssm_log_dt[l], ssm_b_re[l], ssm_b_im[l],
                            ssm_c_re[l], ssm_c_im[l], ssm_d[l])
        a_lat, a_ctx = _s5_branch(proj3[..., :SSM_WIDTH], projc3[..., :SSM_WIDTH], mats)

        att = _neighbourhood_attention(proj3, projc3, rope_tabs, _bias_table(na_rpb[l], rows))
        z = _merge(a_lat.reshape(m_lat, SSM_WIDTH), att.reshape(m_lat, NA_WIDTH), proj, w_val_l, w_glu_l, w_na_l)
        xl, h2 = _outproj(z, w_out_l, xl, gt1, g_post_mix[l], g_pre_mlp[l], sh2, sc2, t_len)
        nxt = (g_pre_mix[l + 1], mods[l + 1][0][0], mods[l + 1][0][1]) if with_ctx_out else None
        xl, h = _mlp(h2, w1_l, w2_l, xl, gt2, g_post_mlp[l], nxt, t_len)

        if with_ctx_out:
            attc = _context_attention(projc3)
            zc = _merge(a_ctx.reshape(m_ctx, SSM_WIDTH), attc.reshape(m_ctx, NA_WIDTH), projc,
                        w_val_l, w_glu_l, w_na_l)
            xc, h2c = _outproj(zc, w_out_l, xc, cgt1, g_post_mix[l], g_pre_mlp[l], csh2, csc2, c_len)
            nxtc = (g_pre_mix[l + 1], mods[l + 1][1][0], mods[l + 1][1][1])
            xc, hc = _mlp(h2c, w1_l, w2_l, xc, cgt2, g_post_mlp[l], nxtc, c_len)

    return xl.reshape(bsz, t_len, D_MODEL)
```

```python
import functools
import math

import numpy as np
import jax
import jax.numpy as jnp
from jax import lax
from jax.experimental import pallas as pl
from jax.experimental.pallas import tpu as pltpu

D_MODEL = 2048
DEPTH = 2
GRID_W = 64
N_MOD = 6
SSM_WIDTH = D_MODEL // 2
SSM_GROUP = 16
SSM_GROUPS = SSM_WIDTH // SSM_GROUP
SSM_STATE = 64
NA_HEADS = 16
NA_HEAD_DIM = 64
NA_WIDTH = NA_HEADS * NA_HEAD_DIM
WIN_R = 8
WIN_C = 16
ROPE_BASE = 10000.0
D_FF = 4 * D_MODEL
NORM_EPS = 1e-6
PROJ_WIDTH = SSM_WIDTH + 3 * NA_WIDTH + 2 * D_MODEL

V7X_LANES = 128
V7X_SUBLANES = 8
V7X_VMEM_BYTES = 64 * 1024 * 1024
HEAD_PAIRS = NA_HEADS * NA_HEAD_DIM // V7X_LANES
S5_CHUNK = 16
S5_TILE = S5_CHUNK * SSM_GROUP
S5_GROUPS_PER_STEP = 4
MASK_VALUE = -1e30

F32 = jnp.float32
BF16 = jnp.bfloat16


def _cparams(n_axes, vmem_bytes):
    assert vmem_bytes <= V7X_VMEM_BYTES
    return pltpu.CompilerParams(dimension_semantics=("arbitrary",) * n_axes, vmem_limit_bytes=vmem_bytes)


def _sigmoid(x):
    return 1.0 / (1.0 + jnp.exp(-x))


def _gelu_tanh(x):
    return 0.5 * x * (1.0 + jnp.tanh(math.sqrt(2.0 / math.pi) * (x + 0.044715 * (x * x * x))))


def _rms(x, g):
    return x * lax.rsqrt(jnp.mean(x * x, axis=-1, keepdims=True) + NORM_EPS) * g


def _mod_kernel(c_ref, w_ref, b_ref, o_ref):
    c = c_ref[...]
    s = (c * _sigmoid(c)).astype(BF16)
    o_ref[...] = jnp.dot(s, w_ref[...].astype(BF16), preferred_element_type=F32) + b_ref[...]


def _modulation(c_rows, w_mod, b_mod, tn=1024):
    n = w_mod.shape[-1]
    return pl.pallas_call(
        _mod_kernel,
        grid=(DEPTH, n // tn),
        in_specs=[pl.BlockSpec((V7X_SUBLANES, D_MODEL), lambda l, j: (0, 0)),
                  pl.BlockSpec((None, D_MODEL, tn), lambda l, j: (l, 0, j)),
                  pl.BlockSpec((None, 1, tn), lambda l, j: (l, 0, j))],
        out_specs=pl.BlockSpec((None, V7X_SUBLANES, tn), lambda l, j: (l, 0, j)),
        out_shape=jax.ShapeDtypeStruct((DEPTH, V7X_SUBLANES, n), F32),
        compiler_params=_cparams(2, 40 * 1024 * 1024),
        name="modulation",
    )(c_rows, w_mod, b_mod.reshape(DEPTH, 1, n))


def _norm_mod_kernel(x_ref, g_ref, sh_ref, sc_ref, o_ref):
    y = _rms(x_ref[...], g_ref[...])
    o_ref[...] = (y * (1.0 + sc_ref[...]) + sh_ref[...]).astype(o_ref.dtype)


def _norm_mod(x, g, sh, sc, rows_per_batch, tm=512):
    m = x.shape[0]
    bidx = lambda i: (i * tm // rows_per_batch, 0, 0)
    return pl.pallas_call(
        _norm_mod_kernel,
        grid=(m // tm,),
        in_specs=[pl.BlockSpec((tm, D_MODEL), lambda i: (i, 0)),
                  pl.BlockSpec((1, D_MODEL), lambda i: (0, 0)),
                  pl.BlockSpec((None, 1, D_MODEL), bidx),
                  pl.BlockSpec((None, 1, D_MODEL), bidx)],
        out_specs=pl.BlockSpec((tm, D_MODEL), lambda i: (i, 0)),
        out_shape=jax.ShapeDtypeStruct((m, D_MODEL), BF16),
        compiler_params=_cparams(1, 32 * 1024 * 1024),
        name="norm_mod",
    )(x, g.reshape(1, D_MODEL), sh, sc)


def _matmul_kernel(a_ref, w_ref, o_ref):
    o_ref[...] = jnp.dot(a_ref[...], w_ref[...], preferred_element_type=F32).astype(o_ref.dtype)


def _matmul(a, w, tm, tn, out_dtype):
    m, k = a.shape
    n = w.shape[1]
    return pl.pallas_call(
        _matmul_kernel,
        grid=(n // tn, m // tm),
        in_specs=[pl.BlockSpec((tm, k), lambda j, i: (i, 0)),
                  pl.BlockSpec((k, tn), lambda j, i: (0, j))],
        out_specs=pl.BlockSpec((tm, tn), lambda j, i: (i, j)),
        out_shape=jax.ShapeDtypeStruct((m, n), out_dtype),
        compiler_params=_cparams(2, 40 * 1024 * 1024),
        name="in_proj",
    )(a, w)


def _s5_matrices(lam_re, lam_im, log_dt, b_re, b_im, c_re, c_im, d_skip):
    L = S5_CHUNK
    def cmul(a, b):
        return a[0] * b[0] - a[1] * b[1], a[0] * b[1] + a[1] * b[0]

    lam = (lam_re.astype(F32), lam_im.astype(F32))
    dt = jnp.exp(log_dt.astype(F32))[..., None]
    tau = jnp.arange(L + 1, dtype=F32)[None, None, :, None]
    mag = jnp.exp((lam[0] * dt)[:, :, None, :] * tau)
    ang = (lam[1] * dt)[:, :, None, :] * tau
    pw = (mag * jnp.cos(ang), mag * jnp.sin(ang))
    num = (pw[0][:, :, 1] - 1.0, pw[1][:, :, 1])
    den = lam[0] * lam[0] + lam[1] * lam[1]
    coef = ((num[0] * lam[0] + num[1] * lam[1]) / den, (num[1] * lam[0] - num[0] * lam[1]) / den)
    b_bar = cmul((coef[0][..., None], coef[1][..., None]), (b_re.astype(F32), b_im.astype(F32)))
    b_bar_t = (jnp.swapaxes(b_bar[0], -1, -2), jnp.swapaxes(b_bar[1], -1, -2))
    cmat = (c_re.astype(F32), c_im.astype(F32))

    def pw_at(d, sl):
        return pw[0][d][:, sl, None, :], pw[1][d][:, sl, None, :]

    c_lag = cmul((cmat[0][:, :, None], cmat[1][:, :, None]),
                 (pw[0][:, :, :L, None, :], pw[1][:, :, :L, None, :]))
    hi = lax.Precision.HIGHEST
    kern = (jnp.einsum('dglcp,dgpi->dglci', c_lag[0], b_bar[0], precision=hi)
            - jnp.einsum('dglcp,dgpi->dglci', c_lag[1], b_bar[1], precision=hi))
    ti = np.arange(L)[:, None]
    to = np.arange(L)[None, :]
    lag_f = np.clip(to - ti, 0, L - 1)
    lag_r = np.clip(ti - to, 0, L - 1)
    a_f = kern[0][:, lag_f] * jnp.asarray((to >= ti), F32)[None, :, :, None, None]
    a_r = kern[1][:, lag_r] * jnp.asarray((ti >= to), F32)[None, :, :, None, None]
    m_intra = (a_f + a_r).transpose(0, 1, 4, 2, 3).reshape(SSM_GROUPS, S5_TILE, S5_TILE)
    m_intra = m_intra + jax.vmap(jnp.diag)(jnp.tile(d_skip.astype(F32).reshape(SSM_GROUPS, 1, SSM_GROUP),
                                                    (1, L, 1)).reshape(SSM_GROUPS, S5_TILE))

    bf = cmul(pw_at(0, slice(L - 1, None, -1)), (b_bar_t[0][0][:, None], b_bar_t[1][0][:, None]))
    br = cmul(pw_at(1, slice(0, L)), (b_bar_t[0][1][:, None], b_bar_t[1][1][:, None]))
    b_in = jnp.concatenate([bf[0], br[0], bf[1], br[1]], axis=-1)
    b_in = b_in.reshape(SSM_GROUPS, S5_TILE, 4 * SSM_STATE)

    zf = cmul((cmat[0][0][:, None], cmat[1][0][:, None]), pw_at(0, slice(1, L + 1)))
    zr = cmul((cmat[0][1][:, None], cmat[1][1][:, None]), pw_at(1, slice(L, 0, -1)))
    c_out = jnp.concatenate([zf[0], zr[0], -zf[1], -zr[1]], axis=-1)
    c_out = c_out.transpose(0, 3, 1, 2).reshape(SSM_GROUPS, 4 * SSM_STATE, S5_TILE)

    lam_chunk = jnp.stack([jnp.concatenate([pw[0][0][:, L], pw[0][1][:, L]], axis=-1),
                           jnp.concatenate([pw[1][0][:, L], pw[1][1][:, L]], axis=-1)], axis=1)
    return m_intra.astype(BF16), b_in.astype(BF16), c_out.astype(BF16), lam_chunk


def _s5_kernel(n_ctx_chunks, n_chunks, u_ref, m_ref, bin_ref, cout_ref, lam_ref, o_ref, s_scr, ea_scr, eb_scr):
    gb = u_ref.shape[0]
    rows = V7X_SUBLANES
    half = 2 * SSM_STATE
    for g in range(gb):
        s_scr[g] = jnp.dot(u_ref[g], bin_ref[g], preferred_element_type=F32)
    lane = lax.broadcasted_iota(jnp.int32, (rows, 2 * half), 1)
    is_fwd = (lane % half) < SSM_STATE
    a_re = [jnp.broadcast_to(lam_ref[g, 0:1, :], (rows, half)) for g in range(gb)]
    a_im = [jnp.broadcast_to(lam_ref[g, 1:2, :], (rows, half)) for g in range(gb)]

    def make_body(rev_base):
        def body(i, carry):
            row_f = pl.multiple_of(i * rows, rows)
            row_r = pl.multiple_of((rev_base - i) * rows, rows)
            out = []
            for g in range(gb):
                s_re, s_im = carry[2 * g], carry[2 * g + 1]
                enter = jnp.concatenate([s_re, s_im], axis=-1)
                ea_scr[g, pl.ds(row_f, rows), :] = enter
                eb_scr[g, pl.ds(row_r, rows), :] = enter
                contrib = jnp.where(is_fwd, s_scr[g, pl.ds(row_f, rows), :], s_scr[g, pl.ds(row_r, rows), :])
                c_re, c_im = contrib[:, :half], contrib[:, half:]
                out.append(a_re[g] * s_re - a_im[g] * s_im + c_re)
                out.append(a_re[g] * s_im + a_im[g] * s_re + c_im)
            return tuple(out)
        return body

    zero = jnp.zeros((rows, half), F32)
    carry = tuple(zero for _ in range(2 * gb))
    carry = lax.fori_loop(0, n_ctx_chunks, make_body(n_ctx_chunks - 1), carry)
    lax.fori_loop(n_ctx_chunks, n_chunks, make_body(n_chunks - 1 + n_ctx_chunks), carry)

    lane_all = lax.broadcasted_iota(jnp.int32, (u_ref.shape[1], 2 * half), 1)
    is_fwd_all = (lane_all % half) < SSM_STATE
    for g in range(gb):
        enter = jnp.where(is_fwd_all, ea_scr[g], eb_scr[g]).astype(BF16)
        y = (jnp.dot(u_ref[g], m_ref[g], preferred_element_type=F32)
             + jnp.dot(enter, cout_ref[g], preferred_element_type=F32))
        o_ref[g] = _gelu_tanh(y).astype(o_ref.dtype)


def _s5_branch(u_lat, u_ctx, mats):
    m_intra, b_in, c_out, lam_chunk = mats
    bsz, t_len, _ = u_lat.shape
    c_len = u_ctx.shape[1]
    L, G, C = S5_CHUNK, SSM_GROUPS, SSM_GROUP
    nc_ctx, nc_lat = c_len // L, t_len // L
    nc = nc_ctx + nc_lat
    rows = nc * V7X_SUBLANES
    gb = S5_GROUPS_PER_STEP

    def to_groups(u, n):
        return u.astype(BF16).reshape(bsz, n, L, G, C).transpose(3, 1, 0, 2, 4).reshape(G, n, bsz, L * C)

    ug = jnp.concatenate([to_groups(u_ctx, nc_ctx), to_groups(u_lat, nc_lat)], axis=1)
    ug = jnp.pad(ug, ((0, 0), (0, 0), (0, V7X_SUBLANES - bsz), (0, 0))).reshape(G, rows, S5_TILE)

    mat_spec = pl.BlockSpec((gb, S5_TILE, S5_TILE), lambda i: (i, 0, 0))
    act_spec = pl.BlockSpec((gb, rows, S5_TILE), lambda i: (i, 0, 0))
    a = pl.pallas_call(
        functools.partial(_s5_kernel, nc_ctx, nc),
        grid=(G // gb,),
        in_specs=[act_spec, mat_spec, mat_spec, mat_spec,
                  pl.BlockSpec((gb, 2, 2 * SSM_STATE), lambda i: (i, 0, 0))],
        out_specs=act_spec,
        out_shape=jax.ShapeDtypeStruct((G, rows, S5_TILE), BF16),
        scratch_shapes=[pltpu.VMEM((gb, rows, S5_TILE), F32)] * 3,
        compiler_params=_cparams(1, 48 * 1024 * 1024),
        name="s5",
    )(ug, m_intra, b_in, c_out, lam_chunk)

    a = a.reshape(G, nc, V7X_SUBLANES, L, C)[:, :, :bsz]

    def from_groups(x, n):
        return x.transpose(2, 1, 3, 0, 4).reshape(bsz, n * L, G * C)

    return from_groups(a[:, nc_ctx:], nc_lat), from_groups(a[:, :nc_ctx], nc_ctx)


def _rope_tables(t_len):
    nf = NA_HEAD_DIM // 4
    inv = ROPE_BASE ** (-jnp.arange(nf, dtype=F32) / nf)
    pos = jnp.arange(t_len, dtype=jnp.int32)
    ang_r = (pos // GRID_W).astype(F32)[:, None] * inv
    ang_c = (pos % GRID_W).astype(F32)[:, None] * inv
    cos = jnp.concatenate([jnp.cos(ang_r)] * 2 + [jnp.cos(ang_c)] * 2, axis=-1)
    sin = jnp.concatenate([-jnp.sin(ang_r), jnp.sin(ang_r), -jnp.sin(ang_c), jnp.sin(ang_c)], axis=-1)
    reps = V7X_LANES // NA_HEAD_DIM
    return jnp.tile(cos, (1, reps)), jnp.tile(sin, (1, reps))


def _bias_table(rpb):
    kc = min(WIN_C, GRID_W)
    w = np.arange(GRID_W)
    c0 = np.clip(w - kc // 2, 0, GRID_W - kc)
    kcol = np.arange(GRID_W)
    valid = (kcol[None, :] >= c0[:, None]) & (kcol[None, :] < c0[:, None] + kc)
    col_off = np.clip(kcol[None, :] - w[:, None] + (WIN_C - 1), 0, 2 * WIN_C - 2)
    t = jnp.where(jnp.asarray(valid)[None, None], rpb.astype(F32)[:, :, col_off], MASK_VALUE)
    t = jnp.concatenate([t[:, :-1], t[:, 1:]], axis=-1)
    n_heads = V7X_LANES // NA_HEAD_DIM
    t = t.reshape(HEAD_PAIRS, n_heads, 2 * WIN_R - 2, GRID_W, 2 * GRID_W).transpose(0, 2, 1, 3, 4)
    return t.reshape(HEAD_PAIRS, 2 * WIN_R - 2, n_heads * GRID_W, 2 * GRID_W)


def _rope(x, cos, sin_signed):
    nf = NA_HEAD_DIM // 4
    lane = lax.broadcasted_iota(jnp.int32, x.shape, 1)
    first = (lane % (2 * nf)) < nf
    swapped = jnp.where(first, pltpu.roll(x, V7X_LANES - nf, 1), pltpu.roll(x, nf, 1))
    return x * cos + swapped * sin_signed


def _dot_nt(a, b):
    return lax.dot_general(a, b, (((1,), (1,)), ((), ())), preferred_element_type=F32)


def _na_kernel(rows, kr, q_ref, k_ref, v_ref, kc_ref, vc_ref, cos_ref, sin_ref, bias_ref, o_ref,
               krope_scr, vb_scr, kcb_scr, vcb_scr, s_scr, p_scr):
    step = pl.program_id(2)
    dh = NA_HEAD_DIM
    t_len = k_ref.shape[0]
    rows_per_step = q_ref.shape[0] // GRID_W
    n_heads = V7X_LANES // dh

    @pl.when(step == 0)
    def _():
        chunk = 512
        for s in range(t_len // chunk):
            sl = pl.ds(s * chunk, chunk)
            krope_scr[sl, :] = _rope(k_ref[sl, :], cos_ref[sl, :], sin_ref[sl, :]).astype(BF16)
            vb_scr[sl, :] = v_ref[sl, :].astype(BF16)
        kcb_scr[...] = kc_ref[...].astype(BF16)
        vcb_scr[...] = vc_ref[...].astype(BF16)

    lane = lax.broadcasted_iota(jnp.int32, (GRID_W, V7X_LANES), 1)
    n_loc = kr * GRID_W
    wins = []
    for i in range(rows_per_step):
        r = step * rows_per_step + i
        r0 = jnp.clip(r - kr // 2, 0, rows - kr)
        ro0 = r0 - r + (WIN_R - 1)
        win = pl.ds(pl.multiple_of(r0 * GRID_W, GRID_W), n_loc)
        wins.append(win)
        qrow = pl.ds(pl.multiple_of(r * GRID_W, GRID_W), GRID_W)
        q = _rope(q_ref[pl.ds(i * GRID_W, GRID_W), :], cos_ref[qrow, :], sin_ref[qrow, :]) * (dh ** -0.5)
        q2 = jnp.concatenate([jnp.where((lane // dh) == h, q, 0.0) for h in range(n_heads)], axis=0).astype(BF16)
        s_loc = _dot_nt(q2, krope_scr[win, :])
        for j in range(kr // 2):
            cols = slice(j * V7X_LANES, (j + 1) * V7X_LANES)
            s_scr[i, :, cols] = s_loc[:, cols] + bias_ref[ro0 + 2 * j]
        s_scr[i, :, n_loc:] = _dot_nt(q2, kcb_scr[...])
    for i in range(rows_per_step):
        s = s_scr[i]
        m = jnp.max(s, axis=-1, keepdims=True)
        p = jnp.exp(s - m)
        denom = jnp.sum(p, axis=-1, keepdims=True)
        p_scr[i] = p.astype(BF16)
        o = (jnp.dot(p_scr[i, :, :n_loc], vb_scr[wins[i], :], preferred_element_type=F32)
             + jnp.dot(p_scr[i, :, n_loc:], vcb_scr[...], preferred_element_type=F32)) / denom
        out = o[:GRID_W]
        for h in range(1, n_heads):
            out = jnp.where((lane // dh) == h, o[h * GRID_W:(h + 1) * GRID_W], out)
        o_ref[pl.ds(i * GRID_W, GRID_W), :] = out.astype(o_ref.dtype)


def _neighbourhood_attention(proj, projc, rope_tabs, bias_tab, rows_per_step=4):
    bsz, t_len, _ = proj.shape
    c_len = projc.shape[1]
    rows = t_len // GRID_W
    kr = min(WIN_R, rows)
    assert kr % 2 == 0 and rows % rows_per_step == 0
    qb, kb, vb = (SSM_WIDTH // V7X_LANES, (SSM_WIDTH + NA_WIDTH) // V7X_LANES,
                  (SSM_WIDTH + 2 * NA_WIDTH) // V7X_LANES)
    cos, sin = rope_tabs
    tq = rows_per_step * GRID_W
    seq_spec = lambda off: pl.BlockSpec((None, t_len, V7X_LANES), lambda b, hp, r: (b, 0, off + hp))
    ctx_spec = lambda off: pl.BlockSpec((None, c_len, V7X_LANES), lambda b, hp, r: (b, 0, off + hp))
    tab_spec = pl.BlockSpec((t_len, V7X_LANES), lambda b, hp, r: (0, 0))
    bias_spec = pl.BlockSpec((None,) + bias_tab.shape[1:], lambda b, hp, r: (hp, 0, 0, 0))
    return pl.pallas_call(
        functools.partial(_na_kernel, rows, kr),
        grid=(bsz, HEAD_PAIRS, rows // rows_per_step),
        in_specs=[pl.BlockSpec((None, tq, V7X_LANES), lambda b, hp, r: (b, r, qb + hp)),
                  seq_spec(kb), seq_spec(vb), ctx_spec(kb), ctx_spec(vb), tab_spec, tab_spec, bias_spec],
        out_specs=pl.BlockSpec((None, tq, V7X_LANES), lambda b, hp, r: (b, r, hp)),
        out_shape=jax.ShapeDtypeStruct((bsz, t_len, NA_WIDTH), BF16),
        scratch_shapes=[pltpu.VMEM((t_len, V7X_LANES), BF16), pltpu.VMEM((t_len, V7X_LANES), BF16),
                        pltpu.VMEM((c_len, V7X_LANES), BF16), pltpu.VMEM((c_len, V7X_LANES), BF16),
                        pltpu.VMEM((rows_per_step, 2 * GRID_W, kr * GRID_W + c_len), F32),
                        pltpu.VMEM((rows_per_step, 2 * GRID_W, kr * GRID_W + c_len), BF16)],
        compiler_params=_cparams(3, 32 * 1024 * 1024),
        name="neighbourhood_attention",
    )(proj, proj, proj, projc, projc, cos, sin, bias_tab)


def _ctx_attn_kernel(q_ref, k_ref, v_ref, o_ref):
    dh = NA_HEAD_DIM
    q = (q_ref[...] * (dh ** -0.5)).astype(BF16)
    k = k_ref[...].astype(BF16)
    v = v_ref[...].astype(BF16)
    for h in range(V7X_LANES // dh):
        hs = slice(h * dh, (h + 1) * dh)
        s = _dot_nt(q[:, hs], k[:, hs])
        p = jnp.exp(s - jnp.max(s, axis=-1, keepdims=True))
        o = jnp.dot(p.astype(BF16), v[:, hs], preferred_element_type=F32)
        o_ref[:, hs] = (o / jnp.sum(p, axis=-1, keepdims=True)).astype(o_ref.dtype)


def _context_attention(projc):
    bsz, c_len, _ = projc.shape
    qb, kb, vb = (SSM_WIDTH // V7X_LANES, (SSM_WIDTH + NA_WIDTH) // V7X_LANES,
                  (SSM_WIDTH + 2 * NA_WIDTH) // V7X_LANES)
    spec = lambda off: pl.BlockSpec((None, c_len, V7X_LANES), lambda b, hp: (b, 0, off + hp))
    return pl.pallas_call(
        _ctx_attn_kernel,
        grid=(bsz, HEAD_PAIRS),
        in_specs=[spec(qb), spec(kb), spec(vb)],
        out_specs=spec(0),
        out_shape=jax.ShapeDtypeStruct((bsz, c_len, NA_WIDTH), BF16),
        compiler_params=_cparams(2, 32 * 1024 * 1024),
        name="context_attention",
    )(projc, projc, projc)


def _merge_kernel(a_ref, att_ref, gs_ref, gn_ref, wv_ref, wg_ref, wn_ref, o_ref):
    a = a_ref[...]
    val = jnp.dot(a, wv_ref[...], preferred_element_type=F32)
    glu = jnp.dot(a, wg_ref[...], preferred_element_type=F32)
    br_n = jnp.dot(att_ref[...], wn_ref[...], preferred_element_type=F32)
    z = _sigmoid(gs_ref[...]) * (val * _sigmoid(glu)) + _sigmoid(gn_ref[...]) * br_n
    o_ref[...] = z.astype(o_ref.dtype)


def _merge(a, att, proj, w_val, w_glu, w_na, tm=512, tn=512):
    m = a.shape[0]
    gs_off = (SSM_WIDTH + 3 * NA_WIDTH) // tn
    gn_off = gs_off + D_MODEL // tn
    act = pl.BlockSpec((tm, SSM_WIDTH), lambda j, i: (i, 0))
    wsp = pl.BlockSpec((SSM_WIDTH, tn), lambda j, i: (0, j))
    return pl.pallas_call(
        _merge_kernel,
        grid=(D_MODEL // tn, m // tm),
        in_specs=[act, act,
                  pl.BlockSpec((tm, tn), lambda j, i: (i, gs_off + j)),
                  pl.BlockSpec((tm, tn), lambda j, i: (i, gn_off + j)),
                  wsp, wsp, wsp],
        out_specs=pl.BlockSpec((tm, tn), lambda j, i: (i, j)),
        out_shape=jax.ShapeDtypeStruct((m, D_MODEL), BF16),
        compiler_params=_cparams(2, 40 * 1024 * 1024),
        name="merge",
    )(a, att, proj, proj, w_val, w_glu, w_na)


def _outproj_kernel(z_ref, w_ref, x_ref, gt_ref, gpost_ref, gpre_ref, sh_ref, sc_ref, x_out, h_out):
    out = jnp.dot(z_ref[...], w_ref[...], preferred_element_type=F32)
    x1 = x_ref[...] + gt_ref[...] * _rms(out, gpost_ref[...])
    x_out[...] = x1
    h_out[...] = (_rms(x1, gpre_ref[...]) * (1.0 + sc_ref[...]) + sh_ref[...]).astype(h_out.dtype)


def _outproj(z, w_out, x, gt, g_post, g_pre, sh, sc, rows_per_batch, tm=256):
    m = z.shape[0]
    row = pl.BlockSpec((tm, D_MODEL), lambda i: (i, 0))
    vec = pl.BlockSpec((1, D_MODEL), lambda i: (0, 0))
    bvec = pl.BlockSpec((None, 1, D_MODEL), lambda i: (i * tm // rows_per_batch, 0, 0))
    return pl.pallas_call(
        _outproj_kernel,
        grid=(m // tm,),
        in_specs=[row, pl.BlockSpec((D_MODEL, D_MODEL), lambda i: (0, 0)), row, bvec, vec, vec, bvec, bvec],
        out_specs=[row, row],
        out_shape=[jax.ShapeDtypeStruct((m, D_MODEL), F32), jax.ShapeDtypeStruct((m, D_MODEL), BF16)],
        compiler_params=_cparams(1, 48 * 1024 * 1024),
        name="out_proj",
    )(z, w_out, x, gt, g_post.reshape(1, D_MODEL), g_pre.reshape(1, D_MODEL), sh, sc)


def _mlp_kernel(emit_next, h_ref, w1_ref, w2_ref, x_ref, gt_ref, gpost_ref, *rest):
    if emit_next:
        gpre_ref, sh_ref, sc_ref, x_out, h_out, acc = rest
    else:
        x_out, acc = rest
    f = pl.program_id(1)

    @pl.when(f == 0)
    def _():
        acc[...] = jnp.zeros_like(acc)

    a = jnp.maximum(jnp.dot(h_ref[...], w1_ref[...], preferred_element_type=F32), 0.0)
    acc[...] += jnp.dot((a * a).astype(BF16), w2_ref[...], preferred_element_type=F32)

    @pl.when(f == pl.num_programs(1) - 1)
    def _():
        x2 = x_ref[...] + gt_ref[...] * _rms(acc[...], gpost_ref[...])
        x_out[...] = x2
        if emit_next:
            h_out[...] = (_rms(x2, gpre_ref[...]) * (1.0 + sc_ref[...]) + sh_ref[...]).astype(h_out.dtype)


def _mlp(h2, w1, w2, x, gt, g_post, nxt, rows_per_batch, tm=512, tf=512):
    m = h2.shape[0]
    emit_next = nxt is not None
    row = pl.BlockSpec((tm, D_MODEL), lambda i, f: (i, 0))
    vec = pl.BlockSpec((1, D_MODEL), lambda i, f: (0, 0))
    bvec = pl.BlockSpec((None, 1, D_MODEL), lambda i, f: (i * tm // rows_per_batch, 0, 0))
    in_specs = [row, pl.BlockSpec((D_MODEL, tf), lambda i, f: (0, f)),
                pl.BlockSpec((tf, D_MODEL), lambda i, f: (f, 0)), row, bvec, vec]
    args = [h2, w1, w2, x, gt, g_post.reshape(1, D_MODEL)]
    out_specs = [row]
    out_shape = [jax.ShapeDtypeStruct((m, D_MODEL), F32)]
    if emit_next:
        g_pre, sh, sc = nxt
        in_specs += [vec, bvec, bvec]
        args += [g_pre.reshape(1, D_MODEL), sh, sc]
        out_specs.append(row)
        out_shape.append(jax.ShapeDtypeStruct((m, D_MODEL), BF16))
    res = pl.pallas_call(
        functools.partial(_mlp_kernel, emit_next),
        grid=(m // tm, D_FF // tf),
        in_specs=in_specs,
        out_specs=out_specs,
        out_shape=out_shape,
        scratch_shapes=[pltpu.VMEM((tm, D_MODEL), F32)],
        compiler_params=_cparams(2, 48 * 1024 * 1024),
        name="mlp",
    )(*args)
    return (res[0], res[1]) if emit_next else (res[0], None)


def kernel(x, c, ctx, c_ctx, w_mod, b_mod, g_pre_mix, g_post_mix, g_pre_mlp, g_post_mlp, w_in,
           ssm_lam_re, ssm_lam_im, ssm_log_dt, ssm_b_re, ssm_b_im, ssm_c_re, ssm_c_im, ssm_d,
           w_ssm_val, w_ssm_glu, na_rpb, w_na_proj, w_out, w_fc1, w_fc2):
    bsz, t_len, _ = x.shape
    c_len = ctx.shape[1]
    rows = t_len // GRID_W
    m_lat, m_ctx = bsz * t_len, bsz * c_len
    assert bsz < V7X_SUBLANES and t_len % (GRID_W * 8) == 0 and c_len % S5_CHUNK == 0

    c_rows = jnp.concatenate([c, c_ctx[None, :], jnp.zeros((V7X_SUBLANES - bsz - 1, D_MODEL), F32)], axis=0)
    mod = _modulation(c_rows, w_mod, b_mod)

    def mod_vectors(l):
        lat = [v[:, None, :] for v in jnp.split(mod[l, :bsz], N_MOD, axis=-1)]
        cvec = [jnp.broadcast_to(v[None, :, :], (bsz, 1, D_MODEL))
                for v in jnp.split(mod[l, bsz:bsz + 1], N_MOD, axis=-1)]
        return lat, cvec

    mods = [mod_vectors(l) for l in range(DEPTH)]
    rope_tabs = _rope_tables(t_len)

    xl = x.reshape(m_lat, D_MODEL)
    xc = ctx.reshape(m_ctx, D_MODEL)
    h = _norm_mod(xl, g_pre_mix[0], mods[0][0][0], mods[0][0][1], t_len)
    hc = _norm_mod(xc, g_pre_mix[0], mods[0][1][0], mods[0][1][1], c_len)

    for l in range(DEPTH):
        with_ctx_out = l < DEPTH - 1
        (sh1, sc1, gt1, sh2, sc2, gt2), (csh1, csc1, cgt1, csh2, csc2, cgt2) = mods[l]
        w_in_l = w_in[l].astype(BF16)
        w_val_l, w_glu_l = w_ssm_val[l].astype(BF16), w_ssm_glu[l].astype(BF16)
        w_na_l, w_out_l = w_na_proj[l].astype(BF16), w_out[l].astype(BF16)
        w1_l, w2_l = w_fc1[l].astype(BF16), w_fc2[l].astype(BF16)

        proj = _matmul(h, w_in_l, 1024, 1024, F32)
        projc = _matmul(hc, w_in_l, 1024, 1024, F32)
        proj3 = proj.reshape(bsz, t_len, PROJ_WIDTH)
        projc3 = projc.reshape(bsz, c_len, PROJ_WIDTH)

        mats = _s5_matrices(ssm_lam_re[l], ssm_lam_im[l], ssm_log_dt[l], ssm_b_re[l], ssm_b_im[l],
                            ssm_c_re[l], ssm_c_im[l], ssm_d[l])
        a_lat, a_ctx = _s5_branch(proj3[..., :SSM_WIDTH], projc3[..., :SSM_WIDTH], mats)

        att = _neighbourhood_attention(proj3, projc3, rope_tabs, _bias_table(na_rpb[l]))
        z = _merge(a_lat.reshape(m_lat, SSM_WIDTH), att.reshape(m_lat, NA_WIDTH), proj, w_val_l, w_glu_l, w_na_l)
        xl, h2 = _outproj(z, w_out_l, xl, gt1, g_post_mix[l], g_pre_mlp[l], sh2, sc2, t_len)
        nxt = (g_pre_mix[l + 1], mods[l + 1][0][0], mods[l + 1][0][1]) if with_ctx_out else None
        xl, h = _mlp(h2, w1_l, w2_l, xl, gt2, g_post_mlp[l], nxt, t_len)

        if with_ctx_out:
            attc = _context_attention(projc3)
            zc = _merge(a_ctx.reshape(m_ctx, SSM_WIDTH), attc.reshape(m_ctx, NA_WIDTH), projc,
                        w_val_l, w_glu_l, w_na_l)
            xc, h2c = _outproj(zc, w_out_l, xc, cgt1, g_post_mix[l], g_pre_mlp[l], csh2, csc2, c_len)
            nxtc = (g_pre_mix[l + 1], mods[l + 1][1][0], mods[l + 1][1][1])
            xc, hc = _mlp(h2c, w1_l, w2_l, xc, cgt2, g_post_mlp[l], nxtc, c_len)

    return xl.reshape(bsz, t_len, D_MODEL)
```

```python
import functools
import math

import numpy as np
import jax
import jax.numpy as jnp
from jax import lax
from jax.experimental import pallas as pl
from jax.experimental.pallas import tpu as pltpu

D_MODEL = 2048
DEPTH = 2
GRID_W = 64
N_MOD = 6
SSM_WIDTH = D_MODEL // 2
SSM_GROUP = 16
SSM_GROUPS = SSM_WIDTH // SSM_GROUP
SSM_STATE = 64
NA_HEADS = 16
NA_HEAD_DIM = 64
NA_WIDTH = NA_HEADS * NA_HEAD_DIM
WIN_R = 8
WIN_C = 16
ROPE_BASE = 10000.0
D_FF = 4 * D_MODEL
NORM_EPS = 1e-6
PROJ_WIDTH = SSM_WIDTH + 3 * NA_WIDTH + 2 * D_MODEL

V7X_LANES = 128
V7X_SUBLANES = 8
V7X_VMEM_BYTES = 64 * 1024 * 1024
HEAD_PAIRS = NA_HEADS * NA_HEAD_DIM // V7X_LANES
S5_CHUNK = 16
S5_TILE = S5_CHUNK * SSM_GROUP
S5_SCAN_TILE = V7X_SUBLANES
S5_GROUPS_PER_BLOCK = V7X_LANES // SSM_GROUP
S5_GROUPS_PER_PASS = 4
MASK_VALUE = -1e30

F32 = jnp.float32
BF16 = jnp.bfloat16


def _cparams(n_axes, vmem_bytes):
    assert vmem_bytes <= V7X_VMEM_BYTES
    return pltpu.CompilerParams(dimension_semantics=("arbitrary",) * n_axes, vmem_limit_bytes=vmem_bytes)


def _sigmoid(x):
    return 1.0 / (1.0 + jnp.exp(-x))


def _gelu_tanh(x):
    return 0.5 * x * (1.0 + jnp.tanh(math.sqrt(2.0 / math.pi) * (x + 0.044715 * (x * x * x))))


def _rms(x, g):
    return x * lax.rsqrt(jnp.mean(x * x, axis=-1, keepdims=True) + NORM_EPS) * g


def _mod_kernel(c_ref, w_ref, b_ref, o_ref):
    c = c_ref[...]
    s = (c * _sigmoid(c)).astype(BF16)
    o_ref[...] = jnp.dot(s, w_ref[...].astype(BF16), preferred_element_type=F32) + b_ref[...]


def _modulation(c_rows, w_mod, b_mod, tn=1024):
    n = w_mod.shape[-1]
    return pl.pallas_call(
        _mod_kernel,
        grid=(DEPTH, n // tn),
        in_specs=[pl.BlockSpec((V7X_SUBLANES, D_MODEL), lambda l, j: (0, 0)),
                  pl.BlockSpec((None, D_MODEL, tn), lambda l, j: (l, 0, j)),
                  pl.BlockSpec((None, 1, tn), lambda l, j: (l, 0, j))],
        out_specs=pl.BlockSpec((None, V7X_SUBLANES, tn), lambda l, j: (l, 0, j)),
        out_shape=jax.ShapeDtypeStruct((DEPTH, V7X_SUBLANES, n), F32),
        compiler_params=_cparams(2, 40 * 1024 * 1024),
        name="modulation",
    )(c_rows, w_mod, b_mod.reshape(DEPTH, 1, n))


def _norm_mod_kernel(x_ref, g_ref, sh_ref, sc_ref, o_ref):
    y = _rms(x_ref[...], g_ref[...])
    o_ref[...] = (y * (1.0 + sc_ref[...]) + sh_ref[...]).astype(o_ref.dtype)


def _norm_mod(x, g, sh, sc, rows_per_batch, tm=512):
    m = x.shape[0]
    bidx = lambda i: (i * tm // rows_per_batch, 0, 0)
    return pl.pallas_call(
        _norm_mod_kernel,
        grid=(m // tm,),
        in_specs=[pl.BlockSpec((tm, D_MODEL), lambda i: (i, 0)),
                  pl.BlockSpec((1, D_MODEL), lambda i: (0, 0)),
                  pl.BlockSpec((None, 1, D_MODEL), bidx),
                  pl.BlockSpec((None, 1, D_MODEL), bidx)],
        out_specs=pl.BlockSpec((tm, D_MODEL), lambda i: (i, 0)),
        out_shape=jax.ShapeDtypeStruct((m, D_MODEL), BF16),
        compiler_params=_cparams(1, 32 * 1024 * 1024),
        name="norm_mod",
    )(x, g.reshape(1, D_MODEL), sh, sc)


def _matmul_kernel(a_ref, w_ref, o_ref):
    o_ref[...] = jnp.dot(a_ref[...], w_ref[...], preferred_element_type=F32).astype(o_ref.dtype)


def _matmul(a, w, layer, tm, tn, out_dtype):
    m, k = a.shape
    n = w.shape[2]
    return pl.pallas_call(
        _matmul_kernel,
        grid=(n // tn, m // tm),
        in_specs=[pl.BlockSpec((tm, k), lambda j, i: (i, 0)),
                  pl.BlockSpec((None, k, tn), lambda j, i: (layer, 0, j))],
        out_specs=pl.BlockSpec((tm, tn), lambda j, i: (i, j)),
        out_shape=jax.ShapeDtypeStruct((m, n), out_dtype),
        compiler_params=_cparams(2, 40 * 1024 * 1024),
        name="in_proj",
    )(a, w)


def _s5_matrices(lam_re, lam_im, log_dt, b_re, b_im, c_re, c_im, d_skip):
    L = S5_CHUNK
    def cmul(a, b):
        return a[0] * b[0] - a[1] * b[1], a[0] * b[1] + a[1] * b[0]

    lam = (lam_re.astype(F32), lam_im.astype(F32))
    dt = jnp.exp(log_dt.astype(F32))[..., None]
    tau = jnp.arange(L + 1, dtype=F32)[None, None, :, None]
    mag = jnp.exp((lam[0] * dt)[:, :, None, :] * tau)
    ang = (lam[1] * dt)[:, :, None, :] * tau
    pw = (mag * jnp.cos(ang), mag * jnp.sin(ang))
    num = (pw[0][:, :, 1] - 1.0, pw[1][:, :, 1])
    den = lam[0] * lam[0] + lam[1] * lam[1]
    coef = ((num[0] * lam[0] + num[1] * lam[1]) / den, (num[1] * lam[0] - num[0] * lam[1]) / den)
    b_bar = cmul((coef[0][..., None], coef[1][..., None]), (b_re.astype(F32), b_im.astype(F32)))
    b_bar_t = (jnp.swapaxes(b_bar[0], -1, -2), jnp.swapaxes(b_bar[1], -1, -2))
    cmat = (c_re.astype(F32), c_im.astype(F32))

    def pw_at(d, sl):
        return pw[0][d][:, sl, None, :], pw[1][d][:, sl, None, :]

    c_lag = cmul((cmat[0][:, :, None], cmat[1][:, :, None]),
                 (pw[0][:, :, :L, None, :], pw[1][:, :, :L, None, :]))
    hi = lax.Precision.HIGHEST
    kern = (jnp.einsum('dglcp,dgpi->dglci', c_lag[0], b_bar[0], precision=hi)
            - jnp.einsum('dglcp,dgpi->dglci', c_lag[1], b_bar[1], precision=hi))
    ti = np.arange(L)[:, None]
    to = np.arange(L)[None, :]
    lag_f = np.clip(to - ti, 0, L - 1)
    lag_r = np.clip(ti - to, 0, L - 1)
    a_f = kern[0][:, lag_f] * jnp.asarray((to >= ti), F32)[None, :, :, None, None]
    a_r = kern[1][:, lag_r] * jnp.asarray((ti >= to), F32)[None, :, :, None, None]
    m_intra = (a_f + a_r).transpose(0, 1, 4, 2, 3).reshape(SSM_GROUPS, S5_TILE, S5_TILE)
    m_intra = m_intra + jax.vmap(jnp.diag)(jnp.tile(d_skip.astype(F32).reshape(SSM_GROUPS, 1, SSM_GROUP),
                                                    (1, L, 1)).reshape(SSM_GROUPS, S5_TILE))

    bf = cmul(pw_at(0, slice(L - 1, None, -1)), (b_bar_t[0][0][:, None], b_bar_t[1][0][:, None]))
    br = cmul(pw_at(1, slice(0, L)), (b_bar_t[0][1][:, None], b_bar_t[1][1][:, None]))
    b_in = jnp.concatenate([bf[0], br[0], bf[1], br[1]], axis=-1)
    b_in = b_in.reshape(SSM_GROUPS, S5_TILE, 4 * SSM_STATE)

    zf = cmul((cmat[0][0][:, None], cmat[1][0][:, None]), pw_at(0, slice(1, L + 1)))
    zr = cmul((cmat[0][1][:, None], cmat[1][1][:, None]), pw_at(1, slice(L, 0, -1)))
    c_out = jnp.concatenate([zf[0], zr[0], -zf[1], -zr[1]], axis=-1)
    c_out = c_out.transpose(0, 3, 1, 2).reshape(SSM_GROUPS, 4 * SSM_STATE, S5_TILE)

    w1 = jnp.concatenate([m_intra, b_in], axis=-1).astype(BF16)
    return (w1, c_out.astype(BF16)) + _s5_tables(lam_re, lam_im, log_dt)


def _s5_tables(lam_re, lam_im, log_dt):
    n = S5_SCAN_TILE
    dt = jnp.exp(log_dt.astype(F32))[..., None]
    steps = jnp.arange(n + 1, dtype=F32)[None, None, :, None] * S5_CHUNK
    mag = jnp.exp((lam_re.astype(F32) * dt)[:, :, None, :] * steps)
    ang = (lam_im.astype(F32) * dt)[:, :, None, :] * steps
    pre, pim = mag * jnp.cos(ang), mag * jnp.sin(ang)
    both = lambda f, r: jnp.concatenate([f, r], axis=-1)
    sel = np.array([1, 2, 4, 8])
    apow = jnp.stack([both(pre[0][:, sel], pre[1][:, sel]), both(pim[0][:, sel], pim[1][:, sel])], axis=2)
    coef = jnp.stack([both(pre[0][:, :n], pre[1][:, n - 1::-1][:, :n]),
                      both(pim[0][:, :n], pim[1][:, n - 1::-1][:, :n])], axis=1)
    return apow, coef


def _s5_kernel(bsz, nc_ctx, nc_lat, *refs):
    L = S5_CHUNK
    xl, xc = refs[:L], refs[L:2 * L]
    w1_ref, cout_ref, apow_ref, coef_ref = refs[2 * L:2 * L + 4]
    ol, oc = refs[2 * L + 4:2 * L + 6]
    xcat, y_scr, s_scr, ea_scr, eb_scr, acat, esel = refs[2 * L + 6:]
    nb = nc_ctx + nc_lat
    n_rows = bsz * nb
    lanes = V7X_LANES
    half = 2 * SSM_STATE
    tile = S5_SCAN_TILE
    gpb, gpp = S5_GROUPS_PER_BLOCK, S5_GROUPS_PER_PASS

    @pl.when(pl.program_id(0) == 0)
    def _():
        blk = 256
        for rb in range(esel.shape[0] // blk):
            r = lax.broadcasted_iota(jnp.int32, (blk, esel.shape[1]), 0) + rb * blk
            c = lax.broadcasted_iota(jnp.int32, (blk, esel.shape[1]), 1)
            dest = ((r % S5_TILE) // SSM_GROUP) * lanes + (r // S5_TILE) * SSM_GROUP + r % SSM_GROUP
            esel[rb * blk:(rb + 1) * blk, :] = jnp.where(dest == c, 1.0, 0.0).astype(BF16)

    for t in range(L):
        cols = slice(t * lanes, (t + 1) * lanes)
        for b in range(bsz):
            xcat[b * nb:b * nb + nc_ctx, cols] = xc[t][b].astype(BF16)
            xcat[b * nb + nc_ctx:(b + 1) * nb, cols] = xl[t][b].astype(BF16)

    col = lax.broadcasted_iota(jnp.int32, xcat.shape, 1)
    lane = lax.broadcasted_iota(jnp.int32, (tile, half), 1)
    row = lax.broadcasted_iota(jnp.int32, (tile, half), 0)
    is_fwd = lane < SSM_STATE
    shifts = (1, 2, 4)
    keep_dn = {s: is_fwd & (row >= s) for s in shifts}
    keep_up = {s: jnp.logical_not(is_fwd) & (row < tile - s) for s in shifts}

    def shift(d, s):
        return jnp.where(keep_dn[s], pltpu.roll(d, s, 0), jnp.where(keep_up[s], pltpu.roll(d, tile - s, 0), 0.0))

    def cfma(a_re, a_im, x_re, x_im, y_re, y_im):
        return a_re * x_re - a_im * x_im + y_re, a_re * x_im + a_im * x_re + y_im

    for p in range(gpb // gpp):
        for q in range(gpp):
            g = p * gpp + q
            x_g = jnp.where((col // SSM_GROUP) % gpb == g, xcat[...], jnp.zeros((), BF16))
            w_g = jnp.concatenate([w1_ref[g, t * SSM_GROUP:(t + 1) * SSM_GROUP, :]
                                   for t in range(L) for _ in range(gpb)], axis=0)
            ys = jnp.dot(x_g, w_g, preferred_element_type=F32)
            y_scr[q] = ys[:, :S5_TILE]
            s_scr[q] = ys[:, S5_TILE:]

        bc = lambda v: jnp.broadcast_to(v, (tile, half))
        a_pow = [[(bc(apow_ref[p * gpp + q, k, 0:1, :]), bc(apow_ref[p * gpp + q, k, 1:2, :])) for k in range(4)]
                 for q in range(gpp)]
        coefs = [(coef_ref[p * gpp + q, 0], coef_ref[p * gpp + q, 1]) for q in range(gpp)]

        for b in range(bsz):
            def make_body(rev_base, base=b * nb):
                def body(i, carry):
                    rf = pl.multiple_of(base + i * tile, tile)
                    rr = pl.multiple_of(base + (rev_base - i) * tile, tile)
                    out = []
                    for q in range(gpp):
                        e_re, e_im = carry[2 * q], carry[2 * q + 1]
                        sf = s_scr[q, pl.ds(rf, tile), :]
                        sr = s_scr[q, pl.ds(rr, tile), :]
                        d_re = jnp.where(is_fwd, sf[:, :half], sr[:, :half])
                        d_im = jnp.where(is_fwd, sf[:, half:], sr[:, half:])
                        for k, s in enumerate(shifts):
                            d_re, d_im = cfma(a_pow[q][k][0], a_pow[q][k][1], shift(d_re, s), shift(d_im, s), d_re, d_im)
                        en_re, en_im = cfma(coefs[q][0], coefs[q][1], e_re, e_im, shift(d_re, 1), shift(d_im, 1))
                        enter = jnp.concatenate([en_re, en_im], axis=-1)
                        ea_scr[q, pl.ds(rf, tile), :] = enter
                        eb_scr[q, pl.ds(rr, tile), :] = enter
                        last_re = jnp.where(is_fwd, bc(d_re[tile - 1:tile]), bc(d_re[0:1]))
                        last_im = jnp.where(is_fwd, bc(d_im[tile - 1:tile]), bc(d_im[0:1]))
                        out.extend(cfma(a_pow[q][3][0], a_pow[q][3][1], e_re, e_im, last_re, last_im))
                    return tuple(out)
                return body

            zero = jnp.zeros((tile, half), F32)
            nt_ctx, nt = nc_ctx // tile, nb // tile
            carry = lax.fori_loop(0, nt_ctx, make_body(nt_ctx - 1), tuple(zero for _ in range(2 * gpp)))
            lax.fori_loop(nt_ctx, nt, make_body(nt - 1 + nt_ctx), carry)

        lane_all = lax.broadcasted_iota(jnp.int32, (n_rows, 2 * half), 1)
        fwd_all = (lane_all % half) < SSM_STATE
        for q in range(gpp):
            g = p * gpp + q
            enter = jnp.where(fwd_all, ea_scr[q], eb_scr[q]).astype(BF16)
            y = y_scr[q] + jnp.dot(enter, cout_ref[g], preferred_element_type=F32)
            acat[:, g * S5_TILE:(g + 1) * S5_TILE] = _gelu_tanh(y).astype(BF16)

    for t2 in range(L // 2):
        cols = slice(t2 * 2 * lanes, (t2 + 1) * 2 * lanes)
        a_nat = jnp.dot(acat[...], esel[:, cols], preferred_element_type=F32).astype(BF16)
        for dt_ in range(2):
            t = 2 * t2 + dt_
            for b in range(bsz):
                oc[t, b] = a_nat[b * nb:b * nb + nc_ctx, dt_ * lanes:(dt_ + 1) * lanes]
                ol[t, b] = a_nat[b * nb + nc_ctx:(b + 1) * nb, dt_ * lanes:(dt_ + 1) * lanes]


def _s5_branch(proj, projc, mats, bsz, t_len, c_len):
    w1, c_out, apow, coef = mats
    L = S5_CHUNK
    nc_lat, nc_ctx = t_len // L, c_len // L
    assert nc_lat % S5_SCAN_TILE == 0 and nc_ctx % S5_SCAN_TILE == 0
    n_rows = bsz * (nc_lat + nc_ctx)
    pw = proj.shape[1]
    gpb = S5_GROUPS_PER_BLOCK
    n_blocks = SSM_GROUPS // gpb
    proj4 = proj.reshape(bsz, nc_lat, L * pw)
    projc4 = projc.reshape(bsz, nc_ctx, L * pw)
    in_slab = lambda n, t: pl.BlockSpec((bsz, n, V7X_LANES), lambda j, t=t: (0, 0, t * (pw // V7X_LANES) + j))
    out_slab = lambda n: pl.BlockSpec((L, bsz, n, V7X_LANES), lambda j: (0, 0, 0, j))
    grp = lambda *tail: pl.BlockSpec((gpb,) + tail, lambda j: (j,) + (0,) * len(tail))
    res = pl.pallas_call(
        functools.partial(_s5_kernel, bsz, nc_ctx, nc_lat),
        grid=(n_blocks,),
        in_specs=([in_slab(nc_lat, t) for t in range(L)] + [in_slab(nc_ctx, t) for t in range(L)]
                  + [grp(S5_TILE, 2 * S5_TILE), grp(S5_TILE, S5_TILE), grp(4, 2, 2 * SSM_STATE),
                     grp(2, S5_SCAN_TILE, 2 * SSM_STATE)]),
        out_specs=[out_slab(nc_lat), out_slab(nc_ctx)],
        out_shape=[jax.ShapeDtypeStruct((L, bsz, nc_lat, SSM_WIDTH), BF16),
                   jax.ShapeDtypeStruct((L, bsz, nc_ctx, SSM_WIDTH), BF16)],
        scratch_shapes=[pltpu.VMEM((n_rows, L * V7X_LANES), BF16)]
                       + [pltpu.VMEM((S5_GROUPS_PER_PASS, n_rows, S5_TILE), F32)] * 4
                       + [pltpu.VMEM((n_rows, gpb * S5_TILE), BF16),
                          pltpu.VMEM((gpb * S5_TILE, L * V7X_LANES), BF16)],
        compiler_params=_cparams(1, 56 * 1024 * 1024),
        name="s5",
    )(*([proj4] * L + [projc4] * L + [w1, c_out, apow, coef]))
    return res


def _rope_tables(t_len):
    nf = NA_HEAD_DIM // 4
    inv = ROPE_BASE ** (-jnp.arange(nf, dtype=F32) / nf)
    pos = jnp.arange(t_len, dtype=jnp.int32)
    ang_r = (pos // GRID_W).astype(F32)[:, None] * inv
    ang_c = (pos % GRID_W).astype(F32)[:, None] * inv
    cos = jnp.concatenate([jnp.cos(ang_r)] * 2 + [jnp.cos(ang_c)] * 2, axis=-1)
    sin = jnp.concatenate([-jnp.sin(ang_r), jnp.sin(ang_r), -jnp.sin(ang_c), jnp.sin(ang_c)], axis=-1)
    reps = V7X_LANES // NA_HEAD_DIM
    return jnp.tile(cos, (1, reps)), jnp.tile(sin, (1, reps))


def _bias_table(rpb):
    kc = min(WIN_C, GRID_W)
    w = np.arange(GRID_W)
    c0 = np.clip(w - kc // 2, 0, GRID_W - kc)
    kcol = np.arange(GRID_W)
    valid = (kcol[None, :] >= c0[:, None]) & (kcol[None, :] < c0[:, None] + kc)
    col_off = np.clip(kcol[None, :] - w[:, None] + (WIN_C - 1), 0, 2 * WIN_C - 2)
    t = jnp.where(jnp.asarray(valid)[None, None], rpb.astype(F32)[:, :, col_off], MASK_VALUE)
    t = jnp.concatenate([t[:, :-1], t[:, 1:]], axis=-1)
    n_heads = V7X_LANES // NA_HEAD_DIM
    t = t.reshape(HEAD_PAIRS, n_heads, 2 * WIN_R - 2, GRID_W, 2 * GRID_W).transpose(0, 2, 1, 3, 4)
    return t.reshape(HEAD_PAIRS, 2 * WIN_R - 2, n_heads * GRID_W, 2 * GRID_W)


def _rope(x, cos, sin_signed):
    nf = NA_HEAD_DIM // 4
    lane = lax.broadcasted_iota(jnp.int32, x.shape, 1)
    first = (lane % (2 * nf)) < nf
    swapped = jnp.where(first, pltpu.roll(x, V7X_LANES - nf, 1), pltpu.roll(x, nf, 1))
    return x * cos + swapped * sin_signed


def _dot_nt(a, b):
    return lax.dot_general(a, b, (((1,), (1,)), ((), ())), preferred_element_type=F32)


def _na_kernel(rows, kr, q_ref, k_ref, v_ref, kc_ref, vc_ref, cos_ref, sin_ref, bias_ref, o_ref,
               krope_scr, vb_scr, kcb_scr, vcb_scr, s_scr, p_scr):
    step = pl.program_id(2)
    dh = NA_HEAD_DIM
    t_len = k_ref.shape[0]
    rows_per_step = q_ref.shape[0] // GRID_W
    n_heads = V7X_LANES // dh

    @pl.when(step == 0)
    def _():
        chunk = 512
        for s in range(t_len // chunk):
            sl = pl.ds(s * chunk, chunk)
            krope_scr[sl, :] = _rope(k_ref[sl, :], cos_ref[sl, :], sin_ref[sl, :]).astype(BF16)
            vb_scr[sl, :] = v_ref[sl, :].astype(BF16)
        kcb_scr[...] = kc_ref[...].astype(BF16)
        vcb_scr[...] = vc_ref[...].astype(BF16)

    lane = lax.broadcasted_iota(jnp.int32, (GRID_W, V7X_LANES), 1)
    n_loc = kr * GRID_W
    wins = []
    for i in range(rows_per_step):
        r = step * rows_per_step + i
        r0 = jnp.clip(r - kr // 2, 0, rows - kr)
        ro0 = r0 - r + (WIN_R - 1)
        win = pl.ds(pl.multiple_of(r0 * GRID_W, GRID_W), n_loc)
        wins.append(win)
        qrow = pl.ds(pl.multiple_of(r * GRID_W, GRID_W), GRID_W)
        q = _rope(q_ref[pl.ds(i * GRID_W, GRID_W), :], cos_ref[qrow, :], sin_ref[qrow, :]) * (dh ** -0.5)
        q2 = jnp.concatenate([jnp.where((lane // dh) == h, q, 0.0) for h in range(n_heads)], axis=0).astype(BF16)
        s_loc = _dot_nt(q2, krope_scr[win, :])
        for j in range(kr // 2):
            cols = slice(j * V7X_LANES, (j + 1) * V7X_LANES)
            s_scr[i, :, cols] = s_loc[:, cols] + bias_ref[ro0 + 2 * j]
        s_scr[i, :, n_loc:] = _dot_nt(q2, kcb_scr[...])
    for i in range(rows_per_step):
        s = s_scr[i]
        m = jnp.max(s, axis=-1, keepdims=True)
        p = jnp.exp(s - m)
        denom = jnp.sum(p, axis=-1, keepdims=True)
        p_scr[i] = p.astype(BF16)
        o = (jnp.dot(p_scr[i, :, :n_loc], vb_scr[wins[i], :], preferred_element_type=F32)
             + jnp.dot(p_scr[i, :, n_loc:], vcb_scr[...], preferred_element_type=F32)) / denom
        out = o[:GRID_W]
        for h in range(1, n_heads):
            out = jnp.where((lane // dh) == h, o[h * GRID_W:(h + 1) * GRID_W], out)
        o_ref[pl.ds(i * GRID_W, GRID_W), :] = out.astype(o_ref.dtype)


def _neighbourhood_attention(proj, projc, rope_tabs, bias_tab, rows_per_step=4):
    bsz, t_len, _ = proj.shape
    c_len = projc.shape[1]
    rows = t_len // GRID_W
    kr = min(WIN_R, rows)
    assert kr % 2 == 0 and rows % rows_per_step == 0
    qb, kb, vb = (SSM_WIDTH // V7X_LANES, (SSM_WIDTH + NA_WIDTH) // V7X_LANES,
                  (SSM_WIDTH + 2 * NA_WIDTH) // V7X_LANES)
    cos, sin = rope_tabs
    tq = rows_per_step * GRID_W
    seq_spec = lambda off: pl.BlockSpec((None, t_len, V7X_LANES), lambda b, hp, r: (b, 0, off + hp))
    ctx_spec = lambda off: pl.BlockSpec((None, c_len, V7X_LANES), lambda b, hp, r: (b, 0, off + hp))
    tab_spec = pl.BlockSpec((t_len, V7X_LANES), lambda b, hp, r: (0, 0))
    bias_spec = pl.BlockSpec((None,) + bias_tab.shape[1:], lambda b, hp, r: (hp, 0, 0, 0))
    return pl.pallas_call(
        functools.partial(_na_kernel, rows, kr),
        grid=(bsz, HEAD_PAIRS, rows // rows_per_step),
        in_specs=[pl.BlockSpec((None, tq, V7X_LANES), lambda b, hp, r: (b, r, qb + hp)),
                  seq_spec(kb), seq_spec(vb), ctx_spec(kb), ctx_spec(vb), tab_spec, tab_spec, bias_spec],
        out_specs=pl.BlockSpec((None, tq, V7X_LANES), lambda b, hp, r: (b, r, hp)),
        out_shape=jax.ShapeDtypeStruct((bsz, t_len, NA_WIDTH), BF16),
        scratch_shapes=[pltpu.VMEM((t_len, V7X_LANES), BF16), pltpu.VMEM((t_len, V7X_LANES), BF16),
                        pltpu.VMEM((c_len, V7X_LANES), BF16), pltpu.VMEM((c_len, V7X_LANES), BF16),
                        pltpu.VMEM((rows_per_step, 2 * GRID_W, kr * GRID_W + c_len), F32),
                        pltpu.VMEM((rows_per_step, 2 * GRID_W, kr * GRID_W + c_len), BF16)],
        compiler_params=_cparams(3, 32 * 1024 * 1024),
        name="neighbourhood_attention",
    )(proj, proj, proj, projc, projc, cos, sin, bias_tab)


def _ctx_attn_kernel(q_ref, k_ref, v_ref, o_ref):
    dh = NA_HEAD_DIM
    q = (q_ref[...] * (dh ** -0.5)).astype(BF16)
    k = k_ref[...].astype(BF16)
    v = v_ref[...].astype(BF16)
    for h in range(V7X_LANES // dh):
        hs = slice(h * dh, (h + 1) * dh)
        s = _dot_nt(q[:, hs], k[:, hs])
        p = jnp.exp(s - jnp.max(s, axis=-1, keepdims=True))
        o = jnp.dot(p.astype(BF16), v[:, hs], preferred_element_type=F32)
        o_ref[:, hs] = (o / jnp.sum(p, axis=-1, keepdims=True)).astype(o_ref.dtype)


def _context_attention(projc):
    bsz, c_len, _ = projc.shape
    qb, kb, vb = (SSM_WIDTH // V7X_LANES, (SSM_WIDTH + NA_WIDTH) // V7X_LANES,
                  (SSM_WIDTH + 2 * NA_WIDTH) // V7X_LANES)
    spec = lambda off: pl.BlockSpec((None, c_len, V7X_LANES), lambda b, hp: (b, 0, off + hp))
    return pl.pallas_call(
        _ctx_attn_kernel,
        grid=(bsz, HEAD_PAIRS),
        in_specs=[spec(qb), spec(kb), spec(vb)],
        out_specs=spec(0),
        out_shape=jax.ShapeDtypeStruct((bsz, c_len, NA_WIDTH), BF16),
        compiler_params=_cparams(2, 32 * 1024 * 1024),
        name="context_attention",
    )(projc, projc, projc)


def _merge_kernel(a_ref, att_ref, gs_ref, gn_ref, wv_ref, wg_ref, wn_ref, o_ref):
    a = a_ref[...]
    val = jnp.dot(a, wv_ref[...], preferred_element_type=F32)
    glu = jnp.dot(a, wg_ref[...], preferred_element_type=F32)
    br_n = jnp.dot(att_ref[...], wn_ref[...], preferred_element_type=F32)
    z = _sigmoid(gs_ref[...]) * (val * _sigmoid(glu)) + _sigmoid(gn_ref[...]) * br_n
    o_ref[...] = z.astype(o_ref.dtype)


def _merge(a_tm, att, proj, layer, w_val, w_glu, w_na, tn=512):
    L = S5_CHUNK
    rows = a_tm.shape[1]
    pw = proj.shape[1]
    gs_off = (SSM_WIDTH + 3 * NA_WIDTH) // tn
    gn_off = gs_off + D_MODEL // tn
    wsp = pl.BlockSpec((None, SSM_WIDTH, tn), lambda j, t: (layer, 0, j))
    proj_v = proj.reshape(rows, L * pw)
    z = pl.pallas_call(
        _merge_kernel,
        grid=(D_MODEL // tn, L),
        in_specs=[pl.BlockSpec((None, rows, SSM_WIDTH), lambda j, t: (t, 0, 0)),
                  pl.BlockSpec((rows, NA_WIDTH), lambda j, t: (0, t)),
                  pl.BlockSpec((rows, tn), lambda j, t: (0, t * (pw // tn) + gs_off + j)),
                  pl.BlockSpec((rows, tn), lambda j, t: (0, t * (pw // tn) + gn_off + j)),
                  wsp, wsp, wsp],
        out_specs=pl.BlockSpec((rows, tn), lambda j, t: (0, t * (D_MODEL // tn) + j)),
        out_shape=jax.ShapeDtypeStruct((rows, L * D_MODEL), BF16),
        compiler_params=_cparams(2, 40 * 1024 * 1024),
        name="merge",
    )(a_tm, att.reshape(rows, L * NA_WIDTH), proj_v, proj_v, w_val, w_glu, w_na)
    return z.reshape(rows * L, D_MODEL)


def _outproj_kernel(z_ref, w_ref, x_ref, gt_ref, gpost_ref, gpre_ref, sh_ref, sc_ref, x_out, h_out):
    out = jnp.dot(z_ref[...], w_ref[...], preferred_element_type=F32)
    x1 = x_ref[...] + gt_ref[...] * _rms(out, gpost_ref[...])
    x_out[...] = x1
    h_out[...] = (_rms(x1, gpre_ref[...]) * (1.0 + sc_ref[...]) + sh_ref[...]).astype(h_out.dtype)


def _outproj(z, w_out, layer, x, gt, g_post, g_pre, sh, sc, rows_per_batch, tm=256):
    m = z.shape[0]
    row = pl.BlockSpec((tm, D_MODEL), lambda i: (i, 0))
    vec = pl.BlockSpec((1, D_MODEL), lambda i: (0, 0))
    bvec = pl.BlockSpec((None, 1, D_MODEL), lambda i: (i * tm // rows_per_batch, 0, 0))
    return pl.pallas_call(
        _outproj_kernel,
        grid=(m // tm,),
        in_specs=[row, pl.BlockSpec((None, D_MODEL, D_MODEL), lambda i: (layer, 0, 0)), row, bvec, vec, vec, bvec,
                  bvec],
        out_specs=[row, row],
        out_shape=[jax.ShapeDtypeStruct((m, D_MODEL), F32), jax.ShapeDtypeStruct((m, D_MODEL), BF16)],
        compiler_params=_cparams(1, 48 * 1024 * 1024),
        name="out_proj",
    )(z, w_out, x, gt, g_post.reshape(1, D_MODEL), g_pre.reshape(1, D_MODEL), sh, sc)


def _mlp_kernel(emit_next, h_ref, w1_ref, w2_ref, x_ref, gt_ref, gpost_ref, *rest):
    if emit_next:
        gpre_ref, sh_ref, sc_ref, x_out, h_out, acc = rest
    else:
        x_out, acc = rest
    f = pl.program_id(1)

    @pl.when(f == 0)
    def _():
        acc[...] = jnp.zeros_like(acc)

    a = jnp.maximum(jnp.dot(h_ref[...], w1_ref[...], preferred_element_type=F32), 0.0)
    acc[...] += jnp.dot((a * a).astype(BF16), w2_ref[...], preferred_element_type=F32)

    @pl.when(f == pl.num_programs(1) - 1)
    def _():
        x2 = x_ref[...] + gt_ref[...] * _rms(acc[...], gpost_ref[...])
        x_out[...] = x2
        if emit_next:
            h_out[...] = (_rms(x2, gpre_ref[...]) * (1.0 + sc_ref[...]) + sh_ref[...]).astype(h_out.dtype)


def _mlp(h2, w1, w2, layer, x, gt, g_post, nxt, rows_per_batch, tm=512, tf=512):
    m = h2.shape[0]
    emit_next = nxt is not None
    row = pl.BlockSpec((tm, D_MODEL), lambda i, f: (i, 0))
    vec = pl.BlockSpec((1, D_MODEL), lambda i, f: (0, 0))
    bvec = pl.BlockSpec((None, 1, D_MODEL), lambda i, f: (i * tm // rows_per_batch, 0, 0))
    in_specs = [row, pl.BlockSpec((None, D_MODEL, tf), lambda i, f: (layer, 0, f)),
                pl.BlockSpec((None, tf, D_MODEL), lambda i, f: (layer, f, 0)), row, bvec, vec]
    args = [h2, w1, w2, x, gt, g_post.reshape(1, D_MODEL)]
    out_specs = [row]
    out_shape = [jax.ShapeDtypeStruct((m, D_MODEL), F32)]
    if emit_next:
        g_pre, sh, sc = nxt
        in_specs += [vec, bvec, bvec]
        args += [g_pre.reshape(1, D_MODEL), sh, sc]
        out_specs.append(row)
        out_shape.append(jax.ShapeDtypeStruct((m, D_MODEL), BF16))
    res = pl.pallas_call(
        functools.partial(_mlp_kernel, emit_next),
        grid=(m // tm, D_FF // tf),
        in_specs=in_specs,
        out_specs=out_specs,
        out_shape=out_shape,
        scratch_shapes=[pltpu.VMEM((tm, D_MODEL), F32)],
        compiler_params=_cparams(2, 48 * 1024 * 1024),
        name="mlp",
    )(*args)
    return (res[0], res[1]) if emit_next else (res[0], None)


def kernel(x, c, ctx, c_ctx, w_mod, b_mod, g_pre_mix, g_post_mix, g_pre_mlp, g_post_mlp, w_in,
           ssm_lam_re, ssm_lam_im, ssm_log_dt, ssm_b_re, ssm_b_im, ssm_c_re, ssm_c_im, ssm_d,
           w_ssm_val, w_ssm_glu, na_rpb, w_na_proj, w_out, w_fc1, w_fc2):
    bsz, t_len, _ = x.shape
    c_len = ctx.shape[1]
    m_lat, m_ctx = bsz * t_len, bsz * c_len
    assert bsz < V7X_SUBLANES and t_len % (GRID_W * 8) == 0 and c_len % S5_CHUNK == 0

    c_rows = jnp.concatenate([c, c_ctx[None, :], jnp.zeros((V7X_SUBLANES - bsz - 1, D_MODEL), F32)], axis=0)
    mod = _modulation(c_rows, w_mod, b_mod)

    def mod_vectors(l):
        lat = [v[:, None, :] for v in jnp.split(mod[l, :bsz], N_MOD, axis=-1)]
        cvec = [jnp.broadcast_to(v[None, :, :], (bsz, 1, D_MODEL))
                for v in jnp.split(mod[l, bsz:bsz + 1], N_MOD, axis=-1)]
        return lat, cvec

    mods = [mod_vectors(l) for l in range(DEPTH)]
    rope_tabs = _rope_tables(t_len)

    xl = x.reshape(m_lat, D_MODEL)
    xc = ctx.reshape(m_ctx, D_MODEL)
    h = _norm_mod(xl, g_pre_mix[0], mods[0][0][0], mods[0][0][1], t_len)
    hc = _norm_mod(xc, g_pre_mix[0], mods[0][1][0], mods[0][1][1], c_len)

    w_in_b, w_val_b, w_glu_b = w_in.astype(BF16), w_ssm_val.astype(BF16), w_ssm_glu.astype(BF16)
    w_na_b, w_out_b = w_na_proj.astype(BF16), w_out.astype(BF16)
    w1_b, w2_b = w_fc1.astype(BF16), w_fc2.astype(BF16)

    for l in range(DEPTH):
        with_ctx_out = l < DEPTH - 1
        (sh1, sc1, gt1, sh2, sc2, gt2), (csh1, csc1, cgt1, csh2, csc2, cgt2) = mods[l]

        proj = _matmul(h, w_in_b, l, 1024, 1024, F32)
        projc = _matmul(hc, w_in_b, l, 1024, 1024, F32)
        proj3 = proj.reshape(bsz, t_len, PROJ_WIDTH)
        projc3 = projc.reshape(bsz, c_len, PROJ_WIDTH)

        mats = _s5_matrices(ssm_lam_re[l], ssm_lam_im[l], ssm_log_dt[l], ssm_b_re[l], ssm_b_im[l],
                            ssm_c_re[l], ssm_c_im[l], ssm_d[l])
        a_lat, a_ctx = _s5_branch(proj, projc, mats, bsz, t_len, c_len)

        att = _neighbourhood_attention(proj3, projc3, rope_tabs, _bias_table(na_rpb[l]))
        z = _merge(a_lat.reshape(S5_CHUNK, m_lat // S5_CHUNK, SSM_WIDTH), att.reshape(m_lat, NA_WIDTH), proj, l,
                   w_val_b, w_glu_b, w_na_b)
        xl, h2 = _outproj(z, w_out_b, l, xl, gt1, g_post_mix[l], g_pre_mlp[l], sh2, sc2, t_len)
        nxt = (g_pre_mix[l + 1], mods[l + 1][0][0], mods[l + 1][0][1]) if with_ctx_out else None
        xl, h = _mlp(h2, w1_b, w2_b, l, xl, gt2, g_post_mlp[l], nxt, t_len)

        if with_ctx_out:
            attc = _context_attention(projc3)
            zc = _merge(a_ctx.reshape(S5_CHUNK, m_ctx // S5_CHUNK, SSM_WIDTH), attc.reshape(m_ctx, NA_WIDTH), projc, l,
                        w_val_b, w_glu_b, w_na_b)
            xc, h2c = _outproj(zc, w_out_b, l, xc, cgt1, g_post_mix[l], g_pre_mlp[l], csh2, csc2, c_len)
            nxtc = (g_pre_mix[l + 1], mods[l + 1][1][0], mods[l + 1][1][1])
            xc, hc = _mlp(h2c, w1_b, w2_b, l, xc, cgt2, g_post_mlp[l], nxtc, c_len)

    return xl.reshape(bsz, t_len, D_MODEL)
```

```python
import functools
import math

import numpy as np
import jax
import jax.numpy as jnp
from jax import lax
from jax.experimental import pallas as pl
from jax.experimental.pallas import tpu as pltpu

D_MODEL = 2048
DEPTH = 2
GRID_W = 64
N_MOD = 6
SSM_WIDTH = D_MODEL // 2
SSM_GROUP = 16
SSM_GROUPS = SSM_WIDTH // SSM_GROUP
SSM_STATE = 64
NA_HEADS = 16
NA_HEAD_DIM = 64
NA_WIDTH = NA_HEADS * NA_HEAD_DIM
WIN_R = 8
WIN_C = 16
ROPE_BASE = 10000.0
D_FF = 4 * D_MODEL
NORM_EPS = 1e-6
PROJ_WIDTH = SSM_WIDTH + 3 * NA_WIDTH + 2 * D_MODEL

V7X_LANES = 128
V7X_SUBLANES = 8
V7X_VMEM_BYTES = 64 * 1024 * 1024
HEAD_PAIRS = NA_HEADS * NA_HEAD_DIM // V7X_LANES
S5_CHUNK = 16
S5_TILE = S5_CHUNK * SSM_GROUP
S5_SCAN_TILE = V7X_SUBLANES
S5_GROUPS_PER_BLOCK = V7X_LANES // SSM_GROUP
S5_GROUPS_PER_PASS = 4
MASK_VALUE = -1e30

F32 = jnp.float32
BF16 = jnp.bfloat16


def _cparams(n_axes, vmem_bytes):
    assert vmem_bytes <= V7X_VMEM_BYTES
    return pltpu.CompilerParams(dimension_semantics=("arbitrary",) * n_axes, vmem_limit_bytes=vmem_bytes)


def _sigmoid(x):
    return 1.0 / (1.0 + jnp.exp(-x))


def _gelu_tanh(x):
    return 0.5 * x * (1.0 + jnp.tanh(math.sqrt(2.0 / math.pi) * (x + 0.044715 * (x * x * x))))


def _rms(x, g):
    return x * lax.rsqrt(jnp.mean(x * x, axis=-1, keepdims=True) + NORM_EPS) * g


def _mod_kernel(c_ref, w_ref, b_ref, o_ref):
    c = c_ref[...]
    s = (c * _sigmoid(c)).astype(BF16)
    o_ref[...] = jnp.dot(s, w_ref[...].astype(BF16), preferred_element_type=F32) + b_ref[...]


def _modulation(c_rows, w_mod, b_mod, tn=1024):
    n = w_mod.shape[-1]
    return pl.pallas_call(
        _mod_kernel,
        grid=(DEPTH, n // tn),
        in_specs=[pl.BlockSpec((V7X_SUBLANES, D_MODEL), lambda l, j: (0, 0)),
                  pl.BlockSpec((None, D_MODEL, tn), lambda l, j: (l, 0, j)),
                  pl.BlockSpec((None, 1, tn), lambda l, j: (l, 0, j))],
        out_specs=pl.BlockSpec((None, V7X_SUBLANES, tn), lambda l, j: (l, 0, j)),
        out_shape=jax.ShapeDtypeStruct((DEPTH, V7X_SUBLANES, n), F32),
        compiler_params=_cparams(2, 40 * 1024 * 1024),
        name="modulation",
    )(c_rows, w_mod, b_mod.reshape(DEPTH, 1, n))


def _norm_mod_kernel(x_ref, g_ref, sh_ref, sc_ref, o_ref):
    y = _rms(x_ref[...], g_ref[...])
    o_ref[...] = (y * (1.0 + sc_ref[...]) + sh_ref[...]).astype(o_ref.dtype)


def _norm_mod(x, g, sh, sc, rows_per_batch, tm=512):
    m = x.shape[0]
    bidx = lambda i: (i * tm // rows_per_batch, 0, 0)
    return pl.pallas_call(
        _norm_mod_kernel,
        grid=(m // tm,),
        in_specs=[pl.BlockSpec((tm, D_MODEL), lambda i: (i, 0)),
                  pl.BlockSpec((1, D_MODEL), lambda i: (0, 0)),
                  pl.BlockSpec((None, 1, D_MODEL), bidx),
                  pl.BlockSpec((None, 1, D_MODEL), bidx)],
        out_specs=pl.BlockSpec((tm, D_MODEL), lambda i: (i, 0)),
        out_shape=jax.ShapeDtypeStruct((m, D_MODEL), BF16),
        compiler_params=_cparams(1, 32 * 1024 * 1024),
        name="norm_mod",
    )(x, g.reshape(1, D_MODEL), sh, sc)


def _matmul_kernel(a_ref, w_ref, o_ref):
    o_ref[...] = jnp.dot(a_ref[...], w_ref[...], preferred_element_type=F32).astype(o_ref.dtype)


def _matmul(a, w, layer, tm, tn, out_dtype, first_tile=0, n_tiles=None, out_cols=None):
    m, k = a.shape
    n_tiles = w.shape[2] // tn - first_tile if n_tiles is None else n_tiles
    out_first = first_tile if out_cols is not None else 0
    out_cols = n_tiles * tn if out_cols is None else out_cols
    return pl.pallas_call(
        _matmul_kernel,
        grid=(n_tiles, m // tm),
        in_specs=[pl.BlockSpec((tm, k), lambda j, i: (i, 0)),
                  pl.BlockSpec((None, k, tn), lambda j, i: (layer, 0, first_tile + j))],
        out_specs=pl.BlockSpec((tm, tn), lambda j, i: (i, out_first + j)),
        out_shape=jax.ShapeDtypeStruct((m, out_cols), out_dtype),
        compiler_params=_cparams(2, 40 * 1024 * 1024),
        name="in_proj",
    )(a, w)


def _s5_matrices(lam_re, lam_im, log_dt, b_re, b_im, c_re, c_im, d_skip):
    L = S5_CHUNK
    def cmul(a, b):
        return a[0] * b[0] - a[1] * b[1], a[0] * b[1] + a[1] * b[0]

    lam = (lam_re.astype(F32), lam_im.astype(F32))
    dt = jnp.exp(log_dt.astype(F32))[..., None]
    tau = jnp.arange(L + 1, dtype=F32)[None, None, :, None]
    mag = jnp.exp((lam[0] * dt)[:, :, None, :] * tau)
    ang = (lam[1] * dt)[:, :, None, :] * tau
    pw = (mag * jnp.cos(ang), mag * jnp.sin(ang))
    num = (pw[0][:, :, 1] - 1.0, pw[1][:, :, 1])
    den = lam[0] * lam[0] + lam[1] * lam[1]
    coef = ((num[0] * lam[0] + num[1] * lam[1]) / den, (num[1] * lam[0] - num[0] * lam[1]) / den)
    b_bar = cmul((coef[0][..., None], coef[1][..., None]), (b_re.astype(F32), b_im.astype(F32)))
    b_bar_t = (jnp.swapaxes(b_bar[0], -1, -2), jnp.swapaxes(b_bar[1], -1, -2))
    cmat = (c_re.astype(F32), c_im.astype(F32))

    def pw_at(d, sl):
        return pw[0][d][:, sl, None, :], pw[1][d][:, sl, None, :]

    c_lag = cmul((cmat[0][:, :, None], cmat[1][:, :, None]),
                 (pw[0][:, :, :L, None, :], pw[1][:, :, :L, None, :]))
    hi = lax.Precision.HIGHEST
    kern = (jnp.einsum('dglcp,dgpi->dglci', c_lag[0], b_bar[0], precision=hi)
            - jnp.einsum('dglcp,dgpi->dglci', c_lag[1], b_bar[1], precision=hi))
    ti = np.arange(L)[:, None]
    to = np.arange(L)[None, :]
    lag_f = np.clip(to - ti, 0, L - 1)
    lag_r = np.clip(ti - to, 0, L - 1)
    a_f = kern[0][:, lag_f] * jnp.asarray((to >= ti), F32)[None, :, :, None, None]
    a_r = kern[1][:, lag_r] * jnp.asarray((ti >= to), F32)[None, :, :, None, None]
    m_intra = (a_f + a_r).transpose(0, 1, 4, 2, 3).reshape(SSM_GROUPS, S5_TILE, S5_TILE)
    m_intra = m_intra + jax.vmap(jnp.diag)(jnp.tile(d_skip.astype(F32).reshape(SSM_GROUPS, 1, SSM_GROUP),
                                                    (1, L, 1)).reshape(SSM_GROUPS, S5_TILE))

    bf = cmul(pw_at(0, slice(L - 1, None, -1)), (b_bar_t[0][0][:, None], b_bar_t[1][0][:, None]))
    br = cmul(pw_at(1, slice(0, L)), (b_bar_t[0][1][:, None], b_bar_t[1][1][:, None]))
    b_in = jnp.concatenate([bf[0], br[0], bf[1], br[1]], axis=-1)
    b_in = b_in.reshape(SSM_GROUPS, S5_TILE, 4 * SSM_STATE)

    zf = cmul((cmat[0][0][:, None], cmat[1][0][:, None]), pw_at(0, slice(1, L + 1)))
    zr = cmul((cmat[0][1][:, None], cmat[1][1][:, None]), pw_at(1, slice(L, 0, -1)))
    c_out = jnp.concatenate([zf[0], zr[0], -zf[1], -zr[1]], axis=-1)
    c_out = c_out.transpose(0, 3, 1, 2).reshape(SSM_GROUPS, 4 * SSM_STATE, S5_TILE)

    w1 = jnp.concatenate([m_intra, b_in], axis=-1).astype(BF16)
    return (w1, c_out.astype(BF16)) + _s5_tables(lam_re, lam_im, log_dt)


def _s5_tables(lam_re, lam_im, log_dt):
    n = S5_SCAN_TILE
    dt = jnp.exp(log_dt.astype(F32))[..., None]
    steps = jnp.arange(n + 1, dtype=F32)[None, None, :, None] * S5_CHUNK
    mag = jnp.exp((lam_re.astype(F32) * dt)[:, :, None, :] * steps)
    ang = (lam_im.astype(F32) * dt)[:, :, None, :] * steps
    pre, pim = mag * jnp.cos(ang), mag * jnp.sin(ang)
    both = lambda f, r: jnp.concatenate([f, r], axis=-1)
    sel = np.array([1, 2, 4, 8])
    apow = jnp.stack([both(pre[0][:, sel], pre[1][:, sel]), both(pim[0][:, sel], pim[1][:, sel])], axis=2)
    coef = jnp.stack([both(pre[0][:, :n], pre[1][:, n - 1::-1][:, :n]),
                      both(pim[0][:, :n], pim[1][:, n - 1::-1][:, :n])], axis=1)
    return apow, coef


def _s5_kernel(bsz, nc_ctx, nc_lat, *refs):
    L = S5_CHUNK
    xl, xc, w1_ref, cout_ref, apow_ref, coef_ref, ol, oc = refs[:8]
    xcat, y_scr, s_scr, ea_scr, eb_scr, acat, esel = refs[8:]
    nb = nc_ctx + nc_lat
    n_rows = bsz * nb
    lanes = V7X_LANES
    half = 2 * SSM_STATE
    tile = S5_SCAN_TILE
    gpb, gpp = S5_GROUPS_PER_BLOCK, S5_GROUPS_PER_PASS

    @pl.when(pl.program_id(0) == 0)
    def _():
        blk = 256
        for rb in range(esel.shape[0] // blk):
            r = lax.broadcasted_iota(jnp.int32, (blk, esel.shape[1]), 0) + rb * blk
            c = lax.broadcasted_iota(jnp.int32, (blk, esel.shape[1]), 1)
            dest = ((r % S5_TILE) // SSM_GROUP) * lanes + (r // S5_TILE) * SSM_GROUP + r % SSM_GROUP
            esel[rb * blk:(rb + 1) * blk, :] = jnp.where(dest == c, 1.0, 0.0).astype(BF16)

    for t in range(L):
        cols = slice(t * lanes, (t + 1) * lanes)
        for b in range(bsz):
            xcat[b * nb:b * nb + nc_ctx, cols] = xc[t, b]
            xcat[b * nb + nc_ctx:(b + 1) * nb, cols] = xl[t, b]

    col = lax.broadcasted_iota(jnp.int32, xcat.shape, 1)
    lane = lax.broadcasted_iota(jnp.int32, (tile, half), 1)
    row = lax.broadcasted_iota(jnp.int32, (tile, half), 0)
    is_fwd = lane < SSM_STATE
    shifts = (1, 2, 4)
    keep_dn = {s: is_fwd & (row >= s) for s in shifts}
    keep_up = {s: jnp.logical_not(is_fwd) & (row < tile - s) for s in shifts}

    def shift(d, s):
        return jnp.where(keep_dn[s], pltpu.roll(d, s, 0), jnp.where(keep_up[s], pltpu.roll(d, tile - s, 0), 0.0))

    def cfma(a_re, a_im, x_re, x_im, y_re, y_im):
        return a_re * x_re - a_im * x_im + y_re, a_re * x_im + a_im * x_re + y_im

    for p in range(gpb // gpp):
        for q in range(gpp):
            g = p * gpp + q
            x_g = jnp.where((col // SSM_GROUP) % gpb == g, xcat[...], jnp.zeros((), BF16))
            w_g = jnp.concatenate([w1_ref[g, t * SSM_GROUP:(t + 1) * SSM_GROUP, :]
                                   for t in range(L) for _ in range(gpb)], axis=0)
            ys = jnp.dot(x_g, w_g, preferred_element_type=F32)
            y_scr[q] = ys[:, :S5_TILE]
            s_scr[q] = ys[:, S5_TILE:]

        bc = lambda v: jnp.broadcast_to(v, (tile, half))
        a_pow = [[(bc(apow_ref[p * gpp + q, k, 0:1, :]), bc(apow_ref[p * gpp + q, k, 1:2, :])) for k in range(4)]
                 for q in range(gpp)]
        coefs = [(coef_ref[p * gpp + q, 0], coef_ref[p * gpp + q, 1]) for q in range(gpp)]

        for b in range(bsz):
            def make_body(rev_base, base=b * nb):
                def body(i, carry):
                    rf = pl.multiple_of(base + i * tile, tile)
                    rr = pl.multiple_of(base + (rev_base - i) * tile, tile)
                    out = []
                    for q in range(gpp):
                        e_re, e_im = carry[2 * q], carry[2 * q + 1]
                        sf = s_scr[q, pl.ds(rf, tile), :]
                        sr = s_scr[q, pl.ds(rr, tile), :]
                        d_re = jnp.where(is_fwd, sf[:, :half], sr[:, :half])
                        d_im = jnp.where(is_fwd, sf[:, half:], sr[:, half:])
                        for k, s in enumerate(shifts):
                            d_re, d_im = cfma(a_pow[q][k][0], a_pow[q][k][1], shift(d_re, s), shift(d_im, s), d_re, d_im)
                        en_re, en_im = cfma(coefs[q][0], coefs[q][1], e_re, e_im, shift(d_re, 1), shift(d_im, 1))
                        enter = jnp.concatenate([en_re, en_im], axis=-1)
                        ea_scr[q, pl.ds(rf, tile), :] = enter
                        eb_scr[q, pl.ds(rr, tile), :] = enter
                        last_re = jnp.where(is_fwd, bc(d_re[tile - 1:tile]), bc(d_re[0:1]))
                        last_im = jnp.where(is_fwd, bc(d_im[tile - 1:tile]), bc(d_im[0:1]))
                        out.extend(cfma(a_pow[q][3][0], a_pow[q][3][1], e_re, e_im, last_re, last_im))
                    return tuple(out)
                return body

            zero = jnp.zeros((tile, half), F32)
            nt_ctx, nt = nc_ctx // tile, nb // tile
            carry = lax.fori_loop(0, nt_ctx, make_body(nt_ctx - 1), tuple(zero for _ in range(2 * gpp)))
            lax.fori_loop(nt_ctx, nt, make_body(nt - 1 + nt_ctx), carry)

        lane_all = lax.broadcasted_iota(jnp.int32, (n_rows, 2 * half), 1)
        fwd_all = (lane_all % half) < SSM_STATE
        for q in range(gpp):
            g = p * gpp + q
            enter = jnp.where(fwd_all, ea_scr[q], eb_scr[q]).astype(BF16)
            y = y_scr[q] + jnp.dot(enter, cout_ref[g], preferred_element_type=F32)
            acat[:, g * S5_TILE:(g + 1) * S5_TILE] = _gelu_tanh(y).astype(BF16)

    for t2 in range(L // 2):
        cols = slice(t2 * 2 * lanes, (t2 + 1) * 2 * lanes)
        a_nat = jnp.dot(acat[...], esel[:, cols], preferred_element_type=F32).astype(BF16)
        for dt_ in range(2):
            t = 2 * t2 + dt_
            for b in range(bsz):
                oc[t, b] = a_nat[b * nb:b * nb + nc_ctx, dt_ * lanes:(dt_ + 1) * lanes]
                ol[t, b] = a_nat[b * nb + nc_ctx:(b + 1) * nb, dt_ * lanes:(dt_ + 1) * lanes]


def _s5_branch(u_lat, u_ctx, mats, bsz, t_len, c_len):
    w1, c_out, apow, coef = mats
    L = S5_CHUNK
    nc_lat, nc_ctx = t_len // L, c_len // L
    assert nc_lat % S5_SCAN_TILE == 0 and nc_ctx % S5_SCAN_TILE == 0
    n_rows = bsz * (nc_lat + nc_ctx)
    gpb = S5_GROUPS_PER_BLOCK
    slab = lambda n: pl.BlockSpec((L, bsz, n, V7X_LANES), lambda j: (0, 0, 0, j))
    grp = lambda *tail: pl.BlockSpec((gpb,) + tail, lambda j: (j,) + (0,) * len(tail))
    return pl.pallas_call(
        functools.partial(_s5_kernel, bsz, nc_ctx, nc_lat),
        grid=(SSM_GROUPS // gpb,),
        in_specs=[slab(nc_lat), slab(nc_ctx), grp(S5_TILE, 2 * S5_TILE), grp(S5_TILE, S5_TILE),
                  grp(4, 2, 2 * SSM_STATE), grp(2, S5_SCAN_TILE, 2 * SSM_STATE)],
        out_specs=[slab(nc_lat), slab(nc_ctx)],
        out_shape=[jax.ShapeDtypeStruct((L, bsz, nc_lat, SSM_WIDTH), BF16),
                   jax.ShapeDtypeStruct((L, bsz, nc_ctx, SSM_WIDTH), BF16)],
        scratch_shapes=[pltpu.VMEM((n_rows, L * V7X_LANES), BF16)]
                       + [pltpu.VMEM((S5_GROUPS_PER_PASS, n_rows, S5_TILE), F32)] * 4
                       + [pltpu.VMEM((n_rows, gpb * S5_TILE), BF16),
                          pltpu.VMEM((gpb * S5_TILE, L * V7X_LANES), BF16)],
        compiler_params=_cparams(1, 56 * 1024 * 1024),
        name="s5",
    )(u_lat, u_ctx, w1, c_out, apow, coef)


def _rope_tables(t_len):
    nf = NA_HEAD_DIM // 4
    inv = ROPE_BASE ** (-jnp.arange(nf, dtype=F32) / nf)
    pos = jnp.arange(t_len, dtype=jnp.int32)
    ang_r = (pos // GRID_W).astype(F32)[:, None] * inv
    ang_c = (pos % GRID_W).astype(F32)[:, None] * inv
    cos = jnp.concatenate([jnp.cos(ang_r)] * 2 + [jnp.cos(ang_c)] * 2, axis=-1)
    sin = jnp.concatenate([-jnp.sin(ang_r), jnp.sin(ang_r), -jnp.sin(ang_c), jnp.sin(ang_c)], axis=-1)
    reps = V7X_LANES // NA_HEAD_DIM
    return jnp.tile(cos, (1, reps)), jnp.tile(sin, (1, reps))


def _bias_table(rpb):
    kc = min(WIN_C, GRID_W)
    w = np.arange(GRID_W)
    c0 = np.clip(w - kc // 2, 0, GRID_W - kc)
    kcol = np.arange(GRID_W)
    valid = (kcol[None, :] >= c0[:, None]) & (kcol[None, :] < c0[:, None] + kc)
    col_off = np.clip(kcol[None, :] - w[:, None] + (WIN_C - 1), 0, 2 * WIN_C - 2)
    t = jnp.where(jnp.asarray(valid)[None, None], rpb.astype(F32)[:, :, col_off], MASK_VALUE)
    t = jnp.concatenate([t[:, :-1], t[:, 1:]], axis=-1)
    n_heads = V7X_LANES // NA_HEAD_DIM
    t = t.reshape(HEAD_PAIRS, n_heads, 2 * WIN_R - 2, GRID_W, 2 * GRID_W).transpose(0, 2, 1, 3, 4)
    return t.reshape(HEAD_PAIRS, 2 * WIN_R - 2, n_heads * GRID_W, 2 * GRID_W)


def _rope(x, cos, sin_signed):
    nf = NA_HEAD_DIM // 4
    lane = lax.broadcasted_iota(jnp.int32, x.shape, 1)
    first = (lane % (2 * nf)) < nf
    swapped = jnp.where(first, pltpu.roll(x, V7X_LANES - nf, 1), pltpu.roll(x, nf, 1))
    return x * cos + swapped * sin_signed


def _dot_nt(a, b):
    return lax.dot_general(a, b, (((1,), (1,)), ((), ())), preferred_element_type=F32)


def _na_kernel(rows, kr, q_ref, k_ref, v_ref, kc_ref, vc_ref, cos_ref, sin_ref, bias_ref, o_ref,
               krope_scr, vb_scr, kcb_scr, vcb_scr, s_scr, p_scr):
    step = pl.program_id(2)
    dh = NA_HEAD_DIM
    t_len = k_ref.shape[0]
    rows_per_step = q_ref.shape[0] // GRID_W
    n_heads = V7X_LANES // dh

    @pl.when(step == 0)
    def _():
        chunk = 512
        for s in range(t_len // chunk):
            sl = pl.ds(s * chunk, chunk)
            krope_scr[sl, :] = _rope(k_ref[sl, :], cos_ref[sl, :], sin_ref[sl, :]).astype(BF16)
            vb_scr[sl, :] = v_ref[sl, :].astype(BF16)
        kcb_scr[...] = kc_ref[...].astype(BF16)
        vcb_scr[...] = vc_ref[...].astype(BF16)

    lane = lax.broadcasted_iota(jnp.int32, (GRID_W, V7X_LANES), 1)
    n_loc = kr * GRID_W
    wins = []
    for i in range(rows_per_step):
        r = step * rows_per_step + i
        r0 = jnp.clip(r - kr // 2, 0, rows - kr)
        ro0 = r0 - r + (WIN_R - 1)
        win = pl.ds(pl.multiple_of(r0 * GRID_W, GRID_W), n_loc)
        wins.append(win)
        qrow = pl.ds(pl.multiple_of(r * GRID_W, GRID_W), GRID_W)
        q = _rope(q_ref[pl.ds(i * GRID_W, GRID_W), :], cos_ref[qrow, :], sin_ref[qrow, :]) * (dh ** -0.5)
        q2 = jnp.concatenate([jnp.where((lane // dh) == h, q, 0.0) for h in range(n_heads)], axis=0).astype(BF16)
        s_loc = _dot_nt(q2, krope_scr[win, :])
        for j in range(kr // 2):
            cols = slice(j * V7X_LANES, (j + 1) * V7X_LANES)
            s_scr[i, :, cols] = s_loc[:, cols] + bias_ref[ro0 + 2 * j]
        s_scr[i, :, n_loc:] = _dot_nt(q2, kcb_scr[...])
    for i in range(rows_per_step):
        s = s_scr[i]
        m = jnp.max(s, axis=-1, keepdims=True)
        p = jnp.exp(s - m)
        denom = jnp.sum(p, axis=-1, keepdims=True)
        p_scr[i] = p.astype(BF16)
        o = (jnp.dot(p_scr[i, :, :n_loc], vb_scr[wins[i], :], preferred_element_type=F32)
             + jnp.dot(p_scr[i, :, n_loc:], vcb_scr[...], preferred_element_type=F32)) / denom
        out = o[:GRID_W]
        for h in range(1, n_heads):
            out = jnp.where((lane // dh) == h, o[h * GRID_W:(h + 1) * GRID_W], out)
        o_ref[pl.ds(i * GRID_W, GRID_W), :] = out.astype(o_ref.dtype)


def _neighbourhood_attention(proj, projc, rope_tabs, bias_tab, rows_per_step=4):
    bsz, t_len, _ = proj.shape
    c_len = projc.shape[1]
    rows = t_len // GRID_W
    kr = min(WIN_R, rows)
    assert kr % 2 == 0 and rows % rows_per_step == 0
    qb, kb, vb = (SSM_WIDTH // V7X_LANES, (SSM_WIDTH + NA_WIDTH) // V7X_LANES,
                  (SSM_WIDTH + 2 * NA_WIDTH) // V7X_LANES)
    cos, sin = rope_tabs
    tq = rows_per_step * GRID_W
    seq_spec = lambda off: pl.BlockSpec((None, t_len, V7X_LANES), lambda b, hp, r: (b, 0, off + hp))
    ctx_spec = lambda off: pl.BlockSpec((None, c_len, V7X_LANES), lambda b, hp, r: (b, 0, off + hp))
    tab_spec = pl.BlockSpec((t_len, V7X_LANES), lambda b, hp, r: (0, 0))
    bias_spec = pl.BlockSpec((None,) + bias_tab.shape[1:], lambda b, hp, r: (hp, 0, 0, 0))
    return pl.pallas_call(
        functools.partial(_na_kernel, rows, kr),
        grid=(bsz, HEAD_PAIRS, rows // rows_per_step),
        in_specs=[pl.BlockSpec((None, tq, V7X_LANES), lambda b, hp, r: (b, r, qb + hp)),
                  seq_spec(kb), seq_spec(vb), ctx_spec(kb), ctx_spec(vb), tab_spec, tab_spec, bias_spec],
        out_specs=pl.BlockSpec((None, tq, V7X_LANES), lambda b, hp, r: (b, r, hp)),
        out_shape=jax.ShapeDtypeStruct((bsz, t_len, NA_WIDTH), BF16),
        scratch_shapes=[pltpu.VMEM((t_len, V7X_LANES), BF16), pltpu.VMEM((t_len, V7X_LANES), BF16),
                        pltpu.VMEM((c_len, V7X_LANES), BF16), pltpu.VMEM((c_len, V7X_LANES), BF16),
                        pltpu.VMEM((rows_per_step, 2 * GRID_W, kr * GRID_W + c_len), F32),
                        pltpu.VMEM((rows_per_step, 2 * GRID_W, kr * GRID_W + c_len), BF16)],
        compiler_params=_cparams(3, 32 * 1024 * 1024),
        name="neighbourhood_attention",
    )(proj, proj, proj, projc, projc, cos, sin, bias_tab)


def _ctx_attn_kernel(q_ref, k_ref, v_ref, o_ref):
    dh = NA_HEAD_DIM
    q = (q_ref[...] * (dh ** -0.5)).astype(BF16)
    k = k_ref[...].astype(BF16)
    v = v_ref[...].astype(BF16)
    for h in range(V7X_LANES // dh):
        hs = slice(h * dh, (h + 1) * dh)
        s = _dot_nt(q[:, hs], k[:, hs])
        p = jnp.exp(s - jnp.max(s, axis=-1, keepdims=True))
        o = jnp.dot(p.astype(BF16), v[:, hs], preferred_element_type=F32)
        o_ref[:, hs] = (o / jnp.sum(p, axis=-1, keepdims=True)).astype(o_ref.dtype)


def _context_attention(projc):
    bsz, c_len, _ = projc.shape
    qb, kb, vb = (SSM_WIDTH // V7X_LANES, (SSM_WIDTH + NA_WIDTH) // V7X_LANES,
                  (SSM_WIDTH + 2 * NA_WIDTH) // V7X_LANES)
    spec = lambda off: pl.BlockSpec((None, c_len, V7X_LANES), lambda b, hp: (b, 0, off + hp))
    return pl.pallas_call(
        _ctx_attn_kernel,
        grid=(bsz, HEAD_PAIRS),
        in_specs=[spec(qb), spec(kb), spec(vb)],
        out_specs=spec(0),
        out_shape=jax.ShapeDtypeStruct((bsz, c_len, NA_WIDTH), BF16),
        compiler_params=_cparams(2, 32 * 1024 * 1024),
        name="context_attention",
    )(projc, projc, projc)


def _merge_kernel(a_ref, att_ref, gs_ref, gn_ref, wv_ref, wg_ref, wn_ref, o_ref):
    a = a_ref[...]
    val = jnp.dot(a, wv_ref[...], preferred_element_type=F32)
    glu = jnp.dot(a, wg_ref[...], preferred_element_type=F32)
    br_n = jnp.dot(att_ref[...], wn_ref[...], preferred_element_type=F32)
    z = _sigmoid(gs_ref[...]) * (val * _sigmoid(glu)) + _sigmoid(gn_ref[...]) * br_n
    o_ref[...] = z.astype(o_ref.dtype)


def _merge(a, att, proj, layer, w_val, w_glu, w_na, tm=512, tn=512):
    m = a.shape[0]
    gs_off = (SSM_WIDTH + 3 * NA_WIDTH) // tn
    gn_off = gs_off + D_MODEL // tn
    act = pl.BlockSpec((tm, SSM_WIDTH), lambda j, i: (i, 0))
    wsp = pl.BlockSpec((None, SSM_WIDTH, tn), lambda j, i: (layer, 0, j))
    return pl.pallas_call(
        _merge_kernel,
        grid=(D_MODEL // tn, m // tm),
        in_specs=[act, act,
                  pl.BlockSpec((tm, tn), lambda j, i: (i, gs_off + j)),
                  pl.BlockSpec((tm, tn), lambda j, i: (i, gn_off + j)),
                  wsp, wsp, wsp],
        out_specs=pl.BlockSpec((tm, tn), lambda j, i: (i, j)),
        out_shape=jax.ShapeDtypeStruct((m, D_MODEL), BF16),
        compiler_params=_cparams(2, 40 * 1024 * 1024),
        name="merge",
    )(a, att, proj, proj, w_val, w_glu, w_na)


def _outproj_kernel(z_ref, w_ref, x_ref, gt_ref, gpost_ref, gpre_ref, sh_ref, sc_ref, x_out, h_out):
    out = jnp.dot(z_ref[...], w_ref[...], preferred_element_type=F32)
    x1 = x_ref[...] + gt_ref[...] * _rms(out, gpost_ref[...])
    x_out[...] = x1
    h_out[...] = (_rms(x1, gpre_ref[...]) * (1.0 + sc_ref[...]) + sh_ref[...]).astype(h_out.dtype)


def _outproj(z, w_out, layer, x, gt, g_post, g_pre, sh, sc, rows_per_batch, tm=256):
    m = z.shape[0]
    row = pl.BlockSpec((tm, D_MODEL), lambda i: (i, 0))
    vec = pl.BlockSpec((1, D_MODEL), lambda i: (0, 0))
    bvec = pl.BlockSpec((None, 1, D_MODEL), lambda i: (i * tm // rows_per_batch, 0, 0))
    return pl.pallas_call(
        _outproj_kernel,
        grid=(m // tm,),
        in_specs=[row, pl.BlockSpec((None, D_MODEL, D_MODEL), lambda i: (layer, 0, 0)), row, bvec, vec, vec, bvec,
                  bvec],
        out_specs=[row, row],
        out_shape=[jax.ShapeDtypeStruct((m, D_MODEL), F32), jax.ShapeDtypeStruct((m, D_MODEL), BF16)],
        compiler_params=_cparams(1, 48 * 1024 * 1024),
        name="out_proj",
    )(z, w_out, x, gt, g_post.reshape(1, D_MODEL), g_pre.reshape(1, D_MODEL), sh, sc)


def _mlp_kernel(emit_next, h_ref, w1_ref, w2_ref, x_ref, gt_ref, gpost_ref, *rest):
    if emit_next:
        gpre_ref, sh_ref, sc_ref, x_out, h_out, acc = rest
    else:
        x_out, acc = rest
    f = pl.program_id(1)

    @pl.when(f == 0)
    def _():
        acc[...] = jnp.zeros_like(acc)

    a = jnp.maximum(jnp.dot(h_ref[...], w1_ref[...], preferred_element_type=F32), 0.0)
    acc[...] += jnp.dot((a * a).astype(BF16), w2_ref[...], preferred_element_type=F32)

    @pl.when(f == pl.num_programs(1) - 1)
    def _():
        x2 = x_ref[...] + gt_ref[...] * _rms(acc[...], gpost_ref[...])
        x_out[...] = x2
        if emit_next:
            h_out[...] = (_rms(x2, gpre_ref[...]) * (1.0 + sc_ref[...]) + sh_ref[...]).astype(h_out.dtype)


def _mlp(h2, w1, w2, layer, x, gt, g_post, nxt, rows_per_batch, tm=512, tf=512):
    m = h2.shape[0]
    emit_next = nxt is not None
    row = pl.BlockSpec((tm, D_MODEL), lambda i, f: (i, 0))
    vec = pl.BlockSpec((1, D_MODEL), lambda i, f: (0, 0))
    bvec = pl.BlockSpec((None, 1, D_MODEL), lambda i, f: (i * tm // rows_per_batch, 0, 0))
    in_specs = [row, pl.BlockSpec((None, D_MODEL, tf), lambda i, f: (layer, 0, f)),
                pl.BlockSpec((None, tf, D_MODEL), lambda i, f: (layer, f, 0)), row, bvec, vec]
    args = [h2, w1, w2, x, gt, g_post.reshape(1, D_MODEL)]
    out_specs = [row]
    out_shape = [jax.ShapeDtypeStruct((m, D_MODEL), F32)]
    if emit_next:
        g_pre, sh, sc = nxt
        in_specs += [vec, bvec, bvec]
        args += [g_pre.reshape(1, D_MODEL), sh, sc]
        out_specs.append(row)
        out_shape.append(jax.ShapeDtypeStruct((m, D_MODEL), BF16))
    res = pl.pallas_call(
        functools.partial(_mlp_kernel, emit_next),
        grid=(m // tm, D_FF // tf),
        in_specs=in_specs,
        out_specs=out_specs,
        out_shape=out_shape,
        scratch_shapes=[pltpu.VMEM((tm, D_MODEL), F32)],
        compiler_params=_cparams(2, 48 * 1024 * 1024),
        name="mlp",
    )(*args)
    return (res[0], res[1]) if emit_next else (res[0], None)


def kernel(x, c, ctx, c_ctx, w_mod, b_mod, g_pre_mix, g_post_mix, g_pre_mlp, g_post_mlp, w_in,
           ssm_lam_re, ssm_lam_im, ssm_log_dt, ssm_b_re, ssm_b_im, ssm_c_re, ssm_c_im, ssm_d,
           w_ssm_val, w_ssm_glu, na_rpb, w_na_proj, w_out, w_fc1, w_fc2):
    bsz, t_len, _ = x.shape
    c_len = ctx.shape[1]
    m_lat, m_ctx = bsz * t_len, bsz * c_len
    assert bsz < V7X_SUBLANES and t_len % (GRID_W * 8) == 0 and c_len % S5_CHUNK == 0

    c_rows = jnp.concatenate([c, c_ctx[None, :], jnp.zeros((V7X_SUBLANES - bsz - 1, D_MODEL), F32)], axis=0)
    mod = _modulation(c_rows, w_mod, b_mod)

    def mod_vectors(l):
        lat = [v[:, None, :] for v in jnp.split(mod[l, :bsz], N_MOD, axis=-1)]
        cvec = [jnp.broadcast_to(v[None, :, :], (bsz, 1, D_MODEL))
                for v in jnp.split(mod[l, bsz:bsz + 1], N_MOD, axis=-1)]
        return lat, cvec

    mods = [mod_vectors(l) for l in range(DEPTH)]
    rope_tabs = _rope_tables(t_len)

    xl = x.reshape(m_lat, D_MODEL)
    xc = ctx.reshape(m_ctx, D_MODEL)
    h = _norm_mod(xl, g_pre_mix[0], mods[0][0][0], mods[0][0][1], t_len)
    hc = _norm_mod(xc, g_pre_mix[0], mods[0][1][0], mods[0][1][1], c_len)

    w_in_b, w_val_b, w_glu_b = w_in.astype(BF16), w_ssm_val.astype(BF16), w_ssm_glu.astype(BF16)
    w_na_b, w_out_b = w_na_proj.astype(BF16), w_out.astype(BF16)
    w1_b, w2_b = w_fc1.astype(BF16), w_fc2.astype(BF16)

    def to_pos_major(v):
        m, w = v.shape
        return v.reshape(m // S5_CHUNK, S5_CHUNK, w).transpose(1, 0, 2).reshape(m, w)

    def from_pos_major(v):
        return v.transpose(1, 2, 0, 3).reshape(-1, v.shape[-1])

    for l in range(DEPTH):
        with_ctx_out = l < DEPTH - 1
        (sh1, sc1, gt1, sh2, sc2, gt2), (csh1, csc1, cgt1, csh2, csc2, cgt2) = mods[l]

        u_tiles = SSM_WIDTH // 1024
        proj = _matmul(h, w_in_b, l, 1024, 1024, F32, first_tile=u_tiles, out_cols=PROJ_WIDTH)
        projc = _matmul(hc, w_in_b, l, 1024, 1024, F32, first_tile=u_tiles, out_cols=PROJ_WIDTH)
        u_lat = _matmul(to_pos_major(h), w_in_b, l, 1024, 1024, BF16, n_tiles=u_tiles)
        u_ctx = _matmul(to_pos_major(hc), w_in_b, l, 1024, 1024, BF16, n_tiles=u_tiles)
        proj3 = proj.reshape(bsz, t_len, PROJ_WIDTH)
        projc3 = projc.reshape(bsz, c_len, PROJ_WIDTH)

        mats = _s5_matrices(ssm_lam_re[l], ssm_lam_im[l], ssm_log_dt[l], ssm_b_re[l], ssm_b_im[l],
                            ssm_c_re[l], ssm_c_im[l], ssm_d[l])
        a_lat, a_ctx = _s5_branch(u_lat.reshape(S5_CHUNK, bsz, t_len // S5_CHUNK, SSM_WIDTH),
                                  u_ctx.reshape(S5_CHUNK, bsz, c_len // S5_CHUNK, SSM_WIDTH), mats, bsz, t_len, c_len)

        att = _neighbourhood_attention(proj3, projc3, rope_tabs, _bias_table(na_rpb[l]))
        z = _merge(from_pos_major(a_lat), att.reshape(m_lat, NA_WIDTH), proj, l, w_val_b, w_glu_b, w_na_b)
        xl, h2 = _outproj(z, w_out_b, l, xl, gt1, g_post_mix[l], g_pre_mlp[l], sh2, sc2, t_len)
        nxt = (g_pre_mix[l + 1], mods[l + 1][0][0], mods[l + 1][0][1]) if with_ctx_out else None
        xl, h = _mlp(h2, w1_b, w2_b, l, xl, gt2, g_post_mlp[l], nxt, t_len)

        if with_ctx_out:
            attc = _context_attention(projc3)
            zc = _merge(from_pos_major(a_ctx), attc.reshape(m_ctx, NA_WIDTH), projc, l, w_val_b, w_glu_b, w_na_b)
            xc, h2c = _outproj(zc, w_out_b, l, xc, cgt1, g_post_mix[l], g_pre_mlp[l], csh2, csc2, c_len)
            nxtc = (g_pre_mix[l + 1], mods[l + 1][1][0], mods[l + 1][1][1])
            xc, hc = _mlp(h2c, w1_b, w2_b, l, xc, cgt2, g_post_mlp[l], nxtc, c_len)

    return xl.reshape(bsz, t_len, D_MODEL)
```

```python
import functools
import math

import numpy as np
import jax
import jax.numpy as jnp
from jax import lax
from jax.experimental import pallas as pl
from jax.experimental.pallas import tpu as pltpu

D_MODEL = 2048
DEPTH = 2
GRID_W = 64
N_MOD = 6
SSM_WIDTH = D_MODEL // 2
SSM_GROUP = 16
SSM_GROUPS = SSM_WIDTH // SSM_GROUP
SSM_STATE = 64
NA_HEADS = 16
NA_HEAD_DIM = 64
NA_WIDTH = NA_HEADS * NA_HEAD_DIM
WIN_R = 8
WIN_C = 16
ROPE_BASE = 10000.0
D_FF = 4 * D_MODEL
NORM_EPS = 1e-6
PROJ_WIDTH = SSM_WIDTH + 3 * NA_WIDTH + 2 * D_MODEL
REST_WIDTH = PROJ_WIDTH - SSM_WIDTH

V7X_LANES = 128
V7X_SUBLANES = 8
V7X_VMEM_BYTES = 64 * 1024 * 1024
HEAD_PAIRS = NA_HEADS * NA_HEAD_DIM // V7X_LANES
S5_CHUNK = 16
S5_TILE = S5_CHUNK * SSM_GROUP
S5_SCAN_TILE = V7X_SUBLANES
S5_GROUPS_PER_BLOCK = V7X_LANES // SSM_GROUP
S5_GROUPS_PER_PASS = 4
MASK_VALUE = -1e30

F32 = jnp.float32
BF16 = jnp.bfloat16


def _cparams(n_axes, vmem_bytes):
    assert vmem_bytes <= V7X_VMEM_BYTES
    return pltpu.CompilerParams(dimension_semantics=("arbitrary",) * n_axes, vmem_limit_bytes=vmem_bytes)


def _sigmoid(x):
    return 1.0 / (1.0 + jnp.exp(-x))


def _gelu_tanh(x):
    return 0.5 * x * (1.0 + jnp.tanh(math.sqrt(2.0 / math.pi) * (x + 0.044715 * (x * x * x))))


def _rms(x, g):
    return x * lax.rsqrt(jnp.mean(x * x, axis=-1, keepdims=True) + NORM_EPS) * g


def _mod_kernel(c_ref, w_ref, b_ref, o_ref):
    c = c_ref[...]
    s = (c * _sigmoid(c)).astype(BF16)
    o_ref[...] = jnp.dot(s, w_ref[...].astype(BF16), preferred_element_type=F32) + b_ref[...]


def _modulation(c_rows, w_mod, b_mod, tn=1024):
    n = w_mod.shape[-1]
    return pl.pallas_call(
        _mod_kernel,
        grid=(DEPTH, n // tn),
        in_specs=[pl.BlockSpec((V7X_SUBLANES, D_MODEL), lambda l, j: (0, 0)),
                  pl.BlockSpec((None, D_MODEL, tn), lambda l, j: (l, 0, j)),
                  pl.BlockSpec((None, 1, tn), lambda l, j: (l, 0, j))],
        out_specs=pl.BlockSpec((None, V7X_SUBLANES, tn), lambda l, j: (l, 0, j)),
        out_shape=jax.ShapeDtypeStruct((DEPTH, V7X_SUBLANES, n), F32),
        compiler_params=_cparams(2, 40 * 1024 * 1024),
        name="modulation",
    )(c_rows, w_mod, b_mod.reshape(DEPTH, 1, n))


def _norm_mod_kernel(x_ref, g_ref, sh_ref, sc_ref, o_ref):
    y = _rms(x_ref[...], g_ref[...])
    o_ref[...] = (y * (1.0 + sc_ref[...]) + sh_ref[...]).astype(o_ref.dtype)


def _norm_mod(x, g, sh, sc, rows_per_batch, tm=512):
    m = x.shape[0]
    bidx = lambda i: (i * tm // rows_per_batch, 0, 0)
    return pl.pallas_call(
        _norm_mod_kernel,
        grid=(m // tm,),
        in_specs=[pl.BlockSpec((tm, D_MODEL), lambda i: (i, 0)),
                  pl.BlockSpec((1, D_MODEL), lambda i: (0, 0)),
                  pl.BlockSpec((None, 1, D_MODEL), bidx),
                  pl.BlockSpec((None, 1, D_MODEL), bidx)],
        out_specs=pl.BlockSpec((tm, D_MODEL), lambda i: (i, 0)),
        out_shape=jax.ShapeDtypeStruct((m, D_MODEL), BF16),
        compiler_params=_cparams(1, 32 * 1024 * 1024),
        name="norm_mod",
    )(x, g.reshape(1, D_MODEL), sh, sc)


def _matmul_kernel(a_ref, w_ref, o_ref):
    o_ref[...] = jnp.dot(a_ref[...], w_ref[...], preferred_element_type=F32).astype(o_ref.dtype)


def _matmul(a, w, layer, tm, tn, out_dtype, first_tile=0, n_tiles=None):
    m, k = a.shape
    n_tiles = w.shape[2] // tn - first_tile if n_tiles is None else n_tiles
    return pl.pallas_call(
        _matmul_kernel,
        grid=(n_tiles, m // tm),
        in_specs=[pl.BlockSpec((tm, k), lambda j, i: (i, 0)),
                  pl.BlockSpec((None, k, tn), lambda j, i: (layer, 0, first_tile + j))],
        out_specs=pl.BlockSpec((tm, tn), lambda j, i: (i, j)),
        out_shape=jax.ShapeDtypeStruct((m, n_tiles * tn), out_dtype),
        compiler_params=_cparams(2, 40 * 1024 * 1024),
        name="in_proj",
    )(a, w)


def _s5_matrices(lam_re, lam_im, log_dt, b_re, b_im, c_re, c_im, d_skip):
    L = S5_CHUNK
    def cmul(a, b):
        return a[0] * b[0] - a[1] * b[1], a[0] * b[1] + a[1] * b[0]

    lam = (lam_re.astype(F32), lam_im.astype(F32))
    dt = jnp.exp(log_dt.astype(F32))[..., None]
    tau = jnp.arange(L + 1, dtype=F32)[None, None, :, None]
    mag = jnp.exp((lam[0] * dt)[:, :, None, :] * tau)
    ang = (lam[1] * dt)[:, :, None, :] * tau
    pw = (mag * jnp.cos(ang), mag * jnp.sin(ang))
    num = (pw[0][:, :, 1] - 1.0, pw[1][:, :, 1])
    den = lam[0] * lam[0] + lam[1] * lam[1]
    coef = ((num[0] * lam[0] + num[1] * lam[1]) / den, (num[1] * lam[0] - num[0] * lam[1]) / den)
    b_bar = cmul((coef[0][..., None], coef[1][..., None]), (b_re.astype(F32), b_im.astype(F32)))
    b_bar_t = (jnp.swapaxes(b_bar[0], -1, -2), jnp.swapaxes(b_bar[1], -1, -2))
    cmat = (c_re.astype(F32), c_im.astype(F32))

    def pw_at(d, sl):
        return pw[0][d][:, sl, None, :], pw[1][d][:, sl, None, :]

    c_lag = cmul((cmat[0][:, :, None], cmat[1][:, :, None]),
                 (pw[0][:, :, :L, None, :], pw[1][:, :, :L, None, :]))
    hi = lax.Precision.HIGHEST
    kern = (jnp.einsum('dglcp,dgpi->dglci', c_lag[0], b_bar[0], precision=hi)
            - jnp.einsum('dglcp,dgpi->dglci', c_lag[1], b_bar[1], precision=hi))
    kt = jnp.stack([kern[0].reshape(SSM_GROUPS, S5_TILE, SSM_GROUP),
                    kern[1][:, ::-1].reshape(SSM_GROUPS, S5_TILE, SSM_GROUP)], axis=1)
    d_rows = jnp.broadcast_to(d_skip.astype(F32).reshape(SSM_GROUPS, SSM_GROUP, 1), (SSM_GROUPS, SSM_GROUP, S5_TILE))

    bf = cmul(pw_at(0, slice(L - 1, None, -1)), (b_bar_t[0][0][:, None], b_bar_t[1][0][:, None]))
    br = cmul(pw_at(1, slice(0, L)), (b_bar_t[0][1][:, None], b_bar_t[1][1][:, None]))
    b_in = jnp.concatenate([bf[0], br[0], bf[1], br[1]], axis=-1)
    b_in = b_in.reshape(SSM_GROUPS, S5_TILE, 4 * SSM_STATE)

    zf = cmul((cmat[0][0][:, None], cmat[1][0][:, None]), pw_at(0, slice(1, L + 1)))
    zr = cmul((cmat[0][1][:, None], cmat[1][1][:, None]), pw_at(1, slice(L, 0, -1)))
    c_out_t = jnp.concatenate([zf[0], zr[0], -zf[1], -zr[1]], axis=-1).reshape(SSM_GROUPS, S5_TILE, 4 * SSM_STATE)

    return (kt, d_rows, b_in.astype(BF16), c_out_t.astype(BF16)) + _s5_tables(lam_re, lam_im, log_dt)


def _s5_tables(lam_re, lam_im, log_dt):
    n = S5_SCAN_TILE
    dt = jnp.exp(log_dt.astype(F32))[..., None]
    steps = jnp.arange(n + 1, dtype=F32)[None, None, :, None] * S5_CHUNK
    mag = jnp.exp((lam_re.astype(F32) * dt)[:, :, None, :] * steps)
    ang = (lam_im.astype(F32) * dt)[:, :, None, :] * steps
    pre, pim = mag * jnp.cos(ang), mag * jnp.sin(ang)
    both = lambda f, r: jnp.concatenate([f, r], axis=-1)
    sel = np.array([1, 2, 4, 8])
    apow = jnp.stack([both(pre[0][:, sel], pre[1][:, sel]), both(pim[0][:, sel], pim[1][:, sel])], axis=2)
    coef = jnp.stack([both(pre[0][:, :n], pre[1][:, n - 1::-1][:, :n]),
                      both(pim[0][:, :n], pim[1][:, n - 1::-1][:, :n])], axis=1)
    return apow, coef


def _dot_nt(a, b):
    return lax.dot_general(a, b, (((1,), (1,)), ((), ())), preferred_element_type=F32)


def _shift_lanes(x, s):
    w = V7X_LANES
    lo, hi = x[:, :w], x[:, w:]
    lane = lax.broadcasted_iota(jnp.int32, lo.shape, 1)
    if s == 0:
        return x
    if s > 0:
        q, r = divmod(s, w)
        if r == 0:
            out = (jnp.zeros_like(lo), lo)
        else:
            rl, rh = pltpu.roll(lo, r, 1), pltpu.roll(hi, r, 1)
            if q == 0:
                out = (jnp.where(lane >= r, rl, 0.0), jnp.where(lane >= r, rh, rl))
            else:
                out = (jnp.zeros_like(lo), jnp.where(lane >= r, rl, 0.0))
    else:
        q, r = divmod(-s, w)
        if r == 0:
            out = (hi, jnp.zeros_like(hi))
        else:
            rl, rh = pltpu.roll(lo, w - r, 1), pltpu.roll(hi, w - r, 1)
            if q == 0:
                out = (jnp.where(lane < w - r, rl, rh), jnp.where(lane < w - r, rh, 0.0))
            else:
                out = (jnp.where(lane < w - r, rh, 0.0), jnp.zeros_like(hi))
    return jnp.concatenate(out, axis=1)


def _s5_chunk_rows(kt, d_rows, bin_ref, g):
    L = S5_CHUNK
    eye = (lax.broadcasted_iota(jnp.int32, (SSM_GROUP, SSM_GROUP), 0)
           == lax.broadcasted_iota(jnp.int32, (SSM_GROUP, SSM_GROUP), 1)).astype(F32)
    to_rows = lambda m: lax.dot_general(eye, m, (((1,), (1,)), ((), ())), precision=lax.Precision.HIGHEST,
                                        preferred_element_type=F32)
    kl_f, kl_r = to_rows(kt[0]), to_rows(kt[1])
    lane = lax.broadcasted_iota(jnp.int32, (SSM_GROUP, S5_TILE), 1)
    row = lax.broadcasted_iota(jnp.int32, (SSM_GROUP, S5_TILE), 0)
    blocks = []
    for t in range(L):
        m_rows = (_shift_lanes(kl_f, t * SSM_GROUP) + _shift_lanes(kl_r, -(L - 1 - t) * SSM_GROUP)
                  + jnp.where(lane == t * SSM_GROUP + row, d_rows, 0.0))
        blocks.append(jnp.concatenate([m_rows.astype(BF16), bin_ref[g, t * SSM_GROUP:(t + 1) * SSM_GROUP, :]], axis=1))
    return blocks


def _s5_kernel(bsz, nc_ctx, nc_lat, *refs):
    L = S5_CHUNK
    xl, xc, kt_ref, d_ref, bin_ref, cout_ref, apow_ref, coef_ref, ol, oc = refs[:10]
    xcat, y_scr, s_scr, ea_scr, eb_scr, acat, esel = refs[10:]
    nb = nc_ctx + nc_lat
    n_rows = bsz * nb
    lanes = V7X_LANES
    half = 2 * SSM_STATE
    tile = S5_SCAN_TILE
    gpb, gpp = S5_GROUPS_PER_BLOCK, S5_GROUPS_PER_PASS

    @pl.when(pl.program_id(0) == 0)
    def _():
        blk = 256
        for rb in range(esel.shape[0] // blk):
            r = lax.broadcasted_iota(jnp.int32, (blk, esel.shape[1]), 0) + rb * blk
            c = lax.broadcasted_iota(jnp.int32, (blk, esel.shape[1]), 1)
            dest = ((r % S5_TILE) // SSM_GROUP) * lanes + (r // S5_TILE) * SSM_GROUP + r % SSM_GROUP
            esel[rb * blk:(rb + 1) * blk, :] = jnp.where(dest == c, 1.0, 0.0).astype(BF16)

    for t in range(L):
        cols = slice(t * lanes, (t + 1) * lanes)
        for b in range(bsz):
            xcat[b * nb:b * nb + nc_ctx, cols] = xc[t, b]
            xcat[b * nb + nc_ctx:(b + 1) * nb, cols] = xl[t, b]

    col = lax.broadcasted_iota(jnp.int32, xcat.shape, 1)
    lane = lax.broadcasted_iota(jnp.int32, (tile, half), 1)
    row = lax.broadcasted_iota(jnp.int32, (tile, half), 0)
    is_fwd = lane < SSM_STATE
    shifts = (1, 2, 4)
    keep_dn = {s: is_fwd & (row >= s) for s in shifts}
    keep_up = {s: jnp.logical_not(is_fwd) & (row < tile - s) for s in shifts}

    def shift(d, s):
        return jnp.where(keep_dn[s], pltpu.roll(d, s, 0), jnp.where(keep_up[s], pltpu.roll(d, tile - s, 0), 0.0))

    def cfma(a_re, a_im, x_re, x_im, y_re, y_im):
        return a_re * x_re - a_im * x_im + y_re, a_re * x_im + a_im * x_re + y_im

    for p in range(gpb // gpp):
        for q in range(gpp):
            g = p * gpp + q
            x_g = jnp.where((col // SSM_GROUP) % gpb == g, xcat[...], jnp.zeros((), BF16))
            w_g = jnp.concatenate([blk for blk in _s5_chunk_rows(kt_ref[g], d_ref[g], bin_ref, g) for _ in range(gpb)],
                                  axis=0)
            ys = jnp.dot(x_g, w_g, preferred_element_type=F32)
            y_scr[q] = ys[:, :S5_TILE]
            s_scr[q] = ys[:, S5_TILE:]

        bc = lambda v: jnp.broadcast_to(v, (tile, half))
        a_pow = [[(bc(apow_ref[p * gpp + q, k, 0:1, :]), bc(apow_ref[p * gpp + q, k, 1:2, :])) for k in range(4)]
                 for q in range(gpp)]
        coefs = [(coef_ref[p * gpp + q, 0], coef_ref[p * gpp + q, 1]) for q in range(gpp)]

        for b in range(bsz):
            def make_body(rev_base, base=b * nb):
                def body(i, carry):
                    rf = pl.multiple_of(base + i * tile, tile)
                    rr = pl.multiple_of(base + (rev_base - i) * tile, tile)
                    out = []
                    for q in range(gpp):
                        e_re, e_im = carry[2 * q], carry[2 * q + 1]
                        sf = s_scr[q, pl.ds(rf, tile), :]
                        sr = s_scr[q, pl.ds(rr, tile), :]
                        d_re = jnp.where(is_fwd, sf[:, :half], sr[:, :half])
                        d_im = jnp.where(is_fwd, sf[:, half:], sr[:, half:])
                        for k, s in enumerate(shifts):
                            d_re, d_im = cfma(a_pow[q][k][0], a_pow[q][k][1], shift(d_re, s), shift(d_im, s), d_re, d_im)
                        en_re, en_im = cfma(coefs[q][0], coefs[q][1], e_re, e_im, shift(d_re, 1), shift(d_im, 1))
                        enter = jnp.concatenate([en_re, en_im], axis=-1)
                        ea_scr[q, pl.ds(rf, tile), :] = enter
                        eb_scr[q, pl.ds(rr, tile), :] = enter
                        last_re = jnp.where(is_fwd, bc(d_re[tile - 1:tile]), bc(d_re[0:1]))
                        last_im = jnp.where(is_fwd, bc(d_im[tile - 1:tile]), bc(d_im[0:1]))
                        out.extend(cfma(a_pow[q][3][0], a_pow[q][3][1], e_re, e_im, last_re, last_im))
                    return tuple(out)
                return body

            zero = jnp.zeros((tile, half), F32)
            nt_ctx, nt = nc_ctx // tile, nb // tile
            carry = lax.fori_loop(0, nt_ctx, make_body(nt_ctx - 1), tuple(zero for _ in range(2 * gpp)))
            lax.fori_loop(nt_ctx, nt, make_body(nt - 1 + nt_ctx), carry)

        lane_all = lax.broadcasted_iota(jnp.int32, (n_rows, 2 * half), 1)
        fwd_all = (lane_all % half) < SSM_STATE
        for q in range(gpp):
            g = p * gpp + q
            enter = jnp.where(fwd_all, ea_scr[q], eb_scr[q]).astype(BF16)
            y = y_scr[q] + _dot_nt(enter, cout_ref[g])
            acat[:, g * S5_TILE:(g + 1) * S5_TILE] = _gelu_tanh(y).astype(BF16)

    for t2 in range(L // 2):
        cols = slice(t2 * 2 * lanes, (t2 + 1) * 2 * lanes)
        a_nat = jnp.dot(acat[...], esel[:, cols], preferred_element_type=F32).astype(BF16)
        for dt_ in range(2):
            t = 2 * t2 + dt_
            for b in range(bsz):
                oc[t, b] = a_nat[b * nb:b * nb + nc_ctx, dt_ * lanes:(dt_ + 1) * lanes]
                ol[t, b] = a_nat[b * nb + nc_ctx:(b + 1) * nb, dt_ * lanes:(dt_ + 1) * lanes]


def _s5_branch(u_lat, u_ctx, mats, bsz, t_len, c_len):
    kt, d_rows, b_in, c_out_t, apow, coef = mats
    L = S5_CHUNK
    nc_lat, nc_ctx = t_len // L, c_len // L
    assert nc_lat % S5_SCAN_TILE == 0 and nc_ctx % S5_SCAN_TILE == 0
    n_rows = bsz * (nc_lat + nc_ctx)
    gpb = S5_GROUPS_PER_BLOCK
    slab = lambda n: pl.BlockSpec((L, bsz, n, V7X_LANES), lambda j: (0, 0, 0, j))
    grp = lambda *tail: pl.BlockSpec((gpb,) + tail, lambda j: (j,) + (0,) * len(tail))
    return pl.pallas_call(
        functools.partial(_s5_kernel, bsz, nc_ctx, nc_lat),
        grid=(SSM_GROUPS // gpb,),
        in_specs=[slab(nc_lat), slab(nc_ctx), grp(2, S5_TILE, SSM_GROUP), grp(SSM_GROUP, S5_TILE), grp(S5_TILE, S5_TILE),
                  grp(S5_TILE, S5_TILE), grp(4, 2, 2 * SSM_STATE), grp(2, S5_SCAN_TILE, 2 * SSM_STATE)],
        out_specs=[slab(nc_lat), slab(nc_ctx)],
        out_shape=[jax.ShapeDtypeStruct((L, bsz, nc_lat, SSM_WIDTH), BF16),
                   jax.ShapeDtypeStruct((L, bsz, nc_ctx, SSM_WIDTH), BF16)],
        scratch_shapes=[pltpu.VMEM((n_rows, L * V7X_LANES), BF16)]
                       + [pltpu.VMEM((S5_GROUPS_PER_PASS, n_rows, S5_TILE), F32)] * 4
                       + [pltpu.VMEM((n_rows, gpb * S5_TILE), BF16),
                          pltpu.VMEM((gpb * S5_TILE, L * V7X_LANES), BF16)],
        compiler_params=_cparams(1, 56 * 1024 * 1024),
        name="s5",
    )(u_lat, u_ctx, kt, d_rows, b_in, c_out_t, apow, coef)


def _rope_tables(t_len):
    nf = NA_HEAD_DIM // 4
    inv = ROPE_BASE ** (-jnp.arange(nf, dtype=F32) / nf)
    pos = jnp.arange(t_len, dtype=jnp.int32)
    ang_r = (pos // GRID_W).astype(F32)[:, None] * inv
    ang_c = (pos % GRID_W).astype(F32)[:, None] * inv
    cos = jnp.concatenate([jnp.cos(ang_r)] * 2 + [jnp.cos(ang_c)] * 2, axis=-1)
    sin = jnp.concatenate([-jnp.sin(ang_r), jnp.sin(ang_r), -jnp.sin(ang_c), jnp.sin(ang_c)], axis=-1)
    reps = V7X_LANES // NA_HEAD_DIM
    return jnp.tile(cos, (1, reps)), jnp.tile(sin, (1, reps))


def _bias_table(rpb):
    kc = min(WIN_C, GRID_W)
    w = np.arange(GRID_W)
    c0 = np.clip(w - kc // 2, 0, GRID_W - kc)
    kcol = np.arange(GRID_W)
    valid = (kcol[None, :] >= c0[:, None]) & (kcol[None, :] < c0[:, None] + kc)
    col_off = np.clip(kcol[None, :] - w[:, None] + (WIN_C - 1), 0, 2 * WIN_C - 2)
    t = jnp.where(jnp.asarray(valid)[None, None], rpb.astype(F32)[:, :, col_off], MASK_VALUE)
    t = jnp.concatenate([t[:, :-1], t[:, 1:]], axis=-1)
    n_heads = V7X_LANES // NA_HEAD_DIM
    t = t.reshape(HEAD_PAIRS, n_heads, 2 * WIN_R - 2, GRID_W, 2 * GRID_W).transpose(0, 2, 1, 3, 4)
    return t.reshape(HEAD_PAIRS, 2 * WIN_R - 2, n_heads * GRID_W, 2 * GRID_W)


def _rope(x, cos, sin_signed):
    nf = NA_HEAD_DIM // 4
    lane = lax.broadcasted_iota(jnp.int32, x.shape, 1)
    first = (lane % (2 * nf)) < nf
    swapped = jnp.where(first, pltpu.roll(x, V7X_LANES - nf, 1), pltpu.roll(x, nf, 1))
    return x * cos + swapped * sin_signed


def _na_kernel(rows, kr, q_ref, k_ref, v_ref, kc_ref, vc_ref, cos_ref, sin_ref, bias_ref, o_ref,
               krope_scr, vb_scr, kcb_scr, vcb_scr, s_scr, p_scr):
    step = pl.program_id(2)
    dh = NA_HEAD_DIM
    t_len = k_ref.shape[0]
    rows_per_step = q_ref.shape[0] // GRID_W
    n_heads = V7X_LANES // dh

    @pl.when(step == 0)
    def _():
        chunk = 512
        for s in range(t_len // chunk):
            sl = pl.ds(s * chunk, chunk)
            krope_scr[sl, :] = _rope(k_ref[sl, :], cos_ref[sl, :], sin_ref[sl, :]).astype(BF16)
            vb_scr[sl, :] = v_ref[sl, :].astype(BF16)
        kcb_scr[...] = kc_ref[...].astype(BF16)
        vcb_scr[...] = vc_ref[...].astype(BF16)

    lane = lax.broadcasted_iota(jnp.int32, (GRID_W, V7X_LANES), 1)
    n_loc = kr * GRID_W
    wins = []
    for i in range(rows_per_step):
        r = step * rows_per_step + i
        r0 = jnp.clip(r - kr // 2, 0, rows - kr)
        ro0 = r0 - r + (WIN_R - 1)
        win = pl.ds(pl.multiple_of(r0 * GRID_W, GRID_W), n_loc)
        wins.append(win)
        qrow = pl.ds(pl.multiple_of(r * GRID_W, GRID_W), GRID_W)
        q = _rope(q_ref[pl.ds(i * GRID_W, GRID_W), :], cos_ref[qrow, :], sin_ref[qrow, :]) * (dh ** -0.5)
        q2 = jnp.concatenate([jnp.where((lane // dh) == h, q, 0.0) for h in range(n_heads)], axis=0).astype(BF16)
        s_loc = _dot_nt(q2, krope_scr[win, :])
        for j in range(kr // 2):
            cols = slice(j * V7X_LANES, (j + 1) * V7X_LANES)
            s_scr[i, :, cols] = s_loc[:, cols] + bias_ref[ro0 + 2 * j]
        s_scr[i, :, n_loc:] = _dot_nt(q2, kcb_scr[...])
    for i in range(rows_per_step):
        s = s_scr[i]
        m = jnp.max(s, axis=-1, keepdims=True)
        p = jnp.exp(s - m)
        denom = jnp.sum(p, axis=-1, keepdims=True)
        p_scr[i] = p.astype(BF16)
        o = (jnp.dot(p_scr[i, :, :n_loc], vb_scr[wins[i], :], preferred_element_type=F32)
             + jnp.dot(p_scr[i, :, n_loc:], vcb_scr[...], preferred_element_type=F32)) / denom
        out = o[:GRID_W]
        for h in range(1, n_heads):
            out = jnp.where((lane // dh) == h, o[h * GRID_W:(h + 1) * GRID_W], out)
        o_ref[pl.ds(i * GRID_W, GRID_W), :] = out.astype(o_ref.dtype)


def _neighbourhood_attention(proj, projc, rope_tabs, bias_tab, rows_per_step=4):
    bsz, t_len, _ = proj.shape
    c_len = projc.shape[1]
    rows = t_len // GRID_W
    kr = min(WIN_R, rows)
    assert kr % 2 == 0 and rows % rows_per_step == 0
    qb, kb, vb = 0, NA_WIDTH // V7X_LANES, 2 * NA_WIDTH // V7X_LANES
    cos, sin = rope_tabs
    tq = rows_per_step * GRID_W
    seq_spec = lambda off: pl.BlockSpec((None, t_len, V7X_LANES), lambda b, hp, r: (b, 0, off + hp))
    ctx_spec = lambda off: pl.BlockSpec((None, c_len, V7X_LANES), lambda b, hp, r: (b, 0, off + hp))
    tab_spec = pl.BlockSpec((t_len, V7X_LANES), lambda b, hp, r: (0, 0))
    bias_spec = pl.BlockSpec((None,) + bias_tab.shape[1:], lambda b, hp, r: (hp, 0, 0, 0))
    return pl.pallas_call(
        functools.partial(_na_kernel, rows, kr),
        grid=(bsz, HEAD_PAIRS, rows // rows_per_step),
        in_specs=[pl.BlockSpec((None, tq, V7X_LANES), lambda b, hp, r: (b, r, qb + hp)),
                  seq_spec(kb), seq_spec(vb), ctx_spec(kb), ctx_spec(vb), tab_spec, tab_spec, bias_spec],
        out_specs=pl.BlockSpec((None, tq, V7X_LANES), lambda b, hp, r: (b, r, hp)),
        out_shape=jax.ShapeDtypeStruct((bsz, t_len, NA_WIDTH), BF16),
        scratch_shapes=[pltpu.VMEM((t_len, V7X_LANES), BF16), pltpu.VMEM((t_len, V7X_LANES), BF16),
                        pltpu.VMEM((c_len, V7X_LANES), BF16), pltpu.VMEM((c_len, V7X_LANES), BF16),
                        pltpu.VMEM((rows_per_step, 2 * GRID_W, kr * GRID_W + c_len), F32),
                        pltpu.VMEM((rows_per_step, 2 * GRID_W, kr * GRID_W + c_len), BF16)],
        compiler_params=_cparams(3, 32 * 1024 * 1024),
        name="neighbourhood_attention",
    )(proj, proj, proj, projc, projc, cos, sin, bias_tab)


def _ctx_attn_kernel(q_ref, k_ref, v_ref, o_ref):
    dh = NA_HEAD_DIM
    q = (q_ref[...] * (dh ** -0.5)).astype(BF16)
    k = k_ref[...].astype(BF16)
    v = v_ref[...].astype(BF16)
    for h in range(V7X_LANES // dh):
        hs = slice(h * dh, (h + 1) * dh)
        s = _dot_nt(q[:, hs], k[:, hs])
        p = jnp.exp(s - jnp.max(s, axis=-1, keepdims=True))
        o = jnp.dot(p.astype(BF16), v[:, hs], preferred_element_type=F32)
        o_ref[:, hs] = (o / jnp.sum(p, axis=-1, keepdims=True)).astype(o_ref.dtype)


def _context_attention(projc):
    bsz, c_len, _ = projc.shape
    qb, kb, vb = 0, NA_WIDTH // V7X_LANES, 2 * NA_WIDTH // V7X_LANES
    spec = lambda off: pl.BlockSpec((None, c_len, V7X_LANES), lambda b, hp: (b, 0, off + hp))
    return pl.pallas_call(
        _ctx_attn_kernel,
        grid=(bsz, HEAD_PAIRS),
        in_specs=[spec(qb), spec(kb), spec(vb)],
        out_specs=spec(0),
        out_shape=jax.ShapeDtypeStruct((bsz, c_len, NA_WIDTH), BF16),
        compiler_params=_cparams(2, 32 * 1024 * 1024),
        name="context_attention",
    )(projc, projc, projc)


def _merge_kernel(a_ref, att_ref, gs_ref, gn_ref, wv_ref, wg_ref, wn_ref, o_ref):
    a = a_ref[...]
    val = jnp.dot(a, wv_ref[...], preferred_element_type=F32)
    glu = jnp.dot(a, wg_ref[...], preferred_element_type=F32)
    br_n = jnp.dot(att_ref[...], wn_ref[...], preferred_element_type=F32)
    z = _sigmoid(gs_ref[...]) * (val * _sigmoid(glu)) + _sigmoid(gn_ref[...]) * br_n
    o_ref[...] = z.astype(o_ref.dtype)


def _merge(a, att, proj, layer, w_val, w_glu, w_na, tm=512, tn=512):
    m = a.shape[0]
    gs_off = 3 * NA_WIDTH // tn
    gn_off = gs_off + D_MODEL // tn
    act = pl.BlockSpec((tm, SSM_WIDTH), lambda j, i: (i, 0))
    wsp = pl.BlockSpec((None, SSM_WIDTH, tn), lambda j, i: (layer, 0, j))
    return pl.pallas_call(
        _merge_kernel,
        grid=(D_MODEL // tn, m // tm),
        in_specs=[act, act,
                  pl.BlockSpec((tm, tn), lambda j, i: (i, gs_off + j)),
                  pl.BlockSpec((tm, tn), lambda j, i: (i, gn_off + j)),
                  wsp, wsp, wsp],
        out_specs=pl.BlockSpec((tm, tn), lambda j, i: (i, j)),
        out_shape=jax.ShapeDtypeStruct((m, D_MODEL), BF16),
        compiler_params=_cparams(2, 40 * 1024 * 1024),
        name="merge",
    )(a, att, proj, proj, w_val, w_glu, w_na)


def _outproj_kernel(z_ref, w_ref, x_ref, gt_ref, gpost_ref, gpre_ref, sh_ref, sc_ref, x_out, h_out):
    out = jnp.dot(z_ref[...], w_ref[...], preferred_element_type=F32)
    x1 = x_ref[...] + gt_ref[...] * _rms(out, gpost_ref[...])
    x_out[...] = x1
    h_out[...] = (_rms(x1, gpre_ref[...]) * (1.0 + sc_ref[...]) + sh_ref[...]).astype(h_out.dtype)


def _outproj(z, w_out, layer, x, gt, g_post, g_pre, sh, sc, rows_per_batch, tm=256):
    m = z.shape[0]
    row = pl.BlockSpec((tm, D_MODEL), lambda i: (i, 0))
    vec = pl.BlockSpec((1, D_MODEL), lambda i: (0, 0))
    bvec = pl.BlockSpec((None, 1, D_MODEL), lambda i: (i * tm // rows_per_batch, 0, 0))
    return pl.pallas_call(
        _outproj_kernel,
        grid=(m // tm,),
        in_specs=[row, pl.BlockSpec((None, D_MODEL, D_MODEL), lambda i: (layer, 0, 0)), row, bvec, vec, vec, bvec,
                  bvec],
        out_specs=[row, row],
        out_shape=[jax.ShapeDtypeStruct((m, D_MODEL), F32), jax.ShapeDtypeStruct((m, D_MODEL), BF16)],
        compiler_params=_cparams(1, 48 * 1024 * 1024),
        name="out_proj",
    )(z, w_out, x, gt, g_post.reshape(1, D_MODEL), g_pre.reshape(1, D_MODEL), sh, sc)


def _mlp_kernel(emit_next, h_ref, w1_ref, w2_ref, x_ref, gt_ref, gpost_ref, *rest):
    if emit_next:
        gpre_ref, sh_ref, sc_ref, x_out, h_out, acc = rest
    else:
        x_out, acc = rest
    f = pl.program_id(1)

    @pl.when(f == 0)
    def _():
        acc[...] = jnp.zeros_like(acc)

    a = jnp.maximum(jnp.dot(h_ref[...], w1_ref[...], preferred_element_type=F32), 0.0)
    acc[...] += jnp.dot((a * a).astype(BF16), w2_ref[...], preferred_element_type=F32)

    @pl.when(f == pl.num_programs(1) - 1)
    def _():
        x2 = x_ref[...] + gt_ref[...] * _rms(acc[...], gpost_ref[...])
        x_out[...] = x2
        if emit_next:
            h_out[...] = (_rms(x2, gpre_ref[...]) * (1.0 + sc_ref[...]) + sh_ref[...]).astype(h_out.dtype)


def _mlp(h2, w1, w2, layer, x, gt, g_post, nxt, rows_per_batch, tm=512, tf=512):
    m = h2.shape[0]
    emit_next = nxt is not None
    row = pl.BlockSpec((tm, D_MODEL), lambda i, f: (i, 0))
    vec = pl.BlockSpec((1, D_MODEL), lambda i, f: (0, 0))
    bvec = pl.BlockSpec((None, 1, D_MODEL), lambda i, f: (i * tm // rows_per_batch, 0, 0))
    in_specs = [row, pl.BlockSpec((None, D_MODEL, tf), lambda i, f: (layer, 0, f)),
                pl.BlockSpec((None, tf, D_MODEL), lambda i, f: (layer, f, 0)), row, bvec, vec]
    args = [h2, w1, w2, x, gt, g_post.reshape(1, D_MODEL)]
    out_specs = [row]
    out_shape = [jax.ShapeDtypeStruct((m, D_MODEL), F32)]
    if emit_next:
        g_pre, sh, sc = nxt
        in_specs += [vec, bvec, bvec]
        args += [g_pre.reshape(1, D_MODEL), sh, sc]
        out_specs.append(row)
        out_shape.append(jax.ShapeDtypeStruct((m, D_MODEL), BF16))
    res = pl.pallas_call(
        functools.partial(_mlp_kernel, emit_next),
        grid=(m // tm, D_FF // tf),
        in_specs=in_specs,
        out_specs=out_specs,
        out_shape=out_shape,
        scratch_shapes=[pltpu.VMEM((tm, D_MODEL), F32)],
        compiler_params=_cparams(2, 48 * 1024 * 1024),
        name="mlp",
    )(*args)
    return (res[0], res[1]) if emit_next else (res[0], None)


def kernel(x, c, ctx, c_ctx, w_mod, b_mod, g_pre_mix, g_post_mix, g_pre_mlp, g_post_mlp, w_in,
           ssm_lam_re, ssm_lam_im, ssm_log_dt, ssm_b_re, ssm_b_im, ssm_c_re, ssm_c_im, ssm_d,
           w_ssm_val, w_ssm_glu, na_rpb, w_na_proj, w_out, w_fc1, w_fc2):
    bsz, t_len, _ = x.shape
    c_len = ctx.shape[1]
    m_lat, m_ctx = bsz * t_len, bsz * c_len
    assert bsz < V7X_SUBLANES and t_len % (GRID_W * 8) == 0 and c_len % S5_CHUNK == 0

    c_rows = jnp.concatenate([c, c_ctx[None, :], jnp.zeros((V7X_SUBLANES - bsz - 1, D_MODEL), F32)], axis=0)
    mod = _modulation(c_rows, w_mod, b_mod)

    def mod_vectors(l):
        lat = [v[:, None, :] for v in jnp.split(mod[l, :bsz], N_MOD, axis=-1)]
        cvec = [jnp.broadcast_to(v[None, :, :], (bsz, 1, D_MODEL))
                for v in jnp.split(mod[l, bsz:bsz + 1], N_MOD, axis=-1)]
        return lat, cvec

    mods = [mod_vectors(l) for l in range(DEPTH)]
    rope_tabs = _rope_tables(t_len)

    xl = x.reshape(m_lat, D_MODEL)
    xc = ctx.reshape(m_ctx, D_MODEL)
    h = _norm_mod(xl, g_pre_mix[0], mods[0][0][0], mods[0][0][1], t_len)
    hc = _norm_mod(xc, g_pre_mix[0], mods[0][1][0], mods[0][1][1], c_len)

    w_in_b, w_val_b, w_glu_b = w_in.astype(BF16), w_ssm_val.astype(BF16), w_ssm_glu.astype(BF16)
    w_na_b, w_out_b = w_na_proj.astype(BF16), w_out.astype(BF16)
    w1_b, w2_b = w_fc1.astype(BF16), w_fc2.astype(BF16)

    def to_pos_major(v):
        m, w = v.shape
        return v.reshape(m // S5_CHUNK, S5_CHUNK, w).transpose(1, 0, 2).reshape(m, w)

    def from_pos_major(v):
        return v.transpose(1, 2, 0, 3).reshape(-1, v.shape[-1])

    for l in range(DEPTH):
        with_ctx_out = l < DEPTH - 1
        (sh1, sc1, gt1, sh2, sc2, gt2), (csh1, csc1, cgt1, csh2, csc2, cgt2) = mods[l]

        u_tiles = SSM_WIDTH // 1024
        proj = _matmul(h, w_in_b, l, 1024, 1024, F32, first_tile=u_tiles)
        projc = _matmul(hc, w_in_b, l, 1024, 1024, F32, first_tile=u_tiles)
        u_lat = _matmul(to_pos_major(h), w_in_b, l, 1024, 1024, BF16, n_tiles=u_tiles)
        u_ctx = _matmul(to_pos_major(hc), w_in_b, l, 1024, 1024, BF16, n_tiles=u_tiles)
        proj3 = proj.reshape(bsz, t_len, REST_WIDTH)
        projc3 = projc.reshape(bsz, c_len, REST_WIDTH)

        mats = _s5_matrices(ssm_lam_re[l], ssm_lam_im[l], ssm_log_dt[l], ssm_b_re[l], ssm_b_im[l],
                            ssm_c_re[l], ssm_c_im[l], ssm_d[l])
        a_lat, a_ctx = _s5_branch(u_lat.reshape(S5_CHUNK, bsz, t_len // S5_CHUNK, SSM_WIDTH),
                                  u_ctx.reshape(S5_CHUNK, bsz, c_len // S5_CHUNK, SSM_WIDTH), mats, bsz, t_len, c_len)

        att = _neighbourhood_attention(proj3, projc3, rope_tabs, _bias_table(na_rpb[l]))
        z = _merge(from_pos_major(a_lat), att.reshape(m_lat, NA_WIDTH), proj, l, w_val_b, w_glu_b, w_na_b)
        xl, h2 = _outproj(z, w_out_b, l, xl, gt1, g_post_mix[l], g_pre_mlp[l], sh2, sc2, t_len)
        nxt = (g_pre_mix[l + 1], mods[l + 1][0][0], mods[l + 1][0][1]) if with_ctx_out else None
        xl, h = _mlp(h2, w1_b, w2_b, l, xl, gt2, g_post_mlp[l], nxt, t_len)

        if with_ctx_out:
            attc = _context_attention(projc3)
            zc = _merge(from_pos_major(a_ctx), attc.reshape(m_ctx, NA_WIDTH), projc, l, w_val_b, w_glu_b, w_na_b)
            xc, h2c = _outproj(zc, w_out_b, l, xc, cgt1, g_post_mix[l], g_pre_mlp[l], csh2, csc2, c_len)
            nxtc = (g_pre_mix[l + 1], mods[l + 1][1][0], mods[l + 1][1][1])
            xc, hc = _mlp(h2c, w1_b, w2_b, l, xc, cgt2, g_post_mlp[l], nxtc, c_len)

    return xl.reshape(bsz, t_len, D_MODEL)
```

```python
import functools
import math

import numpy as np
import jax
import jax.numpy as jnp
from jax import lax
from jax.experimental import pallas as pl
from jax.experimental.pallas import tpu as pltpu

D_MODEL = 2048
DEPTH = 2
GRID_W = 64
N_MOD = 6
SSM_WIDTH = D_MODEL // 2
SSM_GROUP = 16
SSM_GROUPS = SSM_WIDTH // SSM_GROUP
SSM_STATE = 64
NA_HEADS = 16
NA_HEAD_DIM = 64
NA_WIDTH = NA_HEADS * NA_HEAD_DIM
WIN_R = 8
WIN_C = 16
ROPE_BASE = 10000.0
D_FF = 4 * D_MODEL
NORM_EPS = 1e-6
PROJ_WIDTH = SSM_WIDTH + 3 * NA_WIDTH + 2 * D_MODEL
REST_WIDTH = PROJ_WIDTH - SSM_WIDTH

V7X_LANES = 128
V7X_SUBLANES = 8
V7X_VMEM_BYTES = 64 * 1024 * 1024
HEAD_PAIRS = NA_HEADS * NA_HEAD_DIM // V7X_LANES
S5_CHUNK = 16
S5_TILE = S5_CHUNK * SSM_GROUP
S5_SCAN_TILE = V7X_SUBLANES
S5_GROUPS_PER_BLOCK = V7X_LANES // SSM_GROUP
S5_GROUPS_PER_PASS = 4
MASK_VALUE = -1e30

F32 = jnp.float32
BF16 = jnp.bfloat16


def _cparams(n_axes, vmem_bytes):
    assert vmem_bytes <= V7X_VMEM_BYTES
    return pltpu.CompilerParams(dimension_semantics=("arbitrary",) * n_axes, vmem_limit_bytes=vmem_bytes)


def _sigmoid(x):
    return 1.0 / (1.0 + jnp.exp(-x))


def _gelu_tanh(x):
    return 0.5 * x * (1.0 + jnp.tanh(math.sqrt(2.0 / math.pi) * (x + 0.044715 * (x * x * x))))


def _rms(x, g):
    return x * lax.rsqrt(jnp.mean(x * x, axis=-1, keepdims=True) + NORM_EPS) * g


def _mod_kernel(c_ref, w_ref, b_ref, o_ref):
    c = c_ref[...]
    s = (c * _sigmoid(c)).astype(BF16)
    o_ref[...] = jnp.dot(s, w_ref[...].astype(BF16), preferred_element_type=F32) + b_ref[...]


def _modulation(c_rows, w_mod, b_mod, tn=1024):
    n = w_mod.shape[-1]
    return pl.pallas_call(
        _mod_kernel,
        grid=(DEPTH, n // tn),
        in_specs=[pl.BlockSpec((V7X_SUBLANES, D_MODEL), lambda l, j: (0, 0)),
                  pl.BlockSpec((None, D_MODEL, tn), lambda l, j: (l, 0, j)),
                  pl.BlockSpec((None, 1, tn), lambda l, j: (l, 0, j))],
        out_specs=pl.BlockSpec((None, V7X_SUBLANES, tn), lambda l, j: (l, 0, j)),
        out_shape=jax.ShapeDtypeStruct((DEPTH, V7X_SUBLANES, n), F32),
        compiler_params=_cparams(2, 40 * 1024 * 1024),
        name="modulation",
    )(c_rows, w_mod, b_mod.reshape(DEPTH, 1, n))


def _norm_mod_kernel(x_ref, g_ref, sh_ref, sc_ref, o_ref):
    y = _rms(x_ref[...], g_ref[...])
    o_ref[...] = (y * (1.0 + sc_ref[...]) + sh_ref[...]).astype(o_ref.dtype)


def _norm_mod(x, g, sh, sc, rows_per_batch, tm=512):
    m = x.shape[0]
    bidx = lambda i: (i * tm // rows_per_batch, 0, 0)
    return pl.pallas_call(
        _norm_mod_kernel,
        grid=(m // tm,),
        in_specs=[pl.BlockSpec((tm, D_MODEL), lambda i: (i, 0)),
                  pl.BlockSpec((1, D_MODEL), lambda i: (0, 0)),
                  pl.BlockSpec((None, 1, D_MODEL), bidx),
                  pl.BlockSpec((None, 1, D_MODEL), bidx)],
        out_specs=pl.BlockSpec((tm, D_MODEL), lambda i: (i, 0)),
        out_shape=jax.ShapeDtypeStruct((m, D_MODEL), BF16),
        compiler_params=_cparams(1, 32 * 1024 * 1024),
        name="norm_mod",
    )(x, g.reshape(1, D_MODEL), sh, sc)


def _matmul_kernel(a_ref, w_ref, o_ref):
    o_ref[...] = jnp.dot(a_ref[...], w_ref[...], preferred_element_type=F32).astype(o_ref.dtype)


def _matmul(a, w, layer, tm, tn, out_dtype, first_tile=0, n_tiles=None):
    m, k = a.shape
    n_tiles = w.shape[2] // tn - first_tile if n_tiles is None else n_tiles
    return pl.pallas_call(
        _matmul_kernel,
        grid=(n_tiles, m // tm),
        in_specs=[pl.BlockSpec((tm, k), lambda j, i: (i, 0)),
                  pl.BlockSpec((None, k, tn), lambda j, i: (layer, 0, first_tile + j))],
        out_specs=pl.BlockSpec((tm, tn), lambda j, i: (i, j)),
        out_shape=jax.ShapeDtypeStruct((m, n_tiles * tn), out_dtype),
        compiler_params=_cparams(2, 40 * 1024 * 1024),
        name="in_proj",
    )(a, w)


def _s5_matrices(lam_re, lam_im, log_dt, b_re, b_im, c_re, c_im, d_skip):
    L = S5_CHUNK
    def cmul(a, b):
        return a[0] * b[0] - a[1] * b[1], a[0] * b[1] + a[1] * b[0]

    lam = (lam_re.astype(F32), lam_im.astype(F32))
    dt = jnp.exp(log_dt.astype(F32))[..., None]
    tau = jnp.arange(L + 1, dtype=F32)[None, None, :, None]
    mag = jnp.exp((lam[0] * dt)[:, :, None, :] * tau)
    ang = (lam[1] * dt)[:, :, None, :] * tau
    pw = (mag * jnp.cos(ang), mag * jnp.sin(ang))
    num = (pw[0][:, :, 1] - 1.0, pw[1][:, :, 1])
    den = lam[0] * lam[0] + lam[1] * lam[1]
    coef = ((num[0] * lam[0] + num[1] * lam[1]) / den, (num[1] * lam[0] - num[0] * lam[1]) / den)
    b_bar = cmul((coef[0][..., None], coef[1][..., None]), (b_re.astype(F32), b_im.astype(F32)))
    b_bar_t = (jnp.swapaxes(b_bar[0], -1, -2), jnp.swapaxes(b_bar[1], -1, -2))
    cmat = (c_re.astype(F32), c_im.astype(F32))

    def pw_at(d, sl):
        return pw[0][d][:, sl, None, :], pw[1][d][:, sl, None, :]

    c_lag = cmul((cmat[0][:, :, None], cmat[1][:, :, None]),
                 (pw[0][:, :, :L, None, :], pw[1][:, :, :L, None, :]))
    hi = lax.Precision.HIGHEST
    kern = (jnp.einsum('dglcp,dgpi->dglci', c_lag[0], b_bar[0], precision=hi)
            - jnp.einsum('dglcp,dgpi->dglci', c_lag[1], b_bar[1], precision=hi))
    kt = jnp.stack([kern[0].reshape(SSM_GROUPS, S5_TILE, SSM_GROUP),
                    kern[1][:, ::-1].reshape(SSM_GROUPS, S5_TILE, SSM_GROUP)], axis=1)
    d_rows = jnp.broadcast_to(d_skip.astype(F32).reshape(SSM_GROUPS, SSM_GROUP, 1), (SSM_GROUPS, SSM_GROUP, S5_TILE))

    bf = cmul(pw_at(0, slice(L - 1, None, -1)), (b_bar_t[0][0][:, None], b_bar_t[1][0][:, None]))
    br = cmul(pw_at(1, slice(0, L)), (b_bar_t[0][1][:, None], b_bar_t[1][1][:, None]))
    b_in = jnp.concatenate([bf[0], br[0], bf[1], br[1]], axis=-1)
    b_in = b_in.reshape(SSM_GROUPS, S5_TILE, 4 * SSM_STATE)

    zf = cmul((cmat[0][0][:, None], cmat[1][0][:, None]), pw_at(0, slice(1, L + 1)))
    zr = cmul((cmat[0][1][:, None], cmat[1][1][:, None]), pw_at(1, slice(L, 0, -1)))
    c_out_t = jnp.concatenate([zf[0], zr[0], -zf[1], -zr[1]], axis=-1).reshape(SSM_GROUPS, S5_TILE, 4 * SSM_STATE)

    return (kt, d_rows, b_in.astype(BF16), c_out_t.astype(BF16)) + _s5_tables(lam_re, lam_im, log_dt)


def _s5_tables(lam_re, lam_im, log_dt):
    n = S5_SCAN_TILE
    dt = jnp.exp(log_dt.astype(F32))[..., None]
    steps = jnp.arange(n + 1, dtype=F32)[None, None, :, None] * S5_CHUNK
    mag = jnp.exp((lam_re.astype(F32) * dt)[:, :, None, :] * steps)
    ang = (lam_im.astype(F32) * dt)[:, :, None, :] * steps
    pre, pim = mag * jnp.cos(ang), mag * jnp.sin(ang)
    both = lambda f, r: jnp.concatenate([f, r], axis=-1)
    sel = np.array([1, 2, 4, 8])
    apow = jnp.stack([both(pre[0][:, sel], pre[1][:, sel]), both(pim[0][:, sel], pim[1][:, sel])], axis=2)
    coef = jnp.stack([both(pre[0][:, :n], pre[1][:, n - 1::-1][:, :n]),
                      both(pim[0][:, :n], pim[1][:, n - 1::-1][:, :n])], axis=1)
    return apow, coef


def _dot_nt(a, b):
    return lax.dot_general(a, b, (((1,), (1,)), ((), ())), preferred_element_type=F32)


def _shift_lanes(x, s):
    w = V7X_LANES
    lo, hi = x[:, :w], x[:, w:]
    lane = lax.broadcasted_iota(jnp.int32, lo.shape, 1)
    if s == 0:
        return x
    if s > 0:
        q, r = divmod(s, w)
        if r == 0:
            out = (jnp.zeros_like(lo), lo)
        else:
            rl, rh = pltpu.roll(lo, r, 1), pltpu.roll(hi, r, 1)
            if q == 0:
                out = (jnp.where(lane >= r, rl, 0.0), jnp.where(lane >= r, rh, rl))
            else:
                out = (jnp.zeros_like(lo), jnp.where(lane >= r, rl, 0.0))
    else:
        q, r = divmod(-s, w)
        if r == 0:
            out = (hi, jnp.zeros_like(hi))
        else:
            rl, rh = pltpu.roll(lo, w - r, 1), pltpu.roll(hi, w - r, 1)
            if q == 0:
                out = (jnp.where(lane < w - r, rl, rh), jnp.where(lane < w - r, rh, 0.0))
            else:
                out = (jnp.where(lane < w - r, rh, 0.0), jnp.zeros_like(hi))
    return jnp.concatenate(out, axis=1)


def _s5_chunk_rows(kt, d_rows, bin_ref, g):
    L = S5_CHUNK
    eye = (lax.broadcasted_iota(jnp.int32, (SSM_GROUP, SSM_GROUP), 0)
           == lax.broadcasted_iota(jnp.int32, (SSM_GROUP, SSM_GROUP), 1)).astype(F32)
    to_rows = lambda m: lax.dot_general(eye, m, (((1,), (1,)), ((), ())), precision=lax.Precision.HIGHEST,
                                        preferred_element_type=F32)
    kl_f, kl_r = to_rows(kt[0]), to_rows(kt[1])
    lane = lax.broadcasted_iota(jnp.int32, (SSM_GROUP, S5_TILE), 1)
    row = lax.broadcasted_iota(jnp.int32, (SSM_GROUP, S5_TILE), 0)
    blocks = []
    for t in range(L):
        m_rows = (_shift_lanes(kl_f, t * SSM_GROUP) + _shift_lanes(kl_r, -(L - 1 - t) * SSM_GROUP)
                  + jnp.where(lane == t * SSM_GROUP + row, d_rows, 0.0))
        blocks.append(jnp.concatenate([m_rows.astype(BF16), bin_ref[g, t * SSM_GROUP:(t + 1) * SSM_GROUP, :]], axis=1))
    return blocks


def _s5_kernel(bsz, nc_ctx, nc_lat, *refs):
    L = S5_CHUNK
    xl, xc, kt_ref, d_ref, bin_ref, cout_ref, apow_ref, coef_ref, ol, oc = refs[:10]
    xcat, ucat, y_scr, s_scr, ea_scr, eb_scr, acat, esel = refs[10:]
    nb = nc_ctx + nc_lat
    n_rows = bsz * nb
    lanes = V7X_LANES
    half = 2 * SSM_STATE
    tile = S5_SCAN_TILE
    gpb, gpp = S5_GROUPS_PER_BLOCK, S5_GROUPS_PER_PASS

    @pl.when(pl.program_id(0) == 0)
    def _():
        blk = 256
        for rb in range(esel.shape[0] // blk):
            r = lax.broadcasted_iota(jnp.int32, (blk, esel.shape[1]), 0) + rb * blk
            c = lax.broadcasted_iota(jnp.int32, (blk, esel.shape[1]), 1)
            dest = ((r % S5_TILE) // SSM_GROUP) * lanes + (r // S5_TILE) * SSM_GROUP + r % SSM_GROUP
            esel[rb * blk:(rb + 1) * blk, :] = jnp.where(dest == c, 1.0, 0.0).astype(BF16)

    for t in range(L):
        cols = slice(t * lanes, (t + 1) * lanes)
        for b in range(bsz):
            xcat[b * nb:b * nb + nc_ctx, cols] = xc[t, b]
            xcat[b * nb + nc_ctx:(b + 1) * nb, cols] = xl[t, b]

    ucat[...] = _dot_nt(xcat[...], esel[...]).astype(BF16)

    lane = lax.broadcasted_iota(jnp.int32, (tile, half), 1)
    row = lax.broadcasted_iota(jnp.int32, (tile, half), 0)
    is_fwd = lane < SSM_STATE
    shifts = (1, 2, 4)
    keep_dn = {s: is_fwd & (row >= s) for s in shifts}
    keep_up = {s: jnp.logical_not(is_fwd) & (row < tile - s) for s in shifts}

    def shift(d, s):
        return jnp.where(keep_dn[s], pltpu.roll(d, s, 0), jnp.where(keep_up[s], pltpu.roll(d, tile - s, 0), 0.0))

    def cfma(a_re, a_im, x_re, x_im, y_re, y_im):
        return a_re * x_re - a_im * x_im + y_re, a_re * x_im + a_im * x_re + y_im

    for p in range(gpb // gpp):
        for q in range(gpp):
            g = p * gpp + q
            w_g = jnp.concatenate(_s5_chunk_rows(kt_ref[g], d_ref[g], bin_ref, g), axis=0)
            ys = jnp.dot(ucat[:, g * S5_TILE:(g + 1) * S5_TILE], w_g, preferred_element_type=F32)
            y_scr[q] = ys[:, :S5_TILE]
            s_scr[q] = ys[:, S5_TILE:]

        bc = lambda v: jnp.broadcast_to(v, (tile, half))
        a_pow = [[(bc(apow_ref[p * gpp + q, k, 0:1, :]), bc(apow_ref[p * gpp + q, k, 1:2, :])) for k in range(4)]
                 for q in range(gpp)]
        coefs = [(coef_ref[p * gpp + q, 0], coef_ref[p * gpp + q, 1]) for q in range(gpp)]

        for b in range(bsz):
            def make_body(rev_base, base=b * nb):
                def body(i, carry):
                    rf = pl.multiple_of(base + i * tile, tile)
                    rr = pl.multiple_of(base + (rev_base - i) * tile, tile)
                    out = []
                    for q in range(gpp):
                        e_re, e_im = carry[2 * q], carry[2 * q + 1]
                        sf = s_scr[q, pl.ds(rf, tile), :]
                        sr = s_scr[q, pl.ds(rr, tile), :]
                        d_re = jnp.where(is_fwd, sf[:, :half], sr[:, :half])
                        d_im = jnp.where(is_fwd, sf[:, half:], sr[:, half:])
                        for k, s in enumerate(shifts):
                            d_re, d_im = cfma(a_pow[q][k][0], a_pow[q][k][1], shift(d_re, s), shift(d_im, s), d_re, d_im)
                        en_re, en_im = cfma(coefs[q][0], coefs[q][1], e_re, e_im, shift(d_re, 1), shift(d_im, 1))
                        enter = jnp.concatenate([en_re, en_im], axis=-1)
                        ea_scr[q, pl.ds(rf, tile), :] = enter
                        eb_scr[q, pl.ds(rr, tile), :] = enter
                        last_re = jnp.where(is_fwd, bc(d_re[tile - 1:tile]), bc(d_re[0:1]))
                        last_im = jnp.where(is_fwd, bc(d_im[tile - 1:tile]), bc(d_im[0:1]))
                        out.extend(cfma(a_pow[q][3][0], a_pow[q][3][1], e_re, e_im, last_re, last_im))
                    return tuple(out)
                return body

            zero = jnp.zeros((tile, half), F32)
            nt_ctx, nt = nc_ctx // tile, nb // tile
            carry = lax.fori_loop(0, nt_ctx, make_body(nt_ctx - 1), tuple(zero for _ in range(2 * gpp)))
            lax.fori_loop(nt_ctx, nt, make_body(nt - 1 + nt_ctx), carry)

        lane_all = lax.broadcasted_iota(jnp.int32, (n_rows, 2 * half), 1)
        fwd_all = (lane_all % half) < SSM_STATE
        for q in range(gpp):
            g = p * gpp + q
            enter = jnp.where(fwd_all, ea_scr[q], eb_scr[q]).astype(BF16)
            y = y_scr[q] + _dot_nt(enter, cout_ref[g])
            acat[:, g * S5_TILE:(g + 1) * S5_TILE] = _gelu_tanh(y).astype(BF16)

    for t2 in range(L // 2):
        cols = slice(t2 * 2 * lanes, (t2 + 1) * 2 * lanes)
        a_nat = jnp.dot(acat[...], esel[:, cols], preferred_element_type=F32).astype(BF16)
        for dt_ in range(2):
            t = 2 * t2 + dt_
            for b in range(bsz):
                oc[t, b] = a_nat[b * nb:b * nb + nc_ctx, dt_ * lanes:(dt_ + 1) * lanes]
                ol[t, b] = a_nat[b * nb + nc_ctx:(b + 1) * nb, dt_ * lanes:(dt_ + 1) * lanes]


def _s5_branch(u_lat, u_ctx, mats, layer, bsz, t_len, c_len):
    kt, d_rows, b_in, c_out_t, apow, coef = mats
    L = S5_CHUNK
    nc_lat, nc_ctx = t_len // L, c_len // L
    assert nc_lat % S5_SCAN_TILE == 0 and nc_ctx % S5_SCAN_TILE == 0
    n_rows = bsz * (nc_lat + nc_ctx)
    gpb = S5_GROUPS_PER_BLOCK
    slab = lambda n: pl.BlockSpec((L, bsz, n, V7X_LANES), lambda j: (0, 0, 0, j))
    grp = lambda *tail: pl.BlockSpec((None, gpb) + tail, lambda j: (layer, j) + (0,) * len(tail))
    return pl.pallas_call(
        functools.partial(_s5_kernel, bsz, nc_ctx, nc_lat),
        grid=(SSM_GROUPS // gpb,),
        in_specs=[slab(nc_lat), slab(nc_ctx), grp(2, S5_TILE, SSM_GROUP), grp(SSM_GROUP, S5_TILE), grp(S5_TILE, S5_TILE),
                  grp(S5_TILE, S5_TILE), grp(4, 2, 2 * SSM_STATE), grp(2, S5_SCAN_TILE, 2 * SSM_STATE)],
        out_specs=[slab(nc_lat), slab(nc_ctx)],
        out_shape=[jax.ShapeDtypeStruct((L, bsz, nc_lat, SSM_WIDTH), BF16),
                   jax.ShapeDtypeStruct((L, bsz, nc_ctx, SSM_WIDTH), BF16)],
        scratch_shapes=[pltpu.VMEM((n_rows, L * V7X_LANES), BF16)] * 2
                       + [pltpu.VMEM((S5_GROUPS_PER_PASS, n_rows, S5_TILE), F32)] * 4
                       + [pltpu.VMEM((n_rows, gpb * S5_TILE), BF16),
                          pltpu.VMEM((gpb * S5_TILE, L * V7X_LANES), BF16)],
        compiler_params=_cparams(1, 56 * 1024 * 1024),
        name="s5",
    )(u_lat, u_ctx, kt, d_rows, b_in, c_out_t, apow, coef)


def _rope_tables(t_len):
    nf = NA_HEAD_DIM // 4
    inv = ROPE_BASE ** (-jnp.arange(nf, dtype=F32) / nf)
    pos = jnp.arange(t_len, dtype=jnp.int32)
    ang_r = (pos // GRID_W).astype(F32)[:, None] * inv
    ang_c = (pos % GRID_W).astype(F32)[:, None] * inv
    cos = jnp.concatenate([jnp.cos(ang_r)] * 2 + [jnp.cos(ang_c)] * 2, axis=-1)
    sin = jnp.concatenate([-jnp.sin(ang_r), jnp.sin(ang_r), -jnp.sin(ang_c), jnp.sin(ang_c)], axis=-1)
    reps = V7X_LANES // NA_HEAD_DIM
    return jnp.tile(cos, (1, reps)), jnp.tile(sin, (1, reps))


def _bias_table(rpb):
    kc = min(WIN_C, GRID_W)
    w = np.arange(GRID_W)
    c0 = np.clip(w - kc // 2, 0, GRID_W - kc)
    kcol = np.arange(GRID_W)
    valid = (kcol[None, :] >= c0[:, None]) & (kcol[None, :] < c0[:, None] + kc)
    col_off = np.clip(kcol[None, :] - w[:, None] + (WIN_C - 1), 0, 2 * WIN_C - 2)
    t = jnp.where(jnp.asarray(valid)[None, None], rpb.astype(F32)[:, :, col_off], MASK_VALUE)
    t = jnp.concatenate([t[:, :-1], t[:, 1:]], axis=-1)
    n_heads = V7X_LANES // NA_HEAD_DIM
    t = t.reshape(HEAD_PAIRS, n_heads, 2 * WIN_R - 2, GRID_W, 2 * GRID_W).transpose(0, 2, 1, 3, 4)
    return t.reshape(HEAD_PAIRS, 2 * WIN_R - 2, n_heads * GRID_W, 2 * GRID_W)


def _rope(x, cos, sin_signed):
    nf = NA_HEAD_DIM // 4
    lane = lax.broadcasted_iota(jnp.int32, x.shape, 1)
    first = (lane % (2 * nf)) < nf
    swapped = jnp.where(first, pltpu.roll(x, V7X_LANES - nf, 1), pltpu.roll(x, nf, 1))
    return x * cos + swapped * sin_signed


def _na_kernel(rows, kr, q_ref, k_ref, v_ref, kc_ref, vc_ref, cos_ref, sin_ref, bias_ref, o_ref,
               krope_scr, vb_scr, kcb_scr, vcb_scr, s_scr, p_scr):
    step = pl.program_id(2)
    dh = NA_HEAD_DIM
    t_len = k_ref.shape[0]
    rows_per_step = q_ref.shape[0] // GRID_W
    n_heads = V7X_LANES // dh

    @pl.when(step == 0)
    def _():
        chunk = 512
        for s in range(t_len // chunk):
            sl = pl.ds(s * chunk, chunk)
            krope_scr[sl, :] = _rope(k_ref[sl, :], cos_ref[sl, :], sin_ref[sl, :]).astype(BF16)
            vb_scr[sl, :] = v_ref[sl, :].astype(BF16)
        kcb_scr[...] = kc_ref[...].astype(BF16)
        vcb_scr[...] = vc_ref[...].astype(BF16)

    lane = lax.broadcasted_iota(jnp.int32, (GRID_W, V7X_LANES), 1)
    n_loc = kr * GRID_W
    wins = []
    for i in range(rows_per_step):
        r = step * rows_per_step + i
        r0 = jnp.clip(r - kr // 2, 0, rows - kr)
        ro0 = r0 - r + (WIN_R - 1)
        win = pl.ds(pl.multiple_of(r0 * GRID_W, GRID_W), n_loc)
        wins.append(win)
        qrow = pl.ds(pl.multiple_of(r * GRID_W, GRID_W), GRID_W)
        q = _rope(q_ref[pl.ds(i * GRID_W, GRID_W), :], cos_ref[qrow, :], sin_ref[qrow, :]) * (dh ** -0.5)
        q2 = jnp.concatenate([jnp.where((lane // dh) == h, q, 0.0) for h in range(n_heads)], axis=0).astype(BF16)
        s_loc = _dot_nt(q2, krope_scr[win, :])
        for j in range(kr // 2):
            cols = slice(j * V7X_LANES, (j + 1) * V7X_LANES)
            s_scr[i, :, cols] = s_loc[:, cols] + bias_ref[ro0 + 2 * j]
        s_scr[i, :, n_loc:] = _dot_nt(q2, kcb_scr[...])
    for i in range(rows_per_step):
        s = s_scr[i]
        m = jnp.max(s, axis=-1, keepdims=True)
        p = jnp.exp(s - m)
        denom = jnp.sum(p, axis=-1, keepdims=True)
        p_scr[i] = p.astype(BF16)
        o = (jnp.dot(p_scr[i, :, :n_loc], vb_scr[wins[i], :], preferred_element_type=F32)
             + jnp.dot(p_scr[i, :, n_loc:], vcb_scr[...], preferred_element_type=F32)) / denom
        out = o[:GRID_W]
        for h in range(1, n_heads):
            out = jnp.where((lane // dh) == h, o[h * GRID_W:(h + 1) * GRID_W], out)
        o_ref[pl.ds(i * GRID_W, GRID_W), :] = out.astype(o_ref.dtype)


def _neighbourhood_attention(proj, projc, rope_tabs, bias_tab, layer, rows_per_step=4):
    bsz, t_len, _ = proj.shape
    c_len = projc.shape[1]
    rows = t_len // GRID_W
    kr = min(WIN_R, rows)
    assert kr % 2 == 0 and rows % rows_per_step == 0
    qb, kb, vb = 0, NA_WIDTH // V7X_LANES, 2 * NA_WIDTH // V7X_LANES
    cos, sin = rope_tabs
    tq = rows_per_step * GRID_W
    seq_spec = lambda off: pl.BlockSpec((None, t_len, V7X_LANES), lambda b, hp, r: (b, 0, off + hp))
    ctx_spec = lambda off: pl.BlockSpec((None, c_len, V7X_LANES), lambda b, hp, r: (b, 0, off + hp))
    tab_spec = pl.BlockSpec((t_len, V7X_LANES), lambda b, hp, r: (0, 0))
    bias_spec = pl.BlockSpec((None, None) + bias_tab.shape[2:], lambda b, hp, r: (layer, hp, 0, 0, 0))
    return pl.pallas_call(
        functools.partial(_na_kernel, rows, kr),
        grid=(bsz, HEAD_PAIRS, rows // rows_per_step),
        in_specs=[pl.BlockSpec((None, tq, V7X_LANES), lambda b, hp, r: (b, r, qb + hp)),
                  seq_spec(kb), seq_spec(vb), ctx_spec(kb), ctx_spec(vb), tab_spec, tab_spec, bias_spec],
        out_specs=pl.BlockSpec((None, tq, V7X_LANES), lambda b, hp, r: (b, r, hp)),
        out_shape=jax.ShapeDtypeStruct((bsz, t_len, NA_WIDTH), BF16),
        scratch_shapes=[pltpu.VMEM((t_len, V7X_LANES), BF16), pltpu.VMEM((t_len, V7X_LANES), BF16),
                        pltpu.VMEM((c_len, V7X_LANES), BF16), pltpu.VMEM((c_len, V7X_LANES), BF16),
                        pltpu.VMEM((rows_per_step, 2 * GRID_W, kr * GRID_W + c_len), F32),
                        pltpu.VMEM((rows_per_step, 2 * GRID_W, kr * GRID_W + c_len), BF16)],
        compiler_params=_cparams(3, 32 * 1024 * 1024),
        name="neighbourhood_attention",
    )(proj, proj, proj, projc, projc, cos, sin, bias_tab)


def _ctx_attn_kernel(q_ref, k_ref, v_ref, o_ref):
    dh = NA_HEAD_DIM
    q = (q_ref[...] * (dh ** -0.5)).astype(BF16)
    k = k_ref[...].astype(BF16)
    v = v_ref[...].astype(BF16)
    for h in range(V7X_LANES // dh):
        hs = slice(h * dh, (h + 1) * dh)
        s = _dot_nt(q[:, hs], k[:, hs])
        p = jnp.exp(s - jnp.max(s, axis=-1, keepdims=True))
        o = jnp.dot(p.astype(BF16), v[:, hs], preferred_element_type=F32)
        o_ref[:, hs] = (o / jnp.sum(p, axis=-1, keepdims=True)).astype(o_ref.dtype)


def _context_attention(projc):
    bsz, c_len, _ = projc.shape
    qb, kb, vb = 0, NA_WIDTH // V7X_LANES, 2 * NA_WIDTH // V7X_LANES
    spec = lambda off: pl.BlockSpec((None, c_len, V7X_LANES), lambda b, hp: (b, 0, off + hp))
    return pl.pallas_call(
        _ctx_attn_kernel,
        grid=(bsz, HEAD_PAIRS),
        in_specs=[spec(qb), spec(kb), spec(vb)],
        out_specs=spec(0),
        out_shape=jax.ShapeDtypeStruct((bsz, c_len, NA_WIDTH), BF16),
        compiler_params=_cparams(2, 32 * 1024 * 1024),
        name="context_attention",
    )(projc, projc, projc)


def _merge_kernel(a_ref, att_ref, gs_ref, gn_ref, wv_ref, wg_ref, wn_ref, o_ref):
    a = a_ref[...]
    val = jnp.dot(a, wv_ref[...], preferred_element_type=F32)
    glu = jnp.dot(a, wg_ref[...], preferred_element_type=F32)
    br_n = jnp.dot(att_ref[...], wn_ref[...], preferred_element_type=F32)
    z = _sigmoid(gs_ref[...]) * (val * _sigmoid(glu)) + _sigmoid(gn_ref[...]) * br_n
    o_ref[...] = z.astype(o_ref.dtype)


def _merge(a, att, proj, layer, w_val, w_glu, w_na, tm=512, tn=512):
    m = a.shape[0]
    gs_off = 3 * NA_WIDTH // tn
    gn_off = gs_off + D_MODEL // tn
    act = pl.BlockSpec((tm, SSM_WIDTH), lambda j, i: (i, 0))
    wsp = pl.BlockSpec((None, SSM_WIDTH, tn), lambda j, i: (layer, 0, j))
    return pl.pallas_call(
        _merge_kernel,
        grid=(D_MODEL // tn, m // tm),
        in_specs=[act, act,
                  pl.BlockSpec((tm, tn), lambda j, i: (i, gs_off + j)),
                  pl.BlockSpec((tm, tn), lambda j, i: (i, gn_off + j)),
                  wsp, wsp, wsp],
        out_specs=pl.BlockSpec((tm, tn), lambda j, i: (i, j)),
        out_shape=jax.ShapeDtypeStruct((m, D_MODEL), BF16),
        compiler_params=_cparams(2, 40 * 1024 * 1024),
        name="merge",
    )(a, att, proj, proj, w_val, w_glu, w_na)


def _outproj_kernel(z_ref, w_ref, x_ref, gt_ref, gpost_ref, gpre_ref, sh_ref, sc_ref, x_out, h_out):
    out = jnp.dot(z_ref[...], w_ref[...], preferred_element_type=F32)
    x1 = x_ref[...] + gt_ref[...] * _rms(out, gpost_ref[...])
    x_out[...] = x1
    h_out[...] = (_rms(x1, gpre_ref[...]) * (1.0 + sc_ref[...]) + sh_ref[...]).astype(h_out.dtype)


def _outproj(z, w_out, layer, x, gt, g_post, g_pre, sh, sc, rows_per_batch, tm=256):
    m = z.shape[0]
    row = pl.BlockSpec((tm, D_MODEL), lambda i: (i, 0))
    vec = pl.BlockSpec((1, D_MODEL), lambda i: (0, 0))
    bvec = pl.BlockSpec((None, 1, D_MODEL), lambda i: (i * tm // rows_per_batch, 0, 0))
    return pl.pallas_call(
        _outproj_kernel,
        grid=(m // tm,),
        in_specs=[row, pl.BlockSpec((None, D_MODEL, D_MODEL), lambda i: (layer, 0, 0)), row, bvec, vec, vec, bvec,
                  bvec],
        out_specs=[row, row],
        out_shape=[jax.ShapeDtypeStruct((m, D_MODEL), F32), jax.ShapeDtypeStruct((m, D_MODEL), BF16)],
        compiler_params=_cparams(1, 48 * 1024 * 1024),
        name="out_proj",
    )(z, w_out, x, gt, g_post.reshape(1, D_MODEL), g_pre.reshape(1, D_MODEL), sh, sc)


def _mlp_kernel(emit_next, h_ref, w1_ref, w2_ref, x_ref, gt_ref, gpost_ref, *rest):
    if emit_next:
        gpre_ref, sh_ref, sc_ref, x_out, h_out, acc = rest
    else:
        x_out, acc = rest
    f = pl.program_id(1)

    @pl.when(f == 0)
    def _():
        acc[...] = jnp.zeros_like(acc)

    a = jnp.maximum(jnp.dot(h_ref[...], w1_ref[...], preferred_element_type=F32), 0.0)
    acc[...] += jnp.dot((a * a).astype(BF16), w2_ref[...], preferred_element_type=F32)

    @pl.when(f == pl.num_programs(1) - 1)
    def _():
        x2 = x_ref[...] + gt_ref[...] * _rms(acc[...], gpost_ref[...])
        x_out[...] = x2
        if emit_next:
            h_out[...] = (_rms(x2, gpre_ref[...]) * (1.0 + sc_ref[...]) + sh_ref[...]).astype(h_out.dtype)


def _mlp(h2, w1, w2, layer, x, gt, g_post, nxt, rows_per_batch, tm=512, tf=1024):
    m = h2.shape[0]
    emit_next = nxt is not None
    row = pl.BlockSpec((tm, D_MODEL), lambda i, f: (i, 0))
    vec = pl.BlockSpec((1, D_MODEL), lambda i, f: (0, 0))
    bvec = pl.BlockSpec((None, 1, D_MODEL), lambda i, f: (i * tm // rows_per_batch, 0, 0))
    in_specs = [row, pl.BlockSpec((None, D_MODEL, tf), lambda i, f: (layer, 0, f)),
                pl.BlockSpec((None, tf, D_MODEL), lambda i, f: (layer, f, 0)), row, bvec, vec]
    args = [h2, w1, w2, x, gt, g_post.reshape(1, D_MODEL)]
    out_specs = [row]
    out_shape = [jax.ShapeDtypeStruct((m, D_MODEL), F32)]
    if emit_next:
        g_pre, sh, sc = nxt
        in_specs += [vec, bvec, bvec]
        args += [g_pre.reshape(1, D_MODEL), sh, sc]
        out_specs.append(row)
        out_shape.append(jax.ShapeDtypeStruct((m, D_MODEL), BF16))
    res = pl.pallas_call(
        functools.partial(_mlp_kernel, emit_next),
        grid=(m // tm, D_FF // tf),
        in_specs=in_specs,
        out_specs=out_specs,
        out_shape=out_shape,
        scratch_shapes=[pltpu.VMEM((tm, D_MODEL), F32)],
        compiler_params=_cparams(2, 56 * 1024 * 1024),
        name="mlp",
    )(*args)
    return (res[0], res[1]) if emit_next else (res[0], None)


def kernel(x, c, ctx, c_ctx, w_mod, b_mod, g_pre_mix, g_post_mix, g_pre_mlp, g_post_mlp, w_in,
           ssm_lam_re, ssm_lam_im, ssm_log_dt, ssm_b_re, ssm_b_im, ssm_c_re, ssm_c_im, ssm_d,
           w_ssm_val, w_ssm_glu, na_rpb, w_na_proj, w_out, w_fc1, w_fc2):
    bsz, t_len, _ = x.shape
    c_len = ctx.shape[1]
    m_lat, m_ctx = bsz * t_len, bsz * c_len
    assert bsz < V7X_SUBLANES and t_len % (GRID_W * 8) == 0 and c_len % S5_CHUNK == 0

    c_rows = jnp.concatenate([c, c_ctx[None, :], jnp.zeros((V7X_SUBLANES - bsz - 1, D_MODEL), F32)], axis=0)
    mod = _modulation(c_rows, w_mod, b_mod)

    def mod_vectors(l):
        lat = [v[:, None, :] for v in jnp.split(mod[l, :bsz], N_MOD, axis=-1)]
        cvec = [jnp.broadcast_to(v[None, :, :], (bsz, 1, D_MODEL))
                for v in jnp.split(mod[l, bsz:bsz + 1], N_MOD, axis=-1)]
        return lat, cvec

    mods = [mod_vectors(l) for l in range(DEPTH)]
    rope_tabs = _rope_tables(t_len)
    bias_tabs = jax.vmap(_bias_table)(na_rpb)
    s5_mats = jax.vmap(_s5_matrices)(ssm_lam_re, ssm_lam_im, ssm_log_dt, ssm_b_re, ssm_b_im, ssm_c_re, ssm_c_im, ssm_d)

    xl = x.reshape(m_lat, D_MODEL)
    xc = ctx.reshape(m_ctx, D_MODEL)
    h = _norm_mod(xl, g_pre_mix[0], mods[0][0][0], mods[0][0][1], t_len)
    hc = _norm_mod(xc, g_pre_mix[0], mods[0][1][0], mods[0][1][1], c_len)

    w_in_b, w_val_b, w_glu_b = w_in.astype(BF16), w_ssm_val.astype(BF16), w_ssm_glu.astype(BF16)
    w_na_b, w_out_b = w_na_proj.astype(BF16), w_out.astype(BF16)
    w1_b, w2_b = w_fc1.astype(BF16), w_fc2.astype(BF16)

    def to_pos_major(v):
        m, w = v.shape
        return v.reshape(m // S5_CHUNK, S5_CHUNK, w).transpose(1, 0, 2).reshape(m, w)

    def from_pos_major(v):
        return v.transpose(1, 2, 0, 3).reshape(-1, v.shape[-1])

    for l in range(DEPTH):
        with_ctx_out = l < DEPTH - 1
        (sh1, sc1, gt1, sh2, sc2, gt2), (csh1, csc1, cgt1, csh2, csc2, cgt2) = mods[l]

        u_tiles = SSM_WIDTH // 1024
        proj = _matmul(h, w_in_b, l, 1024, 1024, F32, first_tile=u_tiles)
        projc = _matmul(hc, w_in_b, l, 1024, 1024, F32, first_tile=u_tiles)
        u_lat = _matmul(to_pos_major(h), w_in_b, l, 1024, 1024, BF16, n_tiles=u_tiles)
        u_ctx = _matmul(to_pos_major(hc), w_in_b, l, 1024, 1024, BF16, n_tiles=u_tiles)
        proj3 = proj.reshape(bsz, t_len, REST_WIDTH)
        projc3 = projc.reshape(bsz, c_len, REST_WIDTH)

        a_lat, a_ctx = _s5_branch(u_lat.reshape(S5_CHUNK, bsz, t_len // S5_CHUNK, SSM_WIDTH),
                                  u_ctx.reshape(S5_CHUNK, bsz, c_len // S5_CHUNK, SSM_WIDTH), s5_mats, l,
                                  bsz, t_len, c_len)

        att = _neighbourhood_attention(proj3, projc3, rope_tabs, bias_tabs, l)
        z = _merge(from_pos_major(a_lat), att.reshape(m_lat, NA_WIDTH), proj, l, w_val_b, w_glu_b, w_na_b)
        xl, h2 = _outproj(z, w_out_b, l, xl, gt1, g_post_mix[l], g_pre_mlp[l], sh2, sc2, t_len)
        nxt = (g_pre_mix[l + 1], mods[l + 1][0][0], mods[l + 1][0][1]) if with_ctx_out else None
        xl, h = _mlp(h2, w1_b, w2_b, l, xl, gt2, g_post_mlp[l], nxt, t_len)

        if with_ctx_out:
            attc = _context_attention(projc3)
            zc = _merge(from_pos_major(a_ctx), attc.reshape(m_ctx, NA_WIDTH), projc, l, w_val_b, w_glu_b, w_na_b)
            xc, h2c = _outproj(zc, w_out_b, l, xc, cgt1, g_post_mix[l], g_pre_mlp[l], csh2, csc2, c_len)
            nxtc = (g_pre_mix[l + 1], mods[l + 1][1][0], mods[l + 1][1][1])
            xc, hc = _mlp(h2c, w1_b, w2_b, l, xc, cgt2, g_post_mlp[l], nxtc, c_len)

    return xl.reshape(bsz, t_len, D_MODEL)
```

```python
import functools
import math

import numpy as np
import jax
import jax.numpy as jnp
from jax import lax
from jax.experimental import pallas as pl
from jax.experimental.pallas import tpu as pltpu

D_MODEL = 2048
DEPTH = 2
GRID_W = 64
N_MOD = 6
SSM_WIDTH = D_MODEL // 2
SSM_GROUP = 16
SSM_GROUPS = SSM_WIDTH // SSM_GROUP
SSM_STATE = 64
NA_HEADS = 16
NA_HEAD_DIM = 64
NA_WIDTH = NA_HEADS * NA_HEAD_DIM
WIN_R = 8
WIN_C = 16
ROPE_BASE = 10000.0
D_FF = 4 * D_MODEL
NORM_EPS = 1e-6
PROJ_WIDTH = SSM_WIDTH + 3 * NA_WIDTH + 2 * D_MODEL
REST_WIDTH = PROJ_WIDTH - SSM_WIDTH

V7X_LANES = 128
V7X_SUBLANES = 8
V7X_VMEM_BYTES = 64 * 1024 * 1024
HEAD_PAIRS = NA_HEADS * NA_HEAD_DIM // V7X_LANES
S5_CHUNK = 16
S5_TILE = S5_CHUNK * SSM_GROUP
S5_SCAN_TILE = V7X_SUBLANES
S5_GROUPS_PER_BLOCK = V7X_LANES // SSM_GROUP
S5_GROUPS_PER_PASS = 4
MASK_VALUE = -1e30

F32 = jnp.float32
BF16 = jnp.bfloat16


def _cparams(n_axes, vmem_bytes):
    assert vmem_bytes <= V7X_VMEM_BYTES
    return pltpu.CompilerParams(dimension_semantics=("arbitrary",) * n_axes, vmem_limit_bytes=vmem_bytes)


def _sigmoid(x):
    return 1.0 / (1.0 + jnp.exp(-x))


def _gelu_tanh(x):
    return 0.5 * x * (1.0 + jnp.tanh(math.sqrt(2.0 / math.pi) * (x + 0.044715 * (x * x * x))))


def _rms(x, g):
    return x * lax.rsqrt(jnp.mean(x * x, axis=-1, keepdims=True) + NORM_EPS) * g


def _mod_kernel(c_ref, w_ref, b_ref, o_ref):
    c = c_ref[...]
    s = (c * _sigmoid(c)).astype(BF16)
    o_ref[...] = jnp.dot(s, w_ref[...].astype(BF16), preferred_element_type=F32) + b_ref[...]


def _modulation(c_rows, w_mod, b_mod, tn=1024):
    n = w_mod.shape[-1]
    return pl.pallas_call(
        _mod_kernel,
        grid=(DEPTH, n // tn),
        in_specs=[pl.BlockSpec((V7X_SUBLANES, D_MODEL), lambda l, j: (0, 0)),
                  pl.BlockSpec((None, D_MODEL, tn), lambda l, j: (l, 0, j)),
                  pl.BlockSpec((None, 1, tn), lambda l, j: (l, 0, j))],
        out_specs=pl.BlockSpec((None, V7X_SUBLANES, tn), lambda l, j: (l, 0, j)),
        out_shape=jax.ShapeDtypeStruct((DEPTH, V7X_SUBLANES, n), F32),
        compiler_params=_cparams(2, 40 * 1024 * 1024),
        name="modulation",
    )(c_rows, w_mod, b_mod.reshape(DEPTH, 1, n))


def _norm_mod_kernel(x_ref, g_ref, sh_ref, sc_ref, o_ref):
    y = _rms(x_ref[...], g_ref[...])
    o_ref[...] = (y * (1.0 + sc_ref[...]) + sh_ref[...]).astype(o_ref.dtype)


def _norm_mod(x, g, sh, sc, rows_per_batch, tm=512):
    m = x.shape[0]
    bidx = lambda i: (i * tm // rows_per_batch, 0, 0)
    return pl.pallas_call(
        _norm_mod_kernel,
        grid=(m // tm,),
        in_specs=[pl.BlockSpec((tm, D_MODEL), lambda i: (i, 0)),
                  pl.BlockSpec((1, D_MODEL), lambda i: (0, 0)),
                  pl.BlockSpec((None, 1, D_MODEL), bidx),
                  pl.BlockSpec((None, 1, D_MODEL), bidx)],
        out_specs=pl.BlockSpec((tm, D_MODEL), lambda i: (i, 0)),
        out_shape=jax.ShapeDtypeStruct((m, D_MODEL), BF16),
        compiler_params=_cparams(1, 32 * 1024 * 1024),
        name="norm_mod",
    )(x, g.reshape(1, D_MODEL), sh, sc)


def _matmul_kernel(a_ref, w_ref, o_ref):
    o_ref[...] = jnp.dot(a_ref[...], w_ref[...], preferred_element_type=F32).astype(o_ref.dtype)


def _matmul(a, w, layer, tm, tn, out_dtype, first_tile=0, n_tiles=None):
    m, k = a.shape
    n_tiles = w.shape[2] // tn - first_tile if n_tiles is None else n_tiles
    return pl.pallas_call(
        _matmul_kernel,
        grid=(n_tiles, m // tm),
        in_specs=[pl.BlockSpec((tm, k), lambda j, i: (i, 0)),
                  pl.BlockSpec((None, k, tn), lambda j, i: (layer, 0, first_tile + j))],
        out_specs=pl.BlockSpec((tm, tn), lambda j, i: (i, j)),
        out_shape=jax.ShapeDtypeStruct((m, n_tiles * tn), out_dtype),
        compiler_params=_cparams(2, 40 * 1024 * 1024),
        name="in_proj",
    )(a, w)


def _s5_matrices(lam_re, lam_im, log_dt, b_re, b_im, c_re, c_im, d_skip):
    L = S5_CHUNK
    def cmul(a, b):
        return a[0] * b[0] - a[1] * b[1], a[0] * b[1] + a[1] * b[0]

    lam = (lam_re.astype(F32), lam_im.astype(F32))
    dt = jnp.exp(log_dt.astype(F32))[..., None]
    tau = jnp.arange(L + 1, dtype=F32)[None, None, :, None]
    mag = jnp.exp((lam[0] * dt)[:, :, None, :] * tau)
    ang = (lam[1] * dt)[:, :, None, :] * tau
    pw = (mag * jnp.cos(ang), mag * jnp.sin(ang))
    num = (pw[0][:, :, 1] - 1.0, pw[1][:, :, 1])
    den = lam[0] * lam[0] + lam[1] * lam[1]
    coef = ((num[0] * lam[0] + num[1] * lam[1]) / den, (num[1] * lam[0] - num[0] * lam[1]) / den)
    b_bar = cmul((coef[0][..., None], coef[1][..., None]), (b_re.astype(F32), b_im.astype(F32)))
    b_bar_t = (jnp.swapaxes(b_bar[0], -1, -2), jnp.swapaxes(b_bar[1], -1, -2))
    cmat = (c_re.astype(F32), c_im.astype(F32))

    def pw_at(d, sl):
        return pw[0][d][:, sl, None, :], pw[1][d][:, sl, None, :]

    c_lag = cmul((cmat[0][:, :, None], cmat[1][:, :, None]),
                 (pw[0][:, :, :L, None, :], pw[1][:, :, :L, None, :]))
    c_lag = jnp.concatenate(c_lag, axis=-1)
    c_lag = jnp.stack([c_lag[0], c_lag[1][:, ::-1]], axis=1).reshape(SSM_GROUPS, 2, S5_TILE, 2 * SSM_STATE)
    bb = jnp.moveaxis(jnp.concatenate([b_bar_t[0], -b_bar_t[1]], axis=-1), 0, 1)
    d_rows = jnp.broadcast_to(d_skip.astype(F32).reshape(SSM_GROUPS, SSM_GROUP, 1), (SSM_GROUPS, SSM_GROUP, S5_TILE))

    bf = cmul(pw_at(0, slice(L - 1, None, -1)), (b_bar_t[0][0][:, None], b_bar_t[1][0][:, None]))
    br = cmul(pw_at(1, slice(0, L)), (b_bar_t[0][1][:, None], b_bar_t[1][1][:, None]))
    b_in = jnp.concatenate([bf[0], br[0], bf[1], br[1]], axis=-1)
    b_in = b_in.reshape(SSM_GROUPS, S5_TILE, 4 * SSM_STATE)

    zf = cmul((cmat[0][0][:, None], cmat[1][0][:, None]), pw_at(0, slice(1, L + 1)))
    zr = cmul((cmat[0][1][:, None], cmat[1][1][:, None]), pw_at(1, slice(L, 0, -1)))
    c_out_t = jnp.concatenate([zf[0], zr[0], -zf[1], -zr[1]], axis=-1).reshape(SSM_GROUPS, S5_TILE, 4 * SSM_STATE)

    return (c_lag, bb, d_rows, b_in.astype(BF16), c_out_t.astype(BF16)) + _s5_tables(lam_re, lam_im, log_dt)


def _s5_tables(lam_re, lam_im, log_dt):
    n = S5_SCAN_TILE
    dt = jnp.exp(log_dt.astype(F32))[..., None]
    steps = jnp.arange(n + 1, dtype=F32)[None, None, :, None] * S5_CHUNK
    mag = jnp.exp((lam_re.astype(F32) * dt)[:, :, None, :] * steps)
    ang = (lam_im.astype(F32) * dt)[:, :, None, :] * steps
    pre, pim = mag * jnp.cos(ang), mag * jnp.sin(ang)
    both = lambda f, r: jnp.concatenate([f, r], axis=-1)
    sel = np.array([1, 2, 4, 8])
    apow = jnp.stack([both(pre[0][:, sel], pre[1][:, sel]), both(pim[0][:, sel], pim[1][:, sel])], axis=2)
    coef = jnp.stack([both(pre[0][:, :n], pre[1][:, n - 1::-1][:, :n]),
                      both(pim[0][:, :n], pim[1][:, n - 1::-1][:, :n])], axis=1)
    return apow, coef


def _dot_nt(a, b):
    return lax.dot_general(a, b, (((1,), (1,)), ((), ())), preferred_element_type=F32)


def _shift_lanes(x, s):
    w = V7X_LANES
    lo, hi = x[:, :w], x[:, w:]
    lane = lax.broadcasted_iota(jnp.int32, lo.shape, 1)
    if s == 0:
        return x
    if s > 0:
        q, r = divmod(s, w)
        if r == 0:
            out = (jnp.zeros_like(lo), lo)
        else:
            rl, rh = pltpu.roll(lo, r, 1), pltpu.roll(hi, r, 1)
            if q == 0:
                out = (jnp.where(lane >= r, rl, 0.0), jnp.where(lane >= r, rh, rl))
            else:
                out = (jnp.zeros_like(lo), jnp.where(lane >= r, rl, 0.0))
    else:
        q, r = divmod(-s, w)
        if r == 0:
            out = (hi, jnp.zeros_like(hi))
        else:
            rl, rh = pltpu.roll(lo, w - r, 1), pltpu.roll(hi, w - r, 1)
            if q == 0:
                out = (jnp.where(lane < w - r, rl, rh), jnp.where(lane < w - r, rh, 0.0))
            else:
                out = (jnp.where(lane < w - r, rh, 0.0), jnp.zeros_like(hi))
    return jnp.concatenate(out, axis=1)


def _s5_chunk_rows(c_lag, bb, d_rows, bin_ref, g):
    L = S5_CHUNK
    lag_kernels = lambda d: lax.dot_general(bb[d], c_lag[d], (((1,), (1,)), ((), ())),
                                            precision=lax.Precision.HIGHEST, preferred_element_type=F32)
    kl_f, kl_r = lag_kernels(0), lag_kernels(1)
    lane = lax.broadcasted_iota(jnp.int32, (SSM_GROUP, S5_TILE), 1)
    row = lax.broadcasted_iota(jnp.int32, (SSM_GROUP, S5_TILE), 0)
    blocks = []
    for t in range(L):
        m_rows = (_shift_lanes(kl_f, t * SSM_GROUP) + _shift_lanes(kl_r, -(L - 1 - t) * SSM_GROUP)
                  + jnp.where(lane == t * SSM_GROUP + row, d_rows, 0.0))
        blocks.append(jnp.concatenate([m_rows.astype(BF16), bin_ref[g, t * SSM_GROUP:(t + 1) * SSM_GROUP, :]], axis=1))
    return blocks


def _s5_kernel(bsz, nc_ctx, nc_lat, *refs):
    L = S5_CHUNK
    xl, xc, clag_ref, bb_ref, d_ref, bin_ref, cout_ref, apow_ref, coef_ref, ol, oc = refs[:11]
    xcat, ucat, y_scr, s_scr, ea_scr, eb_scr, acat, esel = refs[11:]
    nb = nc_ctx + nc_lat
    n_rows = bsz * nb
    lanes = V7X_LANES
    half = 2 * SSM_STATE
    tile = S5_SCAN_TILE
    gpb, gpp = S5_GROUPS_PER_BLOCK, S5_GROUPS_PER_PASS

    @pl.when(pl.program_id(0) == 0)
    def _():
        blk = 256
        for rb in range(esel.shape[0] // blk):
            r = lax.broadcasted_iota(jnp.int32, (blk, esel.shape[1]), 0) + rb * blk
            c = lax.broadcasted_iota(jnp.int32, (blk, esel.shape[1]), 1)
            dest = ((r % S5_TILE) // SSM_GROUP) * lanes + (r // S5_TILE) * SSM_GROUP + r % SSM_GROUP
            esel[rb * blk:(rb + 1) * blk, :] = jnp.where(dest == c, 1.0, 0.0).astype(BF16)

    for t in range(L):
        cols = slice(t * lanes, (t + 1) * lanes)
        for b in range(bsz):
            xcat[b * nb:b * nb + nc_ctx, cols] = xc[t, b]
            xcat[b * nb + nc_ctx:(b + 1) * nb, cols] = xl[t, b]

    ucat[...] = _dot_nt(xcat[...], esel[...]).astype(BF16)

    lane = lax.broadcasted_iota(jnp.int32, (tile, half), 1)
    row = lax.broadcasted_iota(jnp.int32, (tile, half), 0)
    is_fwd = lane < SSM_STATE
    shifts = (1, 2, 4)
    keep_dn = {s: is_fwd & (row >= s) for s in shifts}
    keep_up = {s: jnp.logical_not(is_fwd) & (row < tile - s) for s in shifts}

    def shift(d, s):
        return jnp.where(keep_dn[s], pltpu.roll(d, s, 0), jnp.where(keep_up[s], pltpu.roll(d, tile - s, 0), 0.0))

    def cfma(a_re, a_im, x_re, x_im, y_re, y_im):
        return a_re * x_re - a_im * x_im + y_re, a_re * x_im + a_im * x_re + y_im

    for p in range(gpb // gpp):
        for q in range(gpp):
            g = p * gpp + q
            w_g = jnp.concatenate(_s5_chunk_rows(clag_ref[g], bb_ref[g], d_ref[g], bin_ref, g), axis=0)
            ys = jnp.dot(ucat[:, g * S5_TILE:(g + 1) * S5_TILE], w_g, preferred_element_type=F32)
            y_scr[q] = ys[:, :S5_TILE]
            s_scr[q] = ys[:, S5_TILE:]

        bc = lambda v: jnp.broadcast_to(v, (tile, half))
        a_pow = [[(bc(apow_ref[p * gpp + q, k, 0:1, :]), bc(apow_ref[p * gpp + q, k, 1:2, :])) for k in range(4)]
                 for q in range(gpp)]
        coefs = [(coef_ref[p * gpp + q, 0], coef_ref[p * gpp + q, 1]) for q in range(gpp)]

        for b in range(bsz):
            def make_body(rev_base, base=b * nb):
                def body(i, carry):
                    rf = pl.multiple_of(base + i * tile, tile)
                    rr = pl.multiple_of(base + (rev_base - i) * tile, tile)
                    out = []
                    for q in range(gpp):
                        e_re, e_im = carry[2 * q], carry[2 * q + 1]
                        sf = s_scr[q, pl.ds(rf, tile), :]
                        sr = s_scr[q, pl.ds(rr, tile), :]
                        d_re = jnp.where(is_fwd, sf[:, :half], sr[:, :half])
                        d_im = jnp.where(is_fwd, sf[:, half:], sr[:, half:])
                        for k, s in enumerate(shifts):
                            d_re, d_im = cfma(a_pow[q][k][0], a_pow[q][k][1], shift(d_re, s), shift(d_im, s), d_re, d_im)
                        en_re, en_im = cfma(coefs[q][0], coefs[q][1], e_re, e_im, shift(d_re, 1), shift(d_im, 1))
                        enter = jnp.concatenate([en_re, en_im], axis=-1)
                        ea_scr[q, pl.ds(rf, tile), :] = enter
                        eb_scr[q, pl.ds(rr, tile), :] = enter
                        last_re = jnp.where(is_fwd, bc(d_re[tile - 1:tile]), bc(d_re[0:1]))
                        last_im = jnp.where(is_fwd, bc(d_im[tile - 1:tile]), bc(d_im[0:1]))
                        out.extend(cfma(a_pow[q][3][0], a_pow[q][3][1], e_re, e_im, last_re, last_im))
                    return tuple(out)
                return body

            zero = jnp.zeros((tile, half), F32)
            nt_ctx, nt = nc_ctx // tile, nb // tile
            carry = lax.fori_loop(0, nt_ctx, make_body(nt_ctx - 1), tuple(zero for _ in range(2 * gpp)))
            lax.fori_loop(nt_ctx, nt, make_body(nt - 1 + nt_ctx), carry)

        lane_all = lax.broadcasted_iota(jnp.int32, (n_rows, 2 * half), 1)
        fwd_all = (lane_all % half) < SSM_STATE
        for q in range(gpp):
            g = p * gpp + q
            enter = jnp.where(fwd_all, ea_scr[q], eb_scr[q]).astype(BF16)
            y = y_scr[q] + _dot_nt(enter, cout_ref[g])
            acat[:, g * S5_TILE:(g + 1) * S5_TILE] = _gelu_tanh(y).astype(BF16)

    for t2 in range(L // 2):
        cols = slice(t2 * 2 * lanes, (t2 + 1) * 2 * lanes)
        a_nat = jnp.dot(acat[...], esel[:, cols], preferred_element_type=F32).astype(BF16)
        for dt_ in range(2):
            t = 2 * t2 + dt_
            for b in range(bsz):
                oc[t, b] = a_nat[b * nb:b * nb + nc_ctx, dt_ * lanes:(dt_ + 1) * lanes]
                ol[t, b] = a_nat[b * nb + nc_ctx:(b + 1) * nb, dt_ * lanes:(dt_ + 1) * lanes]


def _s5_branch(u_lat, u_ctx, mats, layer, bsz, t_len, c_len):
    c_lag, bb, d_rows, b_in, c_out_t, apow, coef = mats
    L = S5_CHUNK
    nc_lat, nc_ctx = t_len // L, c_len // L
    assert nc_lat % S5_SCAN_TILE == 0 and nc_ctx % S5_SCAN_TILE == 0
    n_rows = bsz * (nc_lat + nc_ctx)
    gpb = S5_GROUPS_PER_BLOCK
    slab = lambda n: pl.BlockSpec((L, bsz, n, V7X_LANES), lambda j: (0, 0, 0, j))
    grp = lambda *tail: pl.BlockSpec((None, gpb) + tail, lambda j: (layer, j) + (0,) * len(tail))
    return pl.pallas_call(
        functools.partial(_s5_kernel, bsz, nc_ctx, nc_lat),
        grid=(SSM_GROUPS // gpb,),
        in_specs=[slab(nc_lat), slab(nc_ctx), grp(2, S5_TILE, 2 * SSM_STATE), grp(2, SSM_GROUP, 2 * SSM_STATE),
                  grp(SSM_GROUP, S5_TILE), grp(S5_TILE, S5_TILE), grp(S5_TILE, S5_TILE), grp(4, 2, 2 * SSM_STATE),
                  grp(2, S5_SCAN_TILE, 2 * SSM_STATE)],
        out_specs=[slab(nc_lat), slab(nc_ctx)],
        out_shape=[jax.ShapeDtypeStruct((L, bsz, nc_lat, SSM_WIDTH), BF16),
                   jax.ShapeDtypeStruct((L, bsz, nc_ctx, SSM_WIDTH), BF16)],
        scratch_shapes=[pltpu.VMEM((n_rows, L * V7X_LANES), BF16)] * 2
                       + [pltpu.VMEM((S5_GROUPS_PER_PASS, n_rows, S5_TILE), F32)] * 4
                       + [pltpu.VMEM((n_rows, gpb * S5_TILE), BF16),
                          pltpu.VMEM((gpb * S5_TILE, L * V7X_LANES), BF16)],
        compiler_params=_cparams(1, 56 * 1024 * 1024),
        name="s5",
    )(u_lat, u_ctx, c_lag, bb, d_rows, b_in, c_out_t, apow, coef)


def _rope_tables(t_len):
    nf = NA_HEAD_DIM // 4
    inv = ROPE_BASE ** (-jnp.arange(nf, dtype=F32) / nf)
    pos = jnp.arange(t_len, dtype=jnp.int32)
    ang_r = (pos // GRID_W).astype(F32)[:, None] * inv
    ang_c = (pos % GRID_W).astype(F32)[:, None] * inv
    cos = jnp.concatenate([jnp.cos(ang_r)] * 2 + [jnp.cos(ang_c)] * 2, axis=-1)
    sin = jnp.concatenate([-jnp.sin(ang_r), jnp.sin(ang_r), -jnp.sin(ang_c), jnp.sin(ang_c)], axis=-1)
    reps = V7X_LANES // NA_HEAD_DIM
    return jnp.tile(cos, (1, reps)), jnp.tile(sin, (1, reps))


def _rpb_rows(rpb):
    depth, n_h, n_r, n_c = rpb.shape
    padded = jnp.pad(rpb.astype(F32), ((0, 0), (0, 0), (0, 2 * WIN_R - n_r), (0, V7X_LANES - n_c)))
    return padded.reshape(depth, HEAD_PAIRS, n_h // HEAD_PAIRS, 2 * WIN_R, V7X_LANES)


def _build_bias(rpb_ref, bias_scr):
    kc = min(WIN_C, GRID_W)
    w = lax.broadcasted_iota(jnp.int32, (GRID_W, V7X_LANES), 0)
    lane = lax.broadcasted_iota(jnp.int32, (GRID_W, V7X_LANES), 1)
    kcol = lane % GRID_W
    c0 = jnp.clip(w - kc // 2, 0, GRID_W - kc)
    valid = (kcol >= c0) & (kcol < c0 + kc)
    for h in range(rpb_ref.shape[0]):
        lo, hi = [], []
        for ro in range(2 * WIN_R - 1):
            row = jnp.broadcast_to(rpb_ref[h, ro:ro + 1, :], (GRID_W, V7X_LANES))
            lo.append(pltpu.roll(row, V7X_LANES - (WIN_C - 1), 1, stride=1, stride_axis=0))
            hi.append(pltpu.roll(row, V7X_LANES - (WIN_C - 1) + GRID_W, 1, stride=1, stride_axis=0))
        for ro in range(2 * WIN_R - 2):
            both = jnp.where(lane < GRID_W, lo[ro], hi[ro + 1])
            bias_scr[ro, h * GRID_W:(h + 1) * GRID_W, :] = jnp.where(valid, both, MASK_VALUE)


def _rope(x, cos, sin_signed):
    nf = NA_HEAD_DIM // 4
    lane = lax.broadcasted_iota(jnp.int32, x.shape, 1)
    first = (lane % (2 * nf)) < nf
    swapped = jnp.where(first, pltpu.roll(x, V7X_LANES - nf, 1), pltpu.roll(x, nf, 1))
    return x * cos + swapped * sin_signed


def _na_kernel(rows, kr, q_ref, k_ref, v_ref, kc_ref, vc_ref, cos_ref, sin_ref, rpb_ref, o_ref,
               krope_scr, vb_scr, kcb_scr, vcb_scr, s_scr, p_scr, bias_scr):
    step = pl.program_id(2)
    dh = NA_HEAD_DIM
    t_len = k_ref.shape[0]
    rows_per_step = q_ref.shape[0] // GRID_W
    n_heads = V7X_LANES // dh

    @pl.when(step == 0)
    def _():
        chunk = 512
        for s in range(t_len // chunk):
            sl = pl.ds(s * chunk, chunk)
            krope_scr[sl, :] = _rope(k_ref[sl, :], cos_ref[sl, :], sin_ref[sl, :]).astype(BF16)
            vb_scr[sl, :] = v_ref[sl, :].astype(BF16)
        kcb_scr[...] = kc_ref[...].astype(BF16)
        vcb_scr[...] = vc_ref[...].astype(BF16)
        _build_bias(rpb_ref, bias_scr)

    lane = lax.broadcasted_iota(jnp.int32, (GRID_W, V7X_LANES), 1)
    n_loc = kr * GRID_W
    wins = []
    for i in range(rows_per_step):
        r = step * rows_per_step + i
        r0 = jnp.clip(r - kr // 2, 0, rows - kr)
        ro0 = r0 - r + (WIN_R - 1)
        win = pl.ds(pl.multiple_of(r0 * GRID_W, GRID_W), n_loc)
        wins.append(win)
        qrow = pl.ds(pl.multiple_of(r * GRID_W, GRID_W), GRID_W)
        q = _rope(q_ref[pl.ds(i * GRID_W, GRID_W), :], cos_ref[qrow, :], sin_ref[qrow, :]) * (dh ** -0.5)
        q2 = jnp.concatenate([jnp.where((lane // dh) == h, q, 0.0) for h in range(n_heads)], axis=0).astype(BF16)
        s_loc = _dot_nt(q2, krope_scr[win, :])
        for j in range(kr // 2):
            cols = slice(j * V7X_LANES, (j + 1) * V7X_LANES)
            s_scr[i, :, cols] = s_loc[:, cols] + bias_scr[ro0 + 2 * j]
        s_scr[i, :, n_loc:] = _dot_nt(q2, kcb_scr[...])
    for i in range(rows_per_step):
        s = s_scr[i]
        m = jnp.max(s, axis=-1, keepdims=True)
        p = jnp.exp(s - m)
        denom = jnp.sum(p, axis=-1, keepdims=True)
        p_scr[i] = p.astype(BF16)
        o = (jnp.dot(p_scr[i, :, :n_loc], vb_scr[wins[i], :], preferred_element_type=F32)
             + jnp.dot(p_scr[i, :, n_loc:], vcb_scr[...], preferred_element_type=F32)) / denom
        out = o[:GRID_W]
        for h in range(1, n_heads):
            out = jnp.where((lane // dh) == h, o[h * GRID_W:(h + 1) * GRID_W], out)
        o_ref[pl.ds(i * GRID_W, GRID_W), :] = out.astype(o_ref.dtype)


def _neighbourhood_attention(proj, projc, rope_tabs, rpb_rows, layer, rows_per_step=8):
    bsz, t_len, _ = proj.shape
    c_len = projc.shape[1]
    rows = t_len // GRID_W
    kr = min(WIN_R, rows)
    assert kr % 2 == 0 and rows % rows_per_step == 0
    qb, kb, vb = 0, NA_WIDTH // V7X_LANES, 2 * NA_WIDTH // V7X_LANES
    cos, sin = rope_tabs
    tq = rows_per_step * GRID_W
    seq_spec = lambda off: pl.BlockSpec((None, t_len, V7X_LANES), lambda b, hp, r: (b, 0, off + hp))
    ctx_spec = lambda off: pl.BlockSpec((None, c_len, V7X_LANES), lambda b, hp, r: (b, 0, off + hp))
    tab_spec = pl.BlockSpec((t_len, V7X_LANES), lambda b, hp, r: (0, 0))
    bias_spec = pl.BlockSpec((None, None) + rpb_rows.shape[2:], lambda b, hp, r: (layer, hp, 0, 0, 0))
    return pl.pallas_call(
        functools.partial(_na_kernel, rows, kr),
        grid=(bsz, HEAD_PAIRS, rows // rows_per_step),
        in_specs=[pl.BlockSpec((None, tq, V7X_LANES), lambda b, hp, r: (b, r, qb + hp)),
                  seq_spec(kb), seq_spec(vb), ctx_spec(kb), ctx_spec(vb), tab_spec, tab_spec, bias_spec],
        out_specs=pl.BlockSpec((None, tq, V7X_LANES), lambda b, hp, r: (b, r, hp)),
        out_shape=jax.ShapeDtypeStruct((bsz, t_len, NA_WIDTH), BF16),
        scratch_shapes=[pltpu.VMEM((t_len, V7X_LANES), BF16), pltpu.VMEM((t_len, V7X_LANES), BF16),
                        pltpu.VMEM((c_len, V7X_LANES), BF16), pltpu.VMEM((c_len, V7X_LANES), BF16),
                        pltpu.VMEM((rows_per_step, 2 * GRID_W, kr * GRID_W + c_len), F32),
                        pltpu.VMEM((rows_per_step, 2 * GRID_W, kr * GRID_W + c_len), BF16),
                        pltpu.VMEM((2 * WIN_R - 2, 2 * GRID_W, 2 * GRID_W), F32)],
        compiler_params=_cparams(3, 32 * 1024 * 1024),
        name="neighbourhood_attention",
    )(proj, proj, proj, projc, projc, cos, sin, rpb_rows)


def _ctx_attn_kernel(q_ref, k_ref, v_ref, o_ref):
    dh = NA_HEAD_DIM
    q = (q_ref[...] * (dh ** -0.5)).astype(BF16)
    k = k_ref[...].astype(BF16)
    v = v_ref[...].astype(BF16)
    for h in range(V7X_LANES // dh):
        hs = slice(h * dh, (h + 1) * dh)
        s = _dot_nt(q[:, hs], k[:, hs])
        p = jnp.exp(s - jnp.max(s, axis=-1, keepdims=True))
        o = jnp.dot(p.astype(BF16), v[:, hs], preferred_element_type=F32)
        o_ref[:, hs] = (o / jnp.sum(p, axis=-1, keepdims=True)).astype(o_ref.dtype)


def _context_attention(projc):
    bsz, c_len, _ = projc.shape
    qb, kb, vb = 0, NA_WIDTH // V7X_LANES, 2 * NA_WIDTH // V7X_LANES
    spec = lambda off: pl.BlockSpec((None, c_len, V7X_LANES), lambda b, hp: (b, 0, off + hp))
    return pl.pallas_call(
        _ctx_attn_kernel,
        grid=(bsz, HEAD_PAIRS),
        in_specs=[spec(qb), spec(kb), spec(vb)],
        out_specs=spec(0),
        out_shape=jax.ShapeDtypeStruct((bsz, c_len, NA_WIDTH), BF16),
        compiler_params=_cparams(2, 32 * 1024 * 1024),
        name="context_attention",
    )(projc, projc, projc)


def _merge_kernel(a_ref, att_ref, gs_ref, gn_ref, wv_ref, wg_ref, wn_ref, o_ref):
    a = a_ref[...]
    val = jnp.dot(a, wv_ref[...], preferred_element_type=F32)
    glu = jnp.dot(a, wg_ref[...], preferred_element_type=F32)
    br_n = jnp.dot(att_ref[...], wn_ref[...], preferred_element_type=F32)
    z = _sigmoid(gs_ref[...]) * (val * _sigmoid(glu)) + _sigmoid(gn_ref[...]) * br_n
    o_ref[...] = z.astype(o_ref.dtype)


def _merge(a, att, proj, layer, w_val, w_glu, w_na, tm=512, tn=512):
    m = a.shape[0]
    gs_off = 3 * NA_WIDTH // tn
    gn_off = gs_off + D_MODEL // tn
    act = pl.BlockSpec((tm, SSM_WIDTH), lambda j, i: (i, 0))
    wsp = pl.BlockSpec((None, SSM_WIDTH, tn), lambda j, i: (layer, 0, j))
    return pl.pallas_call(
        _merge_kernel,
        grid=(D_MODEL // tn, m // tm),
        in_specs=[act, act,
                  pl.BlockSpec((tm, tn), lambda j, i: (i, gs_off + j)),
                  pl.BlockSpec((tm, tn), lambda j, i: (i, gn_off + j)),
                  wsp, wsp, wsp],
        out_specs=pl.BlockSpec((tm, tn), lambda j, i: (i, j)),
        out_shape=jax.ShapeDtypeStruct((m, D_MODEL), BF16),
        compiler_params=_cparams(2, 40 * 1024 * 1024),
        name="merge",
    )(a, att, proj, proj, w_val, w_glu, w_na)


def _outproj_kernel(z_ref, w_ref, x_ref, gt_ref, gpost_ref, gpre_ref, sh_ref, sc_ref, x_out, h_out):
    out = jnp.dot(z_ref[...], w_ref[...], preferred_element_type=F32)
    x1 = x_ref[...] + gt_ref[...] * _rms(out, gpost_ref[...])
    x_out[...] = x1
    h_out[...] = (_rms(x1, gpre_ref[...]) * (1.0 + sc_ref[...]) + sh_ref[...]).astype(h_out.dtype)


def _outproj(z, w_out, layer, x, gt, g_post, g_pre, sh, sc, rows_per_batch, tm=256):
    m = z.shape[0]
    row = pl.BlockSpec((tm, D_MODEL), lambda i: (i, 0))
    vec = pl.BlockSpec((1, D_MODEL), lambda i: (0, 0))
    bvec = pl.BlockSpec((None, 1, D_MODEL), lambda i: (i * tm // rows_per_batch, 0, 0))
    return pl.pallas_call(
        _outproj_kernel,
        grid=(m // tm,),
        in_specs=[row, pl.BlockSpec((None, D_MODEL, D_MODEL), lambda i: (layer, 0, 0)), row, bvec, vec, vec, bvec,
                  bvec],
        out_specs=[row, row],
        out_shape=[jax.ShapeDtypeStruct((m, D_MODEL), F32), jax.ShapeDtypeStruct((m, D_MODEL), BF16)],
        compiler_params=_cparams(1, 48 * 1024 * 1024),
        name="out_proj",
    )(z, w_out, x, gt, g_post.reshape(1, D_MODEL), g_pre.reshape(1, D_MODEL), sh, sc)


def _mlp_kernel(emit_next, h_ref, w1_ref, w2_ref, x_ref, gt_ref, gpost_ref, *rest):
    if emit_next:
        gpre_ref, sh_ref, sc_ref, x_out, h_out, acc = rest
    else:
        x_out, acc = rest
    f = pl.program_id(1)

    @pl.when(f == 0)
    def _():
        acc[...] = jnp.zeros_like(acc)

    a = jnp.maximum(jnp.dot(h_ref[...], w1_ref[...], preferred_element_type=F32), 0.0)
    acc[...] += jnp.dot((a * a).astype(BF16), w2_ref[...], preferred_element_type=F32)

    @pl.when(f == pl.num_programs(1) - 1)
    def _():
        x2 = x_ref[...] + gt_ref[...] * _rms(acc[...], gpost_ref[...])
        x_out[...] = x2
        if emit_next:
            h_out[...] = (_rms(x2, gpre_ref[...]) * (1.0 + sc_ref[...]) + sh_ref[...]).astype(h_out.dtype)


def _mlp(h2, w1, w2, layer, x, gt, g_post, nxt, rows_per_batch, tm=512, tf=1024):
    m = h2.shape[0]
    emit_next = nxt is not None
    row = pl.BlockSpec((tm, D_MODEL), lambda i, f: (i, 0))
    vec = pl.BlockSpec((1, D_MODEL), lambda i, f: (0, 0))
    bvec = pl.BlockSpec((None, 1, D_MODEL), lambda i, f: (i * tm // rows_per_batch, 0, 0))
    in_specs = [row, pl.BlockSpec((None, D_MODEL, tf), lambda i, f: (layer, 0, f)),
                pl.BlockSpec((None, tf, D_MODEL), lambda i, f: (layer, f, 0)), row, bvec, vec]
    args = [h2, w1, w2, x, gt, g_post.reshape(1, D_MODEL)]
    out_specs = [row]
    out_shape = [jax.ShapeDtypeStruct((m, D_MODEL), F32)]
    if emit_next:
        g_pre, sh, sc = nxt
        in_specs += [vec, bvec, bvec]
        args += [g_pre.reshape(1, D_MODEL), sh, sc]
        out_specs.append(row)
        out_shape.append(jax.ShapeDtypeStruct((m, D_MODEL), BF16))
    res = pl.pallas_call(
        functools.partial(_mlp_kernel, emit_next),
        grid=(m // tm, D_FF // tf),
        in_specs=in_specs,
        out_specs=out_specs,
        out_shape=out_shape,
        scratch_shapes=[pltpu.VMEM((tm, D_MODEL), F32)],
        compiler_params=_cparams(2, 56 * 1024 * 1024),
        name="mlp",
    )(*args)
    return (res[0], res[1]) if emit_next else (res[0], None)


def kernel(x, c, ctx, c_ctx, w_mod, b_mod, g_pre_mix, g_post_mix, g_pre_mlp, g_post_mlp, w_in,
           ssm_lam_re, ssm_lam_im, ssm_log_dt, ssm_b_re, ssm_b_im, ssm_c_re, ssm_c_im, ssm_d,
           w_ssm_val, w_ssm_glu, na_rpb, w_na_proj, w_out, w_fc1, w_fc2):
    bsz, t_len, _ = x.shape
    c_len = ctx.shape[1]
    m_lat, m_ctx = bsz * t_len, bsz * c_len
    assert bsz < V7X_SUBLANES and t_len % (GRID_W * 8) == 0 and c_len % S5_CHUNK == 0

    c_rows = jnp.concatenate([c, c_ctx[None, :], jnp.zeros((V7X_SUBLANES - bsz - 1, D_MODEL), F32)], axis=0)
    mod = _modulation(c_rows, w_mod, b_mod)

    def mod_vectors(l):
        lat = [v[:, None, :] for v in jnp.split(mod[l, :bsz], N_MOD, axis=-1)]
        cvec = [jnp.broadcast_to(v[None, :, :], (bsz, 1, D_MODEL))
                for v in jnp.split(mod[l, bsz:bsz + 1], N_MOD, axis=-1)]
        return lat, cvec

    mods = [mod_vectors(l) for l in range(DEPTH)]
    rope_tabs = _rope_tables(t_len)
    rpb_rows = _rpb_rows(na_rpb)
    s5_mats = jax.vmap(_s5_matrices)(ssm_lam_re, ssm_lam_im, ssm_log_dt, ssm_b_re, ssm_b_im, ssm_c_re, ssm_c_im, ssm_d)

    xl = x.reshape(m_lat, D_MODEL)
    xc = ctx.reshape(m_ctx, D_MODEL)
    h = _norm_mod(xl, g_pre_mix[0], mods[0][0][0], mods[0][0][1], t_len)
    hc = _norm_mod(xc, g_pre_mix[0], mods[0][1][0], mods[0][1][1], c_len)

    w_in_b, w_val_b, w_glu_b = w_in.astype(BF16), w_ssm_val.astype(BF16), w_ssm_glu.astype(BF16)
    w_na_b, w_out_b = w_na_proj.astype(BF16), w_out.astype(BF16)
    w1_b, w2_b = w_fc1.astype(BF16), w_fc2.astype(BF16)

    def to_pos_major(v):
        m, w = v.shape
        return v.reshape(m // S5_CHUNK, S5_CHUNK, w).transpose(1, 0, 2).reshape(m, w)

    def from_pos_major(v):
        return v.transpose(1, 2, 0, 3).reshape(-1, v.shape[-1])

    for l in range(DEPTH):
        with_ctx_out = l < DEPTH - 1
        (sh1, sc1, gt1, sh2, sc2, gt2), (csh1, csc1, cgt1, csh2, csc2, cgt2) = mods[l]

        u_tiles = SSM_WIDTH // 1024
        proj = _matmul(h, w_in_b, l, 1024, 1024, F32, first_tile=u_tiles)
        projc = _matmul(hc, w_in_b, l, 1024, 1024, F32, first_tile=u_tiles)
        u_lat = _matmul(to_pos_major(h), w_in_b, l, 1024, 1024, BF16, n_tiles=u_tiles)
        u_ctx = _matmul(to_pos_major(hc), w_in_b, l, 1024, 1024, BF16, n_tiles=u_tiles)
        proj3 = proj.reshape(bsz, t_len, REST_WIDTH)
        projc3 = projc.reshape(bsz, c_len, REST_WIDTH)

        a_lat, a_ctx = _s5_branch(u_lat.reshape(S5_CHUNK, bsz, t_len // S5_CHUNK, SSM_WIDTH),
                                  u_ctx.reshape(S5_CHUNK, bsz, c_len // S5_CHUNK, SSM_WIDTH), s5_mats, l,
                                  bsz, t_len, c_len)

        att = _neighbourhood_attention(proj3, projc3, rope_tabs, rpb_rows, l)
        z = _merge(from_pos_major(a_lat), att.reshape(m_lat, NA_WIDTH), proj, l, w_val_b, w_glu_b, w_na_b)
        xl, h2 = _outproj(z, w_out_b, l, xl, gt1, g_post_mix[l], g_pre_mlp[l], sh2, sc2, t_len)
        nxt = (g_pre_mix[l + 1], mods[l + 1][0][0], mods[l + 1][0][1]) if with_ctx_out else None
        xl, h = _mlp(h2, w1_b, w2_b, l, xl, gt2, g_post_mlp[l], nxt, t_len)

        if with_ctx_out:
            attc = _context_attention(projc3)
            zc = _merge(from_pos_major(a_ctx), attc.reshape(m_ctx, NA_WIDTH), projc, l, w_val_b, w_glu_b, w_na_b)
            xc, h2c = _outproj(zc, w_out_b, l, xc, cgt1, g_post_mix[l], g_pre_mlp[l], csh2, csc2, c_len)
            nxtc = (g_pre_mix[l + 1], mods[l + 1][1][0], mods[l + 1][1][1])
            xc, hc = _mlp(h2c, w1_b, w2_b, l, xc, cgt2, g_post_mlp[l], nxtc, c_len)

    return xl.reshape(bsz, t_len, D_MODEL)
```

```python
import functools
import math

import numpy as np
import jax
import jax.numpy as jnp
from jax import lax
from jax.experimental import pallas as pl
from jax.experimental.pallas import tpu as pltpu

D_MODEL = 2048
DEPTH = 2
GRID_W = 64
N_MOD = 6
SSM_WIDTH = D_MODEL // 2
SSM_GROUP = 16
SSM_GROUPS = SSM_WIDTH // SSM_GROUP
SSM_STATE = 64
NA_HEADS = 16
NA_HEAD_DIM = 64
NA_WIDTH = NA_HEADS * NA_HEAD_DIM
WIN_R = 8
WIN_C = 16
ROPE_BASE = 10000.0
D_FF = 4 * D_MODEL
NORM_EPS = 1e-6
PROJ_WIDTH = SSM_WIDTH + 3 * NA_WIDTH + 2 * D_MODEL
REST_WIDTH = PROJ_WIDTH - SSM_WIDTH

V7X_LANES = 128
V7X_SUBLANES = 8
V7X_VMEM_BYTES = 64 * 1024 * 1024
HEAD_PAIRS = NA_HEADS * NA_HEAD_DIM // V7X_LANES
S5_CHUNK = 16
S5_TILE = S5_CHUNK * SSM_GROUP
S5_SCAN_TILE = V7X_SUBLANES
S5_GROUPS_PER_BLOCK = V7X_LANES // SSM_GROUP
S5_GROUPS_PER_PASS = 4
MASK_VALUE = -1e30

F32 = jnp.float32
BF16 = jnp.bfloat16


def _cparams(n_axes, vmem_bytes):
    assert vmem_bytes <= V7X_VMEM_BYTES
    return pltpu.CompilerParams(dimension_semantics=("arbitrary",) * n_axes, vmem_limit_bytes=vmem_bytes)


def _sigmoid(x):
    return 0.5 * jnp.tanh(0.5 * x) + 0.5


def _gelu_tanh(x):
    return 0.5 * x * (1.0 + jnp.tanh(math.sqrt(2.0 / math.pi) * (x + 0.044715 * (x * x * x))))


def _rms(x, g):
    return x * lax.rsqrt(jnp.mean(x * x, axis=-1, keepdims=True) + NORM_EPS) * g


def _mod_kernel(c_ref, w_ref, b_ref, o_ref):
    c = c_ref[...]
    s = (c * _sigmoid(c)).astype(BF16)
    o_ref[...] = jnp.dot(s, w_ref[...].astype(BF16), preferred_element_type=F32) + b_ref[...]


def _modulation(c_rows, w_mod, b_mod, tn=1024):
    n = w_mod.shape[-1]
    return pl.pallas_call(
        _mod_kernel,
        grid=(DEPTH, n // tn),
        in_specs=[pl.BlockSpec((V7X_SUBLANES, D_MODEL), lambda l, j: (0, 0)),
                  pl.BlockSpec((None, D_MODEL, tn), lambda l, j: (l, 0, j)),
                  pl.BlockSpec((None, 1, tn), lambda l, j: (l, 0, j))],
        out_specs=pl.BlockSpec((None, V7X_SUBLANES, tn), lambda l, j: (l, 0, j)),
        out_shape=jax.ShapeDtypeStruct((DEPTH, V7X_SUBLANES, n), F32),
        compiler_params=_cparams(2, 40 * 1024 * 1024),
        name="modulation",
    )(c_rows, w_mod, b_mod.reshape(DEPTH, 1, n))


def _norm_mod_kernel(x_ref, g_ref, sh_ref, sc_ref, o_ref):
    y = _rms(x_ref[...], g_ref[...])
    o_ref[...] = (y * (1.0 + sc_ref[...]) + sh_ref[...]).astype(o_ref.dtype)


def _norm_mod(x, g, sh, sc, rows_per_batch, tm=512):
    m = x.shape[0]
    bidx = lambda i: (i * tm // rows_per_batch, 0, 0)
    return pl.pallas_call(
        _norm_mod_kernel,
        grid=(m // tm,),
        in_specs=[pl.BlockSpec((tm, D_MODEL), lambda i: (i, 0)),
                  pl.BlockSpec((1, D_MODEL), lambda i: (0, 0)),
                  pl.BlockSpec((None, 1, D_MODEL), bidx),
                  pl.BlockSpec((None, 1, D_MODEL), bidx)],
        out_specs=pl.BlockSpec((tm, D_MODEL), lambda i: (i, 0)),
        out_shape=jax.ShapeDtypeStruct((m, D_MODEL), BF16),
        compiler_params=_cparams(1, 32 * 1024 * 1024),
        name="norm_mod",
    )(x, g.reshape(1, D_MODEL), sh, sc)


def _matmul_kernel(a_ref, w_ref, o_ref):
    o_ref[...] = jnp.dot(a_ref[...], w_ref[...], preferred_element_type=F32).astype(o_ref.dtype)


def _matmul(a, w, layer, tm, tn, out_dtype, first_tile=0, n_tiles=None):
    m, k = a.shape
    n_tiles = w.shape[2] // tn - first_tile if n_tiles is None else n_tiles
    return pl.pallas_call(
        _matmul_kernel,
        grid=(n_tiles, m // tm),
        in_specs=[pl.BlockSpec((tm, k), lambda j, i: (i, 0)),
                  pl.BlockSpec((None, k, tn), lambda j, i: (layer, 0, first_tile + j))],
        out_specs=pl.BlockSpec((tm, tn), lambda j, i: (i, j)),
        out_shape=jax.ShapeDtypeStruct((m, n_tiles * tn), out_dtype),
        compiler_params=_cparams(2, 40 * 1024 * 1024),
        name="in_proj",
    )(a, w)


def _s5_matrices(lam_re, lam_im, log_dt, b_re, b_im, c_re, c_im, d_skip):
    L = S5_CHUNK
    lam = (lam_re.astype(F32), lam_im.astype(F32))
    dt = jnp.exp(log_dt.astype(F32))[..., None]
    tau = jnp.arange(L + 1, dtype=F32)[None, None, :, None]
    mag = jnp.exp((lam[0] * dt)[:, :, None, :] * tau)
    ang = (lam[1] * dt)[:, :, None, :] * tau
    pw = (mag * jnp.cos(ang), mag * jnp.sin(ang))
    num = (pw[0][:, :, 1] - 1.0, pw[1][:, :, 1])
    den = lam[0] * lam[0] + lam[1] * lam[1]
    coef = ((num[0] * lam[0] + num[1] * lam[1]) / den, (num[1] * lam[0] - num[0] * lam[1]) / den)
    b32 = (b_re.astype(F32), b_im.astype(F32))
    b_bar = (coef[0][..., None] * b32[0] - coef[1][..., None] * b32[1],
             coef[0][..., None] * b32[1] + coef[1][..., None] * b32[0])
    both = lambda v: jnp.concatenate([v[0], v[1]], axis=-1)
    pw_tab = jnp.stack([both(pw[0]), both(pw[1])], axis=1)
    bc_tab = jnp.stack([both(jnp.swapaxes(b_bar[0], -1, -2)), both(jnp.swapaxes(b_bar[1], -1, -2)),
                        both(c_re.astype(F32)), both(c_im.astype(F32))], axis=1)
    d_rows = jnp.broadcast_to(d_skip.astype(F32).reshape(SSM_GROUPS, SSM_GROUP, 1), (SSM_GROUPS, SSM_GROUP, S5_TILE))
    return (pw_tab, bc_tab, d_rows) + _s5_tables(lam_re, lam_im, log_dt)


def _s5_tables(lam_re, lam_im, log_dt):
    n = S5_SCAN_TILE
    dt = jnp.exp(log_dt.astype(F32))[..., None]
    steps = jnp.arange(n + 1, dtype=F32)[None, None, :, None] * S5_CHUNK
    mag = jnp.exp((lam_re.astype(F32) * dt)[:, :, None, :] * steps)
    ang = (lam_im.astype(F32) * dt)[:, :, None, :] * steps
    pre, pim = mag * jnp.cos(ang), mag * jnp.sin(ang)
    both = lambda f, r: jnp.concatenate([f, r], axis=-1)
    sel = np.array([1, 2, 4, 8])
    apow = jnp.stack([both(pre[0][:, sel], pre[1][:, sel]), both(pim[0][:, sel], pim[1][:, sel])], axis=2)
    coef = jnp.stack([both(pre[0][:, :n], pre[1][:, n - 1::-1][:, :n]),
                      both(pim[0][:, :n], pim[1][:, n - 1::-1][:, :n])], axis=1)
    return apow, coef


def _dot_nt(a, b):
    return lax.dot_general(a, b, (((1,), (1,)), ((), ())), preferred_element_type=F32)


def _shift_lanes(x, s):
    w = V7X_LANES
    lo, hi = x[:, :w], x[:, w:]
    lane = lax.broadcasted_iota(jnp.int32, lo.shape, 1)
    if s == 0:
        return x
    if s > 0:
        q, r = divmod(s, w)
        if r == 0:
            out = (jnp.zeros_like(lo), lo)
        else:
            rl, rh = pltpu.roll(lo, r, 1), pltpu.roll(hi, r, 1)
            if q == 0:
                out = (jnp.where(lane >= r, rl, 0.0), jnp.where(lane >= r, rh, rl))
            else:
                out = (jnp.zeros_like(lo), jnp.where(lane >= r, rl, 0.0))
    else:
        q, r = divmod(-s, w)
        if r == 0:
            out = (hi, jnp.zeros_like(hi))
        else:
            rl, rh = pltpu.roll(lo, w - r, 1), pltpu.roll(hi, w - r, 1)
            if q == 0:
                out = (jnp.where(lane < w - r, rl, rh), jnp.where(lane < w - r, rh, 0.0))
            else:
                out = (jnp.where(lane < w - r, rh, 0.0), jnp.zeros_like(hi))
    return jnp.concatenate(out, axis=1)


def _s5_group_tables(pw, bc, d_rows):
    L = S5_CHUNK
    half = 2 * SSM_STATE
    lane = lax.broadcasted_iota(jnp.int32, (SSM_GROUP, half), 1)
    fwd = lane < SSM_STATE
    b_r, b_i, c_r, c_i = bc[0], bc[1], bc[2], bc[3]

    def powers(tau_f, tau_r):
        sel = lambda part: jnp.where(fwd, jnp.broadcast_to(pw[part, tau_f:tau_f + 1, :], (SSM_GROUP, half)),
                                     jnp.broadcast_to(pw[part, tau_r:tau_r + 1, :], (SSM_GROUP, half)))
        return sel(0), sel(1)

    def times(m_r, m_i, tau_f, tau_r):
        p_r, p_i = powers(tau_f, tau_r)
        return m_r * p_r - m_i * p_i, m_r * p_i + m_i * p_r

    b_in, c_out_t, c_lag = [], [], []
    for t in range(L):
        re, im = times(b_r, b_i, L - 1 - t, t)
        b_in.append(jnp.concatenate([re, im], axis=1).astype(BF16))
        re, im = times(c_r, c_i, t + 1, L - t)
        c_out_t.append(jnp.concatenate([re, -im], axis=1).astype(BF16))
        re, im = times(c_r, c_i, t, L - 1 - t)
        c_lag.append(jnp.concatenate([re, im], axis=1))
    c_lag = jnp.concatenate(c_lag, axis=0)
    zero = jnp.zeros_like(b_r)
    lag_kernels = lambda keep: lax.dot_general(
        jnp.concatenate([jnp.where(keep, b_r, zero), jnp.where(keep, -b_i, zero)], axis=1), c_lag,
        (((1,), (1,)), ((), ())), precision=lax.Precision.HIGHEST, preferred_element_type=F32)
    kl_f, kl_r = lag_kernels(fwd), lag_kernels(jnp.logical_not(fwd))

    lane_t = lax.broadcasted_iota(jnp.int32, (SSM_GROUP, S5_TILE), 1)
    row_t = lax.broadcasted_iota(jnp.int32, (SSM_GROUP, S5_TILE), 0)
    blocks = []
    for t in range(L):
        m_rows = (_shift_lanes(kl_f, t * SSM_GROUP) + _shift_lanes(kl_r, -(L - 1 - t) * SSM_GROUP)
                  + jnp.where(lane_t == t * SSM_GROUP + row_t, d_rows, 0.0))
        blocks.append(jnp.concatenate([m_rows.astype(BF16), b_in[t]], axis=1))
    return jnp.concatenate(blocks, axis=0), jnp.concatenate(c_out_t, axis=0)


def _s5_kernel(bsz, nc_ctx, nc_lat, *refs):
    L = S5_CHUNK
    xl, xc, pw_ref, bc_ref, d_ref, apow_ref, coef_ref, ol, oc = refs[:9]
    xcat, ucat, y_scr, s_scr, ea_scr, eb_scr, acat, esel = refs[9:]
    nb = nc_ctx + nc_lat
    n_rows = bsz * nb
    lanes = V7X_LANES
    half = 2 * SSM_STATE
    tile = S5_SCAN_TILE
    gpb, gpp = S5_GROUPS_PER_BLOCK, S5_GROUPS_PER_PASS

    @pl.when(pl.program_id(0) == 0)
    def _():
        blk = 256
        for rb in range(esel.shape[0] // blk):
            r = lax.broadcasted_iota(jnp.int32, (blk, esel.shape[1]), 0) + rb * blk
            c = lax.broadcasted_iota(jnp.int32, (blk, esel.shape[1]), 1)
            dest = ((r % S5_TILE) // SSM_GROUP) * lanes + (r // S5_TILE) * SSM_GROUP + r % SSM_GROUP
            esel[rb * blk:(rb + 1) * blk, :] = jnp.where(dest == c, 1.0, 0.0).astype(BF16)

    for t in range(L):
        cols = slice(t * lanes, (t + 1) * lanes)
        for b in range(bsz):
            xcat[b * nb:b * nb + nc_ctx, cols] = xc[t, b]
            xcat[b * nb + nc_ctx:(b + 1) * nb, cols] = xl[t, b]

    ucat[...] = _dot_nt(xcat[...], esel[...]).astype(BF16)

    lane = lax.broadcasted_iota(jnp.int32, (tile, half), 1)
    row = lax.broadcasted_iota(jnp.int32, (tile, half), 0)
    is_fwd = lane < SSM_STATE
    shifts = (1, 2, 4)
    keep_dn = {s: is_fwd & (row >= s) for s in shifts}
    keep_up = {s: jnp.logical_not(is_fwd) & (row < tile - s) for s in shifts}

    def shift(d, s):
        return jnp.where(keep_dn[s], pltpu.roll(d, s, 0), jnp.where(keep_up[s], pltpu.roll(d, tile - s, 0), 0.0))

    def cfma(a_re, a_im, x_re, x_im, y_re, y_im):
        return a_re * x_re - a_im * x_im + y_re, a_re * x_im + a_im * x_re + y_im

    for p in range(gpb // gpp):
        readouts = []
        for q in range(gpp):
            g = p * gpp + q
            w_g, c_out_t = _s5_group_tables(pw_ref[g], bc_ref[g], d_ref[g])
            readouts.append(c_out_t)
            ys = jnp.dot(ucat[:, g * S5_TILE:(g + 1) * S5_TILE], w_g, preferred_element_type=F32)
            y_scr[q] = ys[:, :S5_TILE]
            s_scr[q] = ys[:, S5_TILE:]

        bc = lambda v: jnp.broadcast_to(v, (tile, half))
        a_pow = [[(bc(apow_ref[p * gpp + q, k, 0:1, :]), bc(apow_ref[p * gpp + q, k, 1:2, :])) for k in range(4)]
                 for q in range(gpp)]
        coefs = [(coef_ref[p * gpp + q, 0], coef_ref[p * gpp + q, 1]) for q in range(gpp)]

        for b in range(bsz):
            def make_body(rev_base, base=b * nb):
                def body(i, carry):
                    rf = pl.multiple_of(base + i * tile, tile)
                    rr = pl.multiple_of(base + (rev_base - i) * tile, tile)
                    out = []
                    for q in range(gpp):
                        e_re, e_im = carry[2 * q], carry[2 * q + 1]
                        sf = s_scr[q, pl.ds(rf, tile), :]
                        sr = s_scr[q, pl.ds(rr, tile), :]
                        d_re = jnp.where(is_fwd, sf[:, :half], sr[:, :half])
                        d_im = jnp.where(is_fwd, sf[:, half:], sr[:, half:])
                        for k, s in enumerate(shifts):
                            d_re, d_im = cfma(a_pow[q][k][0], a_pow[q][k][1], shift(d_re, s), shift(d_im, s), d_re, d_im)
                        en_re, en_im = cfma(coefs[q][0], coefs[q][1], e_re, e_im, shift(d_re, 1), shift(d_im, 1))
                        enter = jnp.concatenate([en_re, en_im], axis=-1)
                        ea_scr[q, pl.ds(rf, tile), :] = enter
                        eb_scr[q, pl.ds(rr, tile), :] = enter
                        last_re = jnp.where(is_fwd, bc(d_re[tile - 1:tile]), bc(d_re[0:1]))
                        last_im = jnp.where(is_fwd, bc(d_im[tile - 1:tile]), bc(d_im[0:1]))
                        out.extend(cfma(a_pow[q][3][0], a_pow[q][3][1], e_re, e_im, last_re, last_im))
                    return tuple(out)
                return body

            zero = jnp.zeros((tile, half), F32)
            nt_ctx, nt = nc_ctx // tile, nb // tile
            carry = lax.fori_loop(0, nt_ctx, make_body(nt_ctx - 1), tuple(zero for _ in range(2 * gpp)))
            lax.fori_loop(nt_ctx, nt, make_body(nt - 1 + nt_ctx), carry)

        lane_all = lax.broadcasted_iota(jnp.int32, (n_rows, 2 * half), 1)
        fwd_all = (lane_all % half) < SSM_STATE
        for q in range(gpp):
            g = p * gpp + q
            enter = jnp.where(fwd_all, ea_scr[q], eb_scr[q]).astype(BF16)
            y = y_scr[q] + _dot_nt(enter, readouts[q])
            acat[:, g * S5_TILE:(g + 1) * S5_TILE] = _gelu_tanh(y).astype(BF16)

    for t2 in range(L // 2):
        cols = slice(t2 * 2 * lanes, (t2 + 1) * 2 * lanes)
        a_nat = jnp.dot(acat[...], esel[:, cols], preferred_element_type=F32).astype(BF16)
        for dt_ in range(2):
            t = 2 * t2 + dt_
            for b in range(bsz):
                oc[t, b] = a_nat[b * nb:b * nb + nc_ctx, dt_ * lanes:(dt_ + 1) * lanes]
                ol[t, b] = a_nat[b * nb + nc_ctx:(b + 1) * nb, dt_ * lanes:(dt_ + 1) * lanes]


def _s5_branch(u_lat, u_ctx, mats, layer, bsz, t_len, c_len):
    pw_tab, bc_tab, d_rows, apow, coef = mats
    L = S5_CHUNK
    nc_lat, nc_ctx = t_len // L, c_len // L
    assert nc_lat % S5_SCAN_TILE == 0 and nc_ctx % S5_SCAN_TILE == 0
    n_rows = bsz * (nc_lat + nc_ctx)
    gpb = S5_GROUPS_PER_BLOCK
    slab = lambda n: pl.BlockSpec((L, bsz, n, V7X_LANES), lambda j: (0, 0, 0, j))
    grp = lambda *tail: pl.BlockSpec((None, gpb) + tail, lambda j: (layer, j) + (0,) * len(tail))
    return pl.pallas_call(
        functools.partial(_s5_kernel, bsz, nc_ctx, nc_lat),
        grid=(SSM_GROUPS // gpb,),
        in_specs=[slab(nc_lat), slab(nc_ctx), grp(2, L + 1, 2 * SSM_STATE), grp(4, SSM_GROUP, 2 * SSM_STATE),
                  grp(SSM_GROUP, S5_TILE), grp(4, 2, 2 * SSM_STATE), grp(2, S5_SCAN_TILE, 2 * SSM_STATE)],
        out_specs=[slab(nc_lat), slab(nc_ctx)],
        out_shape=[jax.ShapeDtypeStruct((L, bsz, nc_lat, SSM_WIDTH), BF16),
                   jax.ShapeDtypeStruct((L, bsz, nc_ctx, SSM_WIDTH), BF16)],
        scratch_shapes=[pltpu.VMEM((n_rows, L * V7X_LANES), BF16)] * 2
                       + [pltpu.VMEM((S5_GROUPS_PER_PASS, n_rows, S5_TILE), F32)] * 4
                       + [pltpu.VMEM((n_rows, gpb * S5_TILE), BF16),
                          pltpu.VMEM((gpb * S5_TILE, L * V7X_LANES), BF16)],
        compiler_params=_cparams(1, 56 * 1024 * 1024),
        name="s5",
    )(u_lat, u_ctx, pw_tab, bc_tab, d_rows, apow, coef)


def _rope_tables(t_len):
    nf = NA_HEAD_DIM // 4
    inv = ROPE_BASE ** (-jnp.arange(nf, dtype=F32) / nf)
    pos = jnp.arange(t_len, dtype=jnp.int32)
    ang_r = (pos // GRID_W).astype(F32)[:, None] * inv
    ang_c = (pos % GRID_W).astype(F32)[:, None] * inv
    cos = jnp.concatenate([jnp.cos(ang_r)] * 2 + [jnp.cos(ang_c)] * 2, axis=-1)
    sin = jnp.concatenate([-jnp.sin(ang_r), jnp.sin(ang_r), -jnp.sin(ang_c), jnp.sin(ang_c)], axis=-1)
    reps = V7X_LANES // NA_HEAD_DIM
    return jnp.tile(cos, (1, reps)), jnp.tile(sin, (1, reps))


def _rpb_rows(rpb):
    depth, n_h, n_r, n_c = rpb.shape
    padded = jnp.pad(rpb.astype(F32), ((0, 0), (0, 0), (0, 2 * WIN_R - n_r), (0, V7X_LANES - n_c)))
    return padded.reshape(depth, HEAD_PAIRS, n_h // HEAD_PAIRS, 2 * WIN_R, V7X_LANES)


def _build_bias(rpb_ref, bias_scr):
    kc = min(WIN_C, GRID_W)
    w = lax.broadcasted_iota(jnp.int32, (GRID_W, V7X_LANES), 0)
    lane = lax.broadcasted_iota(jnp.int32, (GRID_W, V7X_LANES), 1)
    kcol = lane % GRID_W
    c0 = jnp.clip(w - kc // 2, 0, GRID_W - kc)
    valid = (kcol >= c0) & (kcol < c0 + kc)
    for h in range(rpb_ref.shape[0]):
        lo, hi = [], []
        for ro in range(2 * WIN_R - 1):
            row = jnp.broadcast_to(rpb_ref[h, ro:ro + 1, :], (GRID_W, V7X_LANES))
            lo.append(pltpu.roll(row, V7X_LANES - (WIN_C - 1), 1, stride=1, stride_axis=0))
            hi.append(pltpu.roll(row, V7X_LANES - (WIN_C - 1) + GRID_W, 1, stride=1, stride_axis=0))
        for ro in range(2 * WIN_R - 2):
            both = jnp.where(lane < GRID_W, lo[ro], hi[ro + 1])
            bias_scr[ro, h * GRID_W:(h + 1) * GRID_W, :] = jnp.where(valid, both, MASK_VALUE)


def _rope(x, cos, sin_signed):
    nf = NA_HEAD_DIM // 4
    lane = lax.broadcasted_iota(jnp.int32, x.shape, 1)
    first = (lane % (2 * nf)) < nf
    swapped = jnp.where(first, pltpu.roll(x, V7X_LANES - nf, 1), pltpu.roll(x, nf, 1))
    return x * cos + swapped * sin_signed


def _na_kernel(rows, kr, q_ref, k_ref, v_ref, kc_ref, vc_ref, cos_ref, sin_ref, rpb_ref, o_ref,
               krope_scr, vb_scr, kcb_scr, vcb_scr, s_scr, p_scr, bias_scr):
    step = pl.program_id(2)
    dh = NA_HEAD_DIM
    t_len = k_ref.shape[0]
    rows_per_step = q_ref.shape[0] // GRID_W
    n_heads = V7X_LANES // dh

    @pl.when(step == 0)
    def _():
        chunk = 512
        for s in range(t_len // chunk):
            sl = pl.ds(s * chunk, chunk)
            krope_scr[sl, :] = _rope(k_ref[sl, :], cos_ref[sl, :], sin_ref[sl, :]).astype(BF16)
            vb_scr[sl, :] = v_ref[sl, :].astype(BF16)
        kcb_scr[...] = kc_ref[...].astype(BF16)
        vcb_scr[...] = vc_ref[...].astype(BF16)
        _build_bias(rpb_ref, bias_scr)

    lane = lax.broadcasted_iota(jnp.int32, (GRID_W, V7X_LANES), 1)
    n_loc = kr * GRID_W
    wins = []
    for i in range(rows_per_step):
        r = step * rows_per_step + i
        r0 = jnp.clip(r - kr // 2, 0, rows - kr)
        ro0 = r0 - r + (WIN_R - 1)
        win = pl.ds(pl.multiple_of(r0 * GRID_W, GRID_W), n_loc)
        wins.append(win)
        qrow = pl.ds(pl.multiple_of(r * GRID_W, GRID_W), GRID_W)
        q = _rope(q_ref[pl.ds(i * GRID_W, GRID_W), :], cos_ref[qrow, :], sin_ref[qrow, :]) * (dh ** -0.5)
        q2 = jnp.concatenate([jnp.where((lane // dh) == h, q, 0.0) for h in range(n_heads)], axis=0).astype(BF16)
        s_loc = _dot_nt(q2, krope_scr[win, :])
        for j in range(kr // 2):
            cols = slice(j * V7X_LANES, (j + 1) * V7X_LANES)
            s_scr[i, :, cols] = s_loc[:, cols] + bias_scr[ro0 + 2 * j]
        s_scr[i, :, n_loc:] = _dot_nt(q2, kcb_scr[...])
    for i in range(rows_per_step):
        s = s_scr[i]
        m = jnp.max(s, axis=-1, keepdims=True)
        p = jnp.exp(s - m)
        denom = jnp.sum(p, axis=-1, keepdims=True)
        p_scr[i] = p.astype(BF16)
        o = (jnp.dot(p_scr[i, :, :n_loc], vb_scr[wins[i], :], preferred_element_type=F32)
             + jnp.dot(p_scr[i, :, n_loc:], vcb_scr[...], preferred_element_type=F32)) / denom
        out = o[:GRID_W]
        for h in range(1, n_heads):
            out = jnp.where((lane // dh) == h, o[h * GRID_W:(h + 1) * GRID_W], out)
        o_ref[pl.ds(i * GRID_W, GRID_W), :] = out.astype(o_ref.dtype)


def _neighbourhood_attention(proj, projc, rope_tabs, rpb_rows, layer, rows_per_step=8):
    bsz, t_len, _ = proj.shape
    c_len = projc.shape[1]
    rows = t_len // GRID_W
    kr = min(WIN_R, rows)
    assert kr % 2 == 0 and rows % rows_per_step == 0
    qb, kb, vb = 0, NA_WIDTH // V7X_LANES, 2 * NA_WIDTH // V7X_LANES
    cos, sin = rope_tabs
    tq = rows_per_step * GRID_W
    seq_spec = lambda off: pl.BlockSpec((None, t_len, V7X_LANES), lambda b, hp, r: (b, 0, off + hp))
    ctx_spec = lambda off: pl.BlockSpec((None, c_len, V7X_LANES), lambda b, hp, r: (b, 0, off + hp))
    tab_spec = pl.BlockSpec((t_len, V7X_LANES), lambda b, hp, r: (0, 0))
    bias_spec = pl.BlockSpec((None, None) + rpb_rows.shape[2:], lambda b, hp, r: (layer, hp, 0, 0, 0))
    return pl.pallas_call(
        functools.partial(_na_kernel, rows, kr),
        grid=(bsz, HEAD_PAIRS, rows // rows_per_step),
        in_specs=[pl.BlockSpec((None, tq, V7X_LANES), lambda b, hp, r: (b, r, qb + hp)),
                  seq_spec(kb), seq_spec(vb), ctx_spec(kb), ctx_spec(vb), tab_spec, tab_spec, bias_spec],
        out_specs=pl.BlockSpec((None, tq, V7X_LANES), lambda b, hp, r: (b, r, hp)),
        out_shape=jax.ShapeDtypeStruct((bsz, t_len, NA_WIDTH), BF16),
        scratch_shapes=[pltpu.VMEM((t_len, V7X_LANES), BF16), pltpu.VMEM((t_len, V7X_LANES), BF16),
                        pltpu.VMEM((c_len, V7X_LANES), BF16), pltpu.VMEM((c_len, V7X_LANES), BF16),
                        pltpu.VMEM((rows_per_step, 2 * GRID_W, kr * GRID_W + c_len), F32),
                        pltpu.VMEM((rows_per_step, 2 * GRID_W, kr * GRID_W + c_len), BF16),
                        pltpu.VMEM((2 * WIN_R - 2, 2 * GRID_W, 2 * GRID_W), F32)],
        compiler_params=_cparams(3, 32 * 1024 * 1024),
        name="neighbourhood_attention",
    )(proj, proj, proj, projc, projc, cos, sin, rpb_rows)


def _ctx_attn_kernel(q_ref, k_ref, v_ref, o_ref):
    dh = NA_HEAD_DIM
    q = (q_ref[...] * (dh ** -0.5)).astype(BF16)
    k = k_ref[...].astype(BF16)
    v = v_ref[...].astype(BF16)
    for h in range(V7X_LANES // dh):
        hs = slice(h * dh, (h + 1) * dh)
        s = _dot_nt(q[:, hs], k[:, hs])
        p = jnp.exp(s - jnp.max(s, axis=-1, keepdims=True))
        o = jnp.dot(p.astype(BF16), v[:, hs], preferred_element_type=F32)
        o_ref[:, hs] = (o / jnp.sum(p, axis=-1, keepdims=True)).astype(o_ref.dtype)


def _context_attention(projc):
    bsz, c_len, _ = projc.shape
    qb, kb, vb = 0, NA_WIDTH // V7X_LANES, 2 * NA_WIDTH // V7X_LANES
    spec = lambda off: pl.BlockSpec((None, c_len, V7X_LANES), lambda b, hp: (b, 0, off + hp))
    return pl.pallas_call(
        _ctx_attn_kernel,
        grid=(bsz, HEAD_PAIRS),
        in_specs=[spec(qb), spec(kb), spec(vb)],
        out_specs=spec(0),
        out_shape=jax.ShapeDtypeStruct((bsz, c_len, NA_WIDTH), BF16),
        compiler_params=_cparams(2, 32 * 1024 * 1024),
        name="context_attention",
    )(projc, projc, projc)


def _merge_kernel(a_ref, att_ref, gs_ref, gn_ref, wv_ref, wg_ref, wn_ref, o_ref):
    a = a_ref[...]
    val = jnp.dot(a, wv_ref[...], preferred_element_type=F32)
    glu = jnp.dot(a, wg_ref[...], preferred_element_type=F32)
    br_n = jnp.dot(att_ref[...], wn_ref[...], preferred_element_type=F32)
    z = _sigmoid(gs_ref[...]) * (val * _sigmoid(glu)) + _sigmoid(gn_ref[...]) * br_n
    o_ref[...] = z.astype(o_ref.dtype)


def _merge(a, att, proj, layer, w_val, w_glu, w_na, tm=512, tn=512):
    m = a.shape[0]
    gs_off = 3 * NA_WIDTH // tn
    gn_off = gs_off + D_MODEL // tn
    act = pl.BlockSpec((tm, SSM_WIDTH), lambda j, i: (i, 0))
    wsp = pl.BlockSpec((None, SSM_WIDTH, tn), lambda j, i: (layer, 0, j))
    return pl.pallas_call(
        _merge_kernel,
        grid=(D_MODEL // tn, m // tm),
        in_specs=[act, act,
                  pl.BlockSpec((tm, tn), lambda j, i: (i, gs_off + j)),
                  pl.BlockSpec((tm, tn), lambda j, i: (i, gn_off + j)),
                  wsp, wsp, wsp],
        out_specs=pl.BlockSpec((tm, tn), lambda j, i: (i, j)),
        out_shape=jax.ShapeDtypeStruct((m, D_MODEL), BF16),
        compiler_params=_cparams(2, 40 * 1024 * 1024),
        name="merge",
    )(a, att, proj, proj, w_val, w_glu, w_na)


def _outproj_kernel(z_ref, w_ref, x_ref, gt_ref, gpost_ref, gpre_ref, sh_ref, sc_ref, x_out, h_out):
    sub = V7X_LANES
    for r0 in range(0, z_ref.shape[0], sub):
        rows = slice(r0, r0 + sub)
        out = jnp.dot(z_ref[rows, :], w_ref[...], preferred_element_type=F32)
        x1 = x_ref[rows, :] + gt_ref[...] * _rms(out, gpost_ref[...])
        x_out[rows, :] = x1
        h_out[rows, :] = (_rms(x1, gpre_ref[...]) * (1.0 + sc_ref[...]) + sh_ref[...]).astype(h_out.dtype)


def _outproj(z, w_out, layer, x, gt, g_post, g_pre, sh, sc, rows_per_batch, tm=512):
    m = z.shape[0]
    row = pl.BlockSpec((tm, D_MODEL), lambda i: (i, 0))
    vec = pl.BlockSpec((1, D_MODEL), lambda i: (0, 0))
    bvec = pl.BlockSpec((None, 1, D_MODEL), lambda i: (i * tm // rows_per_batch, 0, 0))
    return pl.pallas_call(
        _outproj_kernel,
        grid=(m // tm,),
        in_specs=[row, pl.BlockSpec((None, D_MODEL, D_MODEL), lambda i: (layer, 0, 0)), row, bvec, vec, vec, bvec,
                  bvec],
        out_specs=[row, row],
        out_shape=[jax.ShapeDtypeStruct((m, D_MODEL), F32), jax.ShapeDtypeStruct((m, D_MODEL), BF16)],
        compiler_params=_cparams(1, 48 * 1024 * 1024),
        name="out_proj",
    )(z, w_out, x, gt, g_post.reshape(1, D_MODEL), g_pre.reshape(1, D_MODEL), sh, sc)


def _mlp_kernel(emit_next, h_ref, w1_ref, w2_ref, x_ref, gt_ref, gpost_ref, *rest):
    if emit_next:
        gpre_ref, sh_ref, sc_ref, x_out, h_out, acc = rest
    else:
        x_out, acc = rest
    f = pl.program_id(1)

    @pl.when(f == 0)
    def _():
        acc[...] = jnp.zeros_like(acc)

    a = jnp.maximum(jnp.dot(h_ref[...], w1_ref[...], preferred_element_type=F32), 0.0)
    acc[...] += jnp.dot((a * a).astype(BF16), w2_ref[...], preferred_element_type=F32)

    @pl.when(f == pl.num_programs(1) - 1)
    def _():
        x2 = x_ref[...] + gt_ref[...] * _rms(acc[...], gpost_ref[...])
        x_out[...] = x2
        if emit_next:
            h_out[...] = (_rms(x2, gpre_ref[...]) * (1.0 + sc_ref[...]) + sh_ref[...]).astype(h_out.dtype)


def _mlp(h2, w1, w2, layer, x, gt, g_post, nxt, rows_per_batch, tm=512, tf=1024):
    m = h2.shape[0]
    emit_next = nxt is not None
    row = pl.BlockSpec((tm, D_MODEL), lambda i, f: (i, 0))
    vec = pl.BlockSpec((1, D_MODEL), lambda i, f: (0, 0))
    bvec = pl.BlockSpec((None, 1, D_MODEL), lambda i, f: (i * tm // rows_per_batch, 0, 0))
    in_specs = [row, pl.BlockSpec((None, D_MODEL, tf), lambda i, f: (layer, 0, f)),
                pl.BlockSpec((None, tf, D_MODEL), lambda i, f: (layer, f, 0)), row, bvec, vec]
    args = [h2, w1, w2, x, gt, g_post.reshape(1, D_MODEL)]
    out_specs = [row]
    out_shape = [jax.ShapeDtypeStruct((m, D_MODEL), F32)]
    if emit_next:
        g_pre, sh, sc = nxt
        in_specs += [vec, bvec, bvec]
        args += [g_pre.reshape(1, D_MODEL), sh, sc]
        out_specs.append(row)
        out_shape.append(jax.ShapeDtypeStruct((m, D_MODEL), BF16))
    res = pl.pallas_call(
        functools.partial(_mlp_kernel, emit_next),
        grid=(m // tm, D_FF // tf),
        in_specs=in_specs,
        out_specs=out_specs,
        out_shape=out_shape,
        scratch_shapes=[pltpu.VMEM((tm, D_MODEL), F32)],
        compiler_params=_cparams(2, 56 * 1024 * 1024),
        name="mlp",
    )(*args)
    return (res[0], res[1]) if emit_next else (res[0], None)


def kernel(x, c, ctx, c_ctx, w_mod, b_mod, g_pre_mix, g_post_mix, g_pre_mlp, g_post_mlp, w_in,
           ssm_lam_re, ssm_lam_im, ssm_log_dt, ssm_b_re, ssm_b_im, ssm_c_re, ssm_c_im, ssm_d,
           w_ssm_val, w_ssm_glu, na_rpb, w_na_proj, w_out, w_fc1, w_fc2):
    bsz, t_len, _ = x.shape
    c_len = ctx.shape[1]
    m_lat, m_ctx = bsz * t_len, bsz * c_len
    assert bsz < V7X_SUBLANES and t_len % (GRID_W * 8) == 0 and c_len % S5_CHUNK == 0

    c_rows = jnp.concatenate([c, c_ctx[None, :], jnp.zeros((V7X_SUBLANES - bsz - 1, D_MODEL), F32)], axis=0)
    mod = _modulation(c_rows, w_mod, b_mod)

    def mod_vectors(l):
        lat = [v[:, None, :] for v in jnp.split(mod[l, :bsz], N_MOD, axis=-1)]
        cvec = [jnp.broadcast_to(v[None, :, :], (bsz, 1, D_MODEL))
                for v in jnp.split(mod[l, bsz:bsz + 1], N_MOD, axis=-1)]
        return lat, cvec

    mods = [mod_vectors(l) for l in range(DEPTH)]
    rope_tabs = _rope_tables(t_len)
    rpb_rows = _rpb_rows(na_rpb)
    s5_mats = jax.vmap(_s5_matrices)(ssm_lam_re, ssm_lam_im, ssm_log_dt, ssm_b_re, ssm_b_im, ssm_c_re, ssm_c_im, ssm_d)

    xl = x.reshape(m_lat, D_MODEL)
    xc = ctx.reshape(m_ctx, D_MODEL)
    h = _norm_mod(xl, g_pre_mix[0], mods[0][0][0], mods[0][0][1], t_len)
    hc = _norm_mod(xc, g_pre_mix[0], mods[0][1][0], mods[0][1][1], c_len)

    w_in_b, w_val_b, w_glu_b = w_in.astype(BF16), w_ssm_val.astype(BF16), w_ssm_glu.astype(BF16)
    w_na_b, w_out_b = w_na_proj.astype(BF16), w_out.astype(BF16)
    w1_b, w2_b = w_fc1.astype(BF16), w_fc2.astype(BF16)

    def to_pos_major(v):
        m, w = v.shape
        return v.reshape(m // S5_CHUNK, S5_CHUNK, w).transpose(1, 0, 2).reshape(m, w)

    def from_pos_major(v):
        return v.transpose(1, 2, 0, 3).reshape(-1, v.shape[-1])

    for l in range(DEPTH):
        with_ctx_out = l < DEPTH - 1
        (sh1, sc1, gt1, sh2, sc2, gt2), (csh1, csc1, cgt1, csh2, csc2, cgt2) = mods[l]

        u_tiles = SSM_WIDTH // 1024
        proj = _matmul(h, w_in_b, l, 1024, 1024, F32, first_tile=u_tiles)
        projc = _matmul(hc, w_in_b, l, 1024, 1024, F32, first_tile=u_tiles)
        u_lat = _matmul(to_pos_major(h), w_in_b, l, 1024, 1024, BF16, n_tiles=u_tiles)
        u_ctx = _matmul(to_pos_major(hc), w_in_b, l, 1024, 1024, BF16, n_tiles=u_tiles)
        proj3 = proj.reshape(bsz, t_len, REST_WIDTH)
        projc3 = projc.reshape(bsz, c_len, REST_WIDTH)

        a_lat, a_ctx = _s5_branch(u_lat.reshape(S5_CHUNK, bsz, t_len // S5_CHUNK, SSM_WIDTH),
                                  u_ctx.reshape(S5_CHUNK, bsz, c_len // S5_CHUNK, SSM_WIDTH), s5_mats, l,
                                  bsz, t_len, c_len)

        att = _neighbourhood_attention(proj3, projc3, rope_tabs, rpb_rows, l)
        z = _merge(from_pos_major(a_lat), att.reshape(m_lat, NA_WIDTH), proj, l, w_val_b, w_glu_b, w_na_b)
        xl, h2 = _outproj(z, w_out_b, l, xl, gt1, g_post_mix[l], g_pre_mlp[l], sh2, sc2, t_len)
        nxt = (g_pre_mix[l + 1], mods[l + 1][0][0], mods[l + 1][0][1]) if with_ctx_out else None
        xl, h = _mlp(h2, w1_b, w2_b, l, xl, gt2, g_post_mlp[l], nxt, t_len)

        if with_ctx_out:
            attc = _context_attention(projc3)
            zc = _merge(from_pos_major(a_ctx), attc.reshape(m_ctx, NA_WIDTH), projc, l, w_val_b, w_glu_b, w_na_b)
            xc, h2c = _outproj(zc, w_out_b, l, xc, cgt1, g_post_mix[l], g_pre_mlp[l], csh2, csc2, c_len)
            nxtc = (g_pre_mix[l + 1], mods[l + 1][1][0], mods[l + 1][1][1])
            xc, hc = _mlp(h2c, w1_b, w2_b, l, xc, cgt2, g_post_mlp[l], nxtc, c_len)

    return xl.reshape(bsz, t_len, D_MODEL)
```

```python
import functools
import math

import numpy as np
import jax
import jax.numpy as jnp
from jax import lax
from jax.experimental import pallas as pl
from jax.experimental.pallas import tpu as pltpu

D_MODEL = 2048
DEPTH = 2
GRID_W = 64
N_MOD = 6
SSM_WIDTH = D_MODEL // 2
SSM_GROUP = 16
SSM_GROUPS = SSM_WIDTH // SSM_GROUP
SSM_STATE = 64
NA_HEADS = 16
NA_HEAD_DIM = 64
NA_WIDTH = NA_HEADS * NA_HEAD_DIM
WIN_R = 8
WIN_C = 16
ROPE_BASE = 10000.0
D_FF = 4 * D_MODEL
NORM_EPS = 1e-6
PROJ_WIDTH = SSM_WIDTH + 3 * NA_WIDTH + 2 * D_MODEL

V7X_LANES = 128
V7X_SUBLANES = 8
V7X_VMEM_BYTES = 64 * 1024 * 1024
HEAD_PAIRS = NA_HEADS * NA_HEAD_DIM // V7X_LANES
S5_CHUNK = 16
S5_TILE = S5_CHUNK * SSM_GROUP
S5_SCAN_TILE = V7X_SUBLANES
S5_GROUPS_PER_BLOCK = V7X_LANES // SSM_GROUP
S5_GROUPS_PER_PASS = 4
MASK_VALUE = -1e30

F32 = jnp.float32
BF16 = jnp.bfloat16


def _cparams(n_axes, vmem_bytes):
    assert vmem_bytes <= V7X_VMEM_BYTES
    return pltpu.CompilerParams(dimension_semantics=("arbitrary",) * n_axes, vmem_limit_bytes=vmem_bytes)


def _sigmoid(x):
    return 0.5 * jnp.tanh(0.5 * x) + 0.5


def _gelu_tanh(x):
    return 0.5 * x * (1.0 + jnp.tanh(math.sqrt(2.0 / math.pi) * (x + 0.044715 * (x * x * x))))


def _rms(x, g):
    return x * lax.rsqrt(jnp.mean(x * x, axis=-1, keepdims=True) + NORM_EPS) * g


def _mod_kernel(c_ref, w_ref, b_ref, o_ref):
    c = c_ref[...]
    s = (c * _sigmoid(c)).astype(BF16)
    o_ref[...] = jnp.dot(s, w_ref[...].astype(BF16), preferred_element_type=F32) + b_ref[...]


def _modulation(c_rows, w_mod, b_mod, tn=1024):
    n = w_mod.shape[-1]
    return pl.pallas_call(
        _mod_kernel,
        grid=(DEPTH, n // tn),
        in_specs=[pl.BlockSpec((V7X_SUBLANES, D_MODEL), lambda l, j: (0, 0)),
                  pl.BlockSpec((None, D_MODEL, tn), lambda l, j: (l, 0, j)),
                  pl.BlockSpec((None, 1, tn), lambda l, j: (l, 0, j))],
        out_specs=pl.BlockSpec((None, V7X_SUBLANES, tn), lambda l, j: (l, 0, j)),
        out_shape=jax.ShapeDtypeStruct((DEPTH, V7X_SUBLANES, n), F32),
        compiler_params=_cparams(2, 40 * 1024 * 1024),
        name="modulation",
    )(c_rows, w_mod, b_mod.reshape(DEPTH, 1, n))


def _norm_mod_kernel(x_ref, g_ref, sh_ref, sc_ref, o_ref):
    y = _rms(x_ref[...], g_ref[...])
    o_ref[...] = (y * (1.0 + sc_ref[...]) + sh_ref[...]).astype(o_ref.dtype)


def _norm_mod(x, g, sh, sc, rows_per_batch, tm=512):
    m = x.shape[0]
    bidx = lambda i: (i * tm // rows_per_batch, 0, 0)
    return pl.pallas_call(
        _norm_mod_kernel,
        grid=(m // tm,),
        in_specs=[pl.BlockSpec((tm, D_MODEL), lambda i: (i, 0)),
                  pl.BlockSpec((1, D_MODEL), lambda i: (0, 0)),
                  pl.BlockSpec((None, 1, D_MODEL), bidx),
                  pl.BlockSpec((None, 1, D_MODEL), bidx)],
        out_specs=pl.BlockSpec((tm, D_MODEL), lambda i: (i, 0)),
        out_shape=jax.ShapeDtypeStruct((m, D_MODEL), BF16),
        compiler_params=_cparams(1, 32 * 1024 * 1024),
        name="norm_mod",
    )(x, g.reshape(1, D_MODEL), sh, sc)


def _matmul_kernel(a_ref, w_ref, o_ref):
    o_ref[...] = jnp.dot(a_ref[...], w_ref[...], preferred_element_type=F32).astype(o_ref.dtype)


def _matmul(a, w, layer, tm, tn, out_dtype, first_tile=0, n_tiles=None):
    m, k = a.shape
    n_tiles = w.shape[2] // tn - first_tile if n_tiles is None else n_tiles
    return pl.pallas_call(
        _matmul_kernel,
        grid=(n_tiles, m // tm),
        in_specs=[pl.BlockSpec((tm, k), lambda j, i: (i, 0)),
                  pl.BlockSpec((None, k, tn), lambda j, i: (layer, 0, first_tile + j))],
        out_specs=pl.BlockSpec((tm, tn), lambda j, i: (i, j)),
        out_shape=jax.ShapeDtypeStruct((m, n_tiles * tn), out_dtype),
        compiler_params=_cparams(2, 40 * 1024 * 1024),
        name="in_proj",
    )(a, w)


def _qkv_kernel(rope, a_ref, w_ref, cos_ref, sin_ref, o_ref):
    j = pl.program_id(0)
    acc = jnp.dot(a_ref[...], w_ref[...], preferred_element_type=F32)
    scale = jnp.where(j == 0, NA_HEAD_DIM ** -0.5, 1.0)
    if not rope:
        o_ref[...] = (acc * scale).astype(o_ref.dtype)
        return

    @pl.when(j == 2)
    def _():
        o_ref[...] = acc.astype(o_ref.dtype)

    @pl.when(j < 2)
    def _():
        for c in range(o_ref.shape[1] // V7X_LANES):
            cols = slice(c * V7X_LANES, (c + 1) * V7X_LANES)
            o_ref[:, cols] = (_rope(acc[:, cols], cos_ref[...], sin_ref[...]) * scale).astype(o_ref.dtype)


def _qkv_proj(a, w, layer, rope_tabs, seq_len, rope, tm=1024, tn=1024):
    m, k = a.shape
    cos, sin = rope_tabs
    first = SSM_WIDTH // tn
    pos_blocks = max(seq_len // tm, 1)
    tab = pl.BlockSpec((tm, V7X_LANES), lambda j, i: (i % pos_blocks, 0))
    return pl.pallas_call(
        functools.partial(_qkv_kernel, rope),
        grid=(3 * NA_WIDTH // tn, m // tm),
        in_specs=[pl.BlockSpec((tm, k), lambda j, i: (i, 0)),
                  pl.BlockSpec((None, k, tn), lambda j, i: (layer, 0, first + j)), tab, tab],
        out_specs=pl.BlockSpec((tm, tn), lambda j, i: (i, j)),
        out_shape=jax.ShapeDtypeStruct((m, 3 * NA_WIDTH), BF16),
        compiler_params=_cparams(2, 40 * 1024 * 1024),
        name="qkv_proj",
    )(a, w, cos, sin)


def _s5_matrices(lam_re, lam_im, log_dt, b_re, b_im, c_re, c_im, d_skip):
    L = S5_CHUNK
    lam = (lam_re.astype(F32), lam_im.astype(F32))
    dt = jnp.exp(log_dt.astype(F32))[..., None]
    tau = jnp.arange(L + 1, dtype=F32)[None, None, :, None]
    mag = jnp.exp((lam[0] * dt)[:, :, None, :] * tau)
    ang = (lam[1] * dt)[:, :, None, :] * tau
    pw = (mag * jnp.cos(ang), mag * jnp.sin(ang))
    num = (pw[0][:, :, 1] - 1.0, pw[1][:, :, 1])
    den = lam[0] * lam[0] + lam[1] * lam[1]
    coef = ((num[0] * lam[0] + num[1] * lam[1]) / den, (num[1] * lam[0] - num[0] * lam[1]) / den)
    b32 = (b_re.astype(F32), b_im.astype(F32))
    b_bar = (coef[0][..., None] * b32[0] - coef[1][..., None] * b32[1],
             coef[0][..., None] * b32[1] + coef[1][..., None] * b32[0])
    both = lambda v: jnp.concatenate([v[0], v[1]], axis=-1)
    pw_tab = jnp.stack([both(pw[0]), both(pw[1])], axis=1)
    bc_tab = jnp.stack([both(jnp.swapaxes(b_bar[0], -1, -2)), both(jnp.swapaxes(b_bar[1], -1, -2)),
                        both(c_re.astype(F32)), both(c_im.astype(F32))], axis=1)
    d_rows = jnp.broadcast_to(d_skip.astype(F32).reshape(SSM_GROUPS, SSM_GROUP, 1), (SSM_GROUPS, SSM_GROUP, S5_TILE))
    return (pw_tab, bc_tab, d_rows) + _s5_tables(lam_re, lam_im, log_dt)


def _s5_tables(lam_re, lam_im, log_dt):
    n = S5_SCAN_TILE
    dt = jnp.exp(log_dt.astype(F32))[..., None]
    steps = jnp.arange(n + 1, dtype=F32)[None, None, :, None] * S5_CHUNK
    mag = jnp.exp((lam_re.astype(F32) * dt)[:, :, None, :] * steps)
    ang = (lam_im.astype(F32) * dt)[:, :, None, :] * steps
    pre, pim = mag * jnp.cos(ang), mag * jnp.sin(ang)
    both = lambda f, r: jnp.concatenate([f, r], axis=-1)
    sel = np.array([1, 2, 4, 8])
    apow = jnp.stack([both(pre[0][:, sel], pre[1][:, sel]), both(pim[0][:, sel], pim[1][:, sel])], axis=2)
    coef = jnp.stack([both(pre[0][:, :n], pre[1][:, n - 1::-1][:, :n]),
                      both(pim[0][:, :n], pim[1][:, n - 1::-1][:, :n])], axis=1)
    return apow, coef


def _dot_nt(a, b):
    return lax.dot_general(a, b, (((1,), (1,)), ((), ())), preferred_element_type=F32)


def _shift_lanes(x, s):
    w = V7X_LANES
    lo, hi = x[:, :w], x[:, w:]
    lane = lax.broadcasted_iota(jnp.int32, lo.shape, 1)
    if s == 0:
        return x
    if s > 0:
        q, r = divmod(s, w)
        if r == 0:
            out = (jnp.zeros_like(lo), lo)
        else:
            rl, rh = pltpu.roll(lo, r, 1), pltpu.roll(hi, r, 1)
            if q == 0:
                out = (jnp.where(lane >= r, rl, 0.0), jnp.where(lane >= r, rh, rl))
            else:
                out = (jnp.zeros_like(lo), jnp.where(lane >= r, rl, 0.0))
    else:
        q, r = divmod(-s, w)
        if r == 0:
            out = (hi, jnp.zeros_like(hi))
        else:
            rl, rh = pltpu.roll(lo, w - r, 1), pltpu.roll(hi, w - r, 1)
            if q == 0:
                out = (jnp.where(lane < w - r, rl, rh), jnp.where(lane < w - r, rh, 0.0))
            else:
                out = (jnp.where(lane < w - r, rh, 0.0), jnp.zeros_like(hi))
    return jnp.concatenate(out, axis=1)


def _s5_group_tables(pw, bc, d_rows):
    L = S5_CHUNK
    half = 2 * SSM_STATE
    lane = lax.broadcasted_iota(jnp.int32, (SSM_GROUP, half), 1)
    fwd = lane < SSM_STATE
    b_r, b_i, c_r, c_i = bc[0], bc[1], bc[2], bc[3]

    def powers(tau_f, tau_r):
        sel = lambda part: jnp.where(fwd, jnp.broadcast_to(pw[part, tau_f:tau_f + 1, :], (SSM_GROUP, half)),
                                     jnp.broadcast_to(pw[part, tau_r:tau_r + 1, :], (SSM_GROUP, half)))
        return sel(0), sel(1)

    def times(m_r, m_i, tau_f, tau_r):
        p_r, p_i = powers(tau_f, tau_r)
        return m_r * p_r - m_i * p_i, m_r * p_i + m_i * p_r

    b_in, c_out_t, c_lag = [], [], []
    for t in range(L):
        re, im = times(b_r, b_i, L - 1 - t, t)
        b_in.append(jnp.concatenate([re, im], axis=1).astype(BF16))
        re, im = times(c_r, c_i, t + 1, L - t)
        c_out_t.append(jnp.concatenate([re, -im], axis=1).astype(BF16))
        re, im = times(c_r, c_i, t, L - 1 - t)
        c_lag.append(jnp.concatenate([re, im], axis=1))
    c_lag = jnp.concatenate(c_lag, axis=0)
    zero = jnp.zeros_like(b_r)
    lag_kernels = lambda keep: lax.dot_general(
        jnp.concatenate([jnp.where(keep, b_r, zero), jnp.where(keep, -b_i, zero)], axis=1), c_lag,
        (((1,), (1,)), ((), ())), precision=lax.Precision.HIGHEST, preferred_element_type=F32)
    kl_f, kl_r = lag_kernels(fwd), lag_kernels(jnp.logical_not(fwd))

    lane_t = lax.broadcasted_iota(jnp.int32, (SSM_GROUP, S5_TILE), 1)
    row_t = lax.broadcasted_iota(jnp.int32, (SSM_GROUP, S5_TILE), 0)
    blocks = []
    for t in range(L):
        m_rows = (_shift_lanes(kl_f, t * SSM_GROUP) + _shift_lanes(kl_r, -(L - 1 - t) * SSM_GROUP)
                  + jnp.where(lane_t == t * SSM_GROUP + row_t, d_rows, 0.0))
        blocks.append(jnp.concatenate([m_rows.astype(BF16), b_in[t]], axis=1))
    return jnp.concatenate(blocks, axis=0), jnp.concatenate(c_out_t, axis=0)


def _s5_kernel(bsz, nc_ctx, nc_lat, *refs):
    L = S5_CHUNK
    xl, xc, pw_ref, bc_ref, d_ref, apow_ref, coef_ref, ol, oc = refs[:9]
    xcat, ucat, y_scr, s_scr, ea_scr, eb_scr, acat, esel = refs[9:]
    nb = nc_ctx + nc_lat
    n_rows = bsz * nb
    lanes = V7X_LANES
    half = 2 * SSM_STATE
    tile = S5_SCAN_TILE
    gpb, gpp = S5_GROUPS_PER_BLOCK, S5_GROUPS_PER_PASS

    @pl.when(pl.program_id(0) == 0)
    def _():
        blk = 256
        for rb in range(esel.shape[0] // blk):
            r = lax.broadcasted_iota(jnp.int32, (blk, esel.shape[1]), 0) + rb * blk
            c = lax.broadcasted_iota(jnp.int32, (blk, esel.shape[1]), 1)
            dest = ((r % S5_TILE) // SSM_GROUP) * lanes + (r // S5_TILE) * SSM_GROUP + r % SSM_GROUP
            esel[rb * blk:(rb + 1) * blk, :] = jnp.where(dest == c, 1.0, 0.0).astype(BF16)

    for t in range(L):
        cols = slice(t * lanes, (t + 1) * lanes)
        for b in range(bsz):
            xcat[b * nb:b * nb + nc_ctx, cols] = xc[t, b]
            xcat[b * nb + nc_ctx:(b + 1) * nb, cols] = xl[t, b]

    ucat[...] = _dot_nt(xcat[...], esel[...]).astype(BF16)

    lane = lax.broadcasted_iota(jnp.int32, (tile, half), 1)
    row = lax.broadcasted_iota(jnp.int32, (tile, half), 0)
    is_fwd = lane < SSM_STATE
    shifts = (1, 2, 4)
    keep_dn = {s: is_fwd & (row >= s) for s in shifts}
    keep_up = {s: jnp.logical_not(is_fwd) & (row < tile - s) for s in shifts}

    def shift(d, s):
        return jnp.where(keep_dn[s], pltpu.roll(d, s, 0), jnp.where(keep_up[s], pltpu.roll(d, tile - s, 0), 0.0))

    def cfma(a_re, a_im, x_re, x_im, y_re, y_im):
        return a_re * x_re - a_im * x_im + y_re, a_re * x_im + a_im * x_re + y_im

    for p in range(gpb // gpp):
        readouts = []
        for q in range(gpp):
            g = p * gpp + q
            w_g, c_out_t = _s5_group_tables(pw_ref[g], bc_ref[g], d_ref[g])
            readouts.append(c_out_t)
            ys = jnp.dot(ucat[:, g * S5_TILE:(g + 1) * S5_TILE], w_g, preferred_element_type=F32)
            y_scr[q] = ys[:, :S5_TILE]
            s_scr[q] = ys[:, S5_TILE:]

        bc = lambda v: jnp.broadcast_to(v, (tile, half))
        a_pow = [[(bc(apow_ref[p * gpp + q, k, 0:1, :]), bc(apow_ref[p * gpp + q, k, 1:2, :])) for k in range(4)]
                 for q in range(gpp)]
        coefs = [(coef_ref[p * gpp + q, 0], coef_ref[p * gpp + q, 1]) for q in range(gpp)]

        for b in range(bsz):
            def make_body(rev_base, base=b * nb):
                def body(i, carry):
                    rf = pl.multiple_of(base + i * tile, tile)
                    rr = pl.multiple_of(base + (rev_base - i) * tile, tile)
                    out = []
                    for q in range(gpp):
                        e_re, e_im = carry[2 * q], carry[2 * q + 1]
                        sf = s_scr[q, pl.ds(rf, tile), :]
                        sr = s_scr[q, pl.ds(rr, tile), :]
                        d_re = jnp.where(is_fwd, sf[:, :half], sr[:, :half])
                        d_im = jnp.where(is_fwd, sf[:, half:], sr[:, half:])
                        for k, s in enumerate(shifts):
                            d_re, d_im = cfma(a_pow[q][k][0], a_pow[q][k][1], shift(d_re, s), shift(d_im, s), d_re, d_im)
                        en_re, en_im = cfma(coefs[q][0], coefs[q][1], e_re, e_im, shift(d_re, 1), shift(d_im, 1))
                        enter = jnp.concatenate([en_re, en_im], axis=-1)
                        ea_scr[q, pl.ds(rf, tile), :] = enter
                        eb_scr[q, pl.ds(rr, tile), :] = enter
                        last_re = jnp.where(is_fwd, bc(d_re[tile - 1:tile]), bc(d_re[0:1]))
                        last_im = jnp.where(is_fwd, bc(d_im[tile - 1:tile]), bc(d_im[0:1]))
                        out.extend(cfma(a_pow[q][3][0], a_pow[q][3][1], e_re, e_im, last_re, last_im))
                    return tuple(out)
                return body

            zero = jnp.zeros((tile, half), F32)
            nt_ctx, nt = nc_ctx // tile, nb // tile
            carry = lax.fori_loop(0, nt_ctx, make_body(nt_ctx - 1), tuple(zero for _ in range(2 * gpp)))
            lax.fori_loop(nt_ctx, nt, make_body(nt - 1 + nt_ctx), carry)

        lane_all = lax.broadcasted_iota(jnp.int32, (n_rows, 2 * half), 1)
        fwd_all = (lane_all % half) < SSM_STATE
        for q in range(gpp):
            g = p * gpp + q
            enter = jnp.where(fwd_all, ea_scr[q], eb_scr[q]).astype(BF16)
            y = y_scr[q] + _dot_nt(enter, readouts[q])
            acat[:, g * S5_TILE:(g + 1) * S5_TILE] = _gelu_tanh(y).astype(BF16)

    for t2 in range(L // 2):
        cols = slice(t2 * 2 * lanes, (t2 + 1) * 2 * lanes)
        a_nat = jnp.dot(acat[...], esel[:, cols], preferred_element_type=F32).astype(BF16)
        for dt_ in range(2):
            t = 2 * t2 + dt_
            for b in range(bsz):
                oc[t, b] = a_nat[b * nb:b * nb + nc_ctx, dt_ * lanes:(dt_ + 1) * lanes]
                ol[t, b] = a_nat[b * nb + nc_ctx:(b + 1) * nb, dt_ * lanes:(dt_ + 1) * lanes]


def _s5_branch(u_lat, u_ctx, mats, layer, bsz, t_len, c_len):
    pw_tab, bc_tab, d_rows, apow, coef = mats
    L = S5_CHUNK
    nc_lat, nc_ctx = t_len // L, c_len // L
    assert nc_lat % S5_SCAN_TILE == 0 and nc_ctx % S5_SCAN_TILE == 0
    n_rows = bsz * (nc_lat + nc_ctx)
    gpb = S5_GROUPS_PER_BLOCK
    slab = lambda n: pl.BlockSpec((L, bsz, n, V7X_LANES), lambda j: (0, 0, 0, j))
    grp = lambda *tail: pl.BlockSpec((None, gpb) + tail, lambda j: (layer, j) + (0,) * len(tail))
    return pl.pallas_call(
        functools.partial(_s5_kernel, bsz, nc_ctx, nc_lat),
        grid=(SSM_GROUPS // gpb,),
        in_specs=[slab(nc_lat), slab(nc_ctx), grp(2, L + 1, 2 * SSM_STATE), grp(4, SSM_GROUP, 2 * SSM_STATE),
                  grp(SSM_GROUP, S5_TILE), grp(4, 2, 2 * SSM_STATE), grp(2, S5_SCAN_TILE, 2 * SSM_STATE)],
        out_specs=[slab(nc_lat), slab(nc_ctx)],
        out_shape=[jax.ShapeDtypeStruct((L, bsz, nc_lat, SSM_WIDTH), BF16),
                   jax.ShapeDtypeStruct((L, bsz, nc_ctx, SSM_WIDTH), BF16)],
        scratch_shapes=[pltpu.VMEM((n_rows, L * V7X_LANES), BF16)] * 2
                       + [pltpu.VMEM((S5_GROUPS_PER_PASS, n_rows, S5_TILE), F32)] * 4
                       + [pltpu.VMEM((n_rows, gpb * S5_TILE), BF16),
                          pltpu.VMEM((gpb * S5_TILE, L * V7X_LANES), BF16)],
        compiler_params=_cparams(1, 56 * 1024 * 1024),
        name="s5",
    )(u_lat, u_ctx, pw_tab, bc_tab, d_rows, apow, coef)


def _rope_tables(t_len):
    nf = NA_HEAD_DIM // 4
    inv = ROPE_BASE ** (-jnp.arange(nf, dtype=F32) / nf)
    pos = jnp.arange(t_len, dtype=jnp.int32)
    ang_r = (pos // GRID_W).astype(F32)[:, None] * inv
    ang_c = (pos % GRID_W).astype(F32)[:, None] * inv
    cos = jnp.concatenate([jnp.cos(ang_r)] * 2 + [jnp.cos(ang_c)] * 2, axis=-1)
    sin = jnp.concatenate([-jnp.sin(ang_r), jnp.sin(ang_r), -jnp.sin(ang_c), jnp.sin(ang_c)], axis=-1)
    reps = V7X_LANES // NA_HEAD_DIM
    return jnp.tile(cos, (1, reps)), jnp.tile(sin, (1, reps))


def _rpb_rows(rpb):
    depth, n_h, n_r, n_c = rpb.shape
    padded = jnp.pad(rpb.astype(F32), ((0, 0), (0, 0), (0, 2 * WIN_R - n_r), (0, V7X_LANES - n_c)))
    return padded.reshape(depth, HEAD_PAIRS, n_h // HEAD_PAIRS, 2 * WIN_R, V7X_LANES)


def _build_bias(rpb_ref, bias_scr):
    kc = min(WIN_C, GRID_W)
    w = lax.broadcasted_iota(jnp.int32, (GRID_W, V7X_LANES), 0)
    lane = lax.broadcasted_iota(jnp.int32, (GRID_W, V7X_LANES), 1)
    kcol = lane % GRID_W
    c0 = jnp.clip(w - kc // 2, 0, GRID_W - kc)
    valid = (kcol >= c0) & (kcol < c0 + kc)
    for h in range(rpb_ref.shape[0]):
        lo, hi = [], []
        for ro in range(2 * WIN_R - 1):
            row = jnp.broadcast_to(rpb_ref[h, ro:ro + 1, :], (GRID_W, V7X_LANES))
            lo.append(pltpu.roll(row, V7X_LANES - (WIN_C - 1), 1, stride=1, stride_axis=0))
            hi.append(pltpu.roll(row, V7X_LANES - (WIN_C - 1) + GRID_W, 1, stride=1, stride_axis=0))
        for ro in range(2 * WIN_R - 2):
            both = jnp.where(lane < GRID_W, lo[ro], hi[ro + 1])
            bias_scr[ro, h * GRID_W:(h + 1) * GRID_W, :] = jnp.where(valid, both, MASK_VALUE)


def _rope(x, cos, sin_signed):
    nf = NA_HEAD_DIM // 4
    lane = lax.broadcasted_iota(jnp.int32, x.shape, 1)
    first = (lane % (2 * nf)) < nf
    swapped = jnp.where(first, pltpu.roll(x, V7X_LANES - nf, 1), pltpu.roll(x, nf, 1))
    return x * cos + swapped * sin_signed


def _na_kernel(rows, kr, q_ref, k_ref, v_ref, kc_ref, vc_ref, rpb_ref, o_ref, s_scr, p_scr, bias_scr):
    step = pl.program_id(2)
    dh = NA_HEAD_DIM
    rows_per_step = q_ref.shape[0] // GRID_W
    n_heads = V7X_LANES // dh

    @pl.when(step == 0)
    def _():
        _build_bias(rpb_ref, bias_scr)

    lane = lax.broadcasted_iota(jnp.int32, (GRID_W, V7X_LANES), 1)
    n_loc = kr * GRID_W
    wins = []
    for i in range(rows_per_step):
        r = step * rows_per_step + i
        r0 = jnp.clip(r - kr // 2, 0, rows - kr)
        ro0 = r0 - r + (WIN_R - 1)
        win = pl.ds(pl.multiple_of(r0 * GRID_W, GRID_W), n_loc)
        wins.append(win)
        q = q_ref[pl.ds(i * GRID_W, GRID_W), :]
        q2 = jnp.concatenate([jnp.where((lane // dh) == h, q, jnp.zeros((), q.dtype)) for h in range(n_heads)], axis=0)
        s_loc = _dot_nt(q2, k_ref[win, :])
        for j in range(kr // 2):
            cols = slice(j * V7X_LANES, (j + 1) * V7X_LANES)
            s_scr[i, :, cols] = s_loc[:, cols] + bias_scr[ro0 + 2 * j]
        s_scr[i, :, n_loc:] = _dot_nt(q2, kc_ref[...])
    for i in range(rows_per_step):
        s = s_scr[i]
        m = jnp.max(s, axis=-1, keepdims=True)
        p = jnp.exp(s - m)
        denom = jnp.sum(p, axis=-1, keepdims=True)
        p_scr[i] = p.astype(BF16)
        o = (jnp.dot(p_scr[i, :, :n_loc], v_ref[wins[i], :], preferred_element_type=F32)
             + jnp.dot(p_scr[i, :, n_loc:], vc_ref[...], preferred_element_type=F32)) / denom
        out = o[:GRID_W]
        for h in range(1, n_heads):
            out = jnp.where((lane // dh) == h, o[h * GRID_W:(h + 1) * GRID_W], out)
        o_ref[pl.ds(i * GRID_W, GRID_W), :] = out.astype(o_ref.dtype)


def _neighbourhood_attention(qkv, qkvc, rpb_rows, layer, rows_per_step=8):
    bsz, t_len, _ = qkv.shape
    c_len = qkvc.shape[1]
    rows = t_len // GRID_W
    kr = min(WIN_R, rows)
    assert kr % 2 == 0 and rows % rows_per_step == 0
    qb, kb, vb = 0, NA_WIDTH // V7X_LANES, 2 * NA_WIDTH // V7X_LANES
    tq = rows_per_step * GRID_W
    seq_spec = lambda off: pl.BlockSpec((None, t_len, V7X_LANES), lambda b, hp, r: (b, 0, off + hp))
    ctx_spec = lambda off: pl.BlockSpec((None, c_len, V7X_LANES), lambda b, hp, r: (b, 0, off + hp))
    bias_spec = pl.BlockSpec((None, None) + rpb_rows.shape[2:], lambda b, hp, r: (layer, hp, 0, 0, 0))
    return pl.pallas_call(
        functools.partial(_na_kernel, rows, kr),
        grid=(bsz, HEAD_PAIRS, rows // rows_per_step),
        in_specs=[pl.BlockSpec((None, tq, V7X_LANES), lambda b, hp, r: (b, r, qb + hp)),
                  seq_spec(kb), seq_spec(vb), ctx_spec(kb), ctx_spec(vb), bias_spec],
        out_specs=pl.BlockSpec((None, tq, V7X_LANES), lambda b, hp, r: (b, r, hp)),
        out_shape=jax.ShapeDtypeStruct((bsz, t_len, NA_WIDTH), BF16),
        scratch_shapes=[pltpu.VMEM((rows_per_step, 2 * GRID_W, kr * GRID_W + c_len), F32),
                        pltpu.VMEM((rows_per_step, 2 * GRID_W, kr * GRID_W + c_len), BF16),
                        pltpu.VMEM((2 * WIN_R - 2, 2 * GRID_W, 2 * GRID_W), F32)],
        compiler_params=_cparams(3, 32 * 1024 * 1024),
        name="neighbourhood_attention",
    )(qkv, qkv, qkv, qkvc, qkvc, rpb_rows)


def _ctx_attn_kernel(q_ref, k_ref, v_ref, o_ref):
    dh = NA_HEAD_DIM
    q, k, v = q_ref[...], k_ref[...], v_ref[...]
    for h in range(V7X_LANES // dh):
        hs = slice(h * dh, (h + 1) * dh)
        s = _dot_nt(q[:, hs], k[:, hs])
        p = jnp.exp(s - jnp.max(s, axis=-1, keepdims=True))
        o = jnp.dot(p.astype(BF16), v[:, hs], preferred_element_type=F32)
        o_ref[:, hs] = (o / jnp.sum(p, axis=-1, keepdims=True)).astype(o_ref.dtype)


def _context_attention(qkvc):
    bsz, c_len, _ = qkvc.shape
    qb, kb, vb = 0, NA_WIDTH // V7X_LANES, 2 * NA_WIDTH // V7X_LANES
    spec = lambda off: pl.BlockSpec((None, c_len, V7X_LANES), lambda b, hp: (b, 0, off + hp))
    return pl.pallas_call(
        _ctx_attn_kernel,
        grid=(bsz, HEAD_PAIRS),
        in_specs=[spec(qb), spec(kb), spec(vb)],
        out_specs=spec(0),
        out_shape=jax.ShapeDtypeStruct((bsz, c_len, NA_WIDTH), BF16),
        compiler_params=_cparams(2, 32 * 1024 * 1024),
        name="context_attention",
    )(qkvc, qkvc, qkvc)


def _merge_kernel(a_ref, att_ref, gs_ref, gn_ref, wv_ref, wg_ref, wn_ref, o_ref):
    a = a_ref[...]
    val = jnp.dot(a, wv_ref[...], preferred_element_type=F32)
    glu = jnp.dot(a, wg_ref[...], preferred_element_type=F32)
    br_n = jnp.dot(att_ref[...], wn_ref[...], preferred_element_type=F32)
    z = _sigmoid(gs_ref[...]) * (val * _sigmoid(glu)) + _sigmoid(gn_ref[...]) * br_n
    o_ref[...] = z.astype(o_ref.dtype)


def _merge(a, att, gates, layer, w_val, w_glu, w_na, tm=512, tn=512):
    m = a.shape[0]
    gs_off = 0
    gn_off = gs_off + D_MODEL // tn
    act = pl.BlockSpec((tm, SSM_WIDTH), lambda j, i: (i, 0))
    wsp = pl.BlockSpec((None, SSM_WIDTH, tn), lambda j, i: (layer, 0, j))
    return pl.pallas_call(
        _merge_kernel,
        grid=(D_MODEL // tn, m // tm),
        in_specs=[act, act,
                  pl.BlockSpec((tm, tn), lambda j, i: (i, gs_off + j)),
                  pl.BlockSpec((tm, tn), lambda j, i: (i, gn_off + j)),
                  wsp, wsp, wsp],
        out_specs=pl.BlockSpec((tm, tn), lambda j, i: (i, j)),
        out_shape=jax.ShapeDtypeStruct((m, D_MODEL), BF16),
        compiler_params=_cparams(2, 40 * 1024 * 1024),
        name="merge",
    )(a, att, gates, gates, w_val, w_glu, w_na)


def _outproj_kernel(z_ref, w_ref, x_ref, gt_ref, gpost_ref, gpre_ref, sh_ref, sc_ref, x_out, h_out):
    sub = V7X_LANES
    for r0 in range(0, z_ref.shape[0], sub):
        rows = slice(r0, r0 + sub)
        out = jnp.dot(z_ref[rows, :], w_ref[...], preferred_element_type=F32)
        x1 = x_ref[rows, :] + gt_ref[...] * _rms(out, gpost_ref[...])
        x_out[rows, :] = x1
        h_out[rows, :] = (_rms(x1, gpre_ref[...]) * (1.0 + sc_ref[...]) + sh_ref[...]).astype(h_out.dtype)


def _outproj(z, w_out, layer, x, gt, g_post, g_pre, sh, sc, rows_per_batch, tm=512):
    m = z.shape[0]
    row = pl.BlockSpec((tm, D_MODEL), lambda i: (i, 0))
    vec = pl.BlockSpec((1, D_MODEL), lambda i: (0, 0))
    bvec = pl.BlockSpec((None, 1, D_MODEL), lambda i: (i * tm // rows_per_batch, 0, 0))
    return pl.pallas_call(
        _outproj_kernel,
        grid=(m // tm,),
        in_specs=[row, pl.BlockSpec((None, D_MODEL, D_MODEL), lambda i: (layer, 0, 0)), row, bvec, vec, vec, bvec,
                  bvec],
        out_specs=[row, row],
        out_shape=[jax.ShapeDtypeStruct((m, D_MODEL), F32), jax.ShapeDtypeStruct((m, D_MODEL), BF16)],
        compiler_params=_cparams(1, 48 * 1024 * 1024),
        name="out_proj",
    )(z, w_out, x, gt, g_post.reshape(1, D_MODEL), g_pre.reshape(1, D_MODEL), sh, sc)


def _mlp_kernel(emit_next, h_ref, w1_ref, w2_ref, x_ref, gt_ref, gpost_ref, *rest):
    if emit_next:
        gpre_ref, sh_ref, sc_ref, x_out, h_out, acc = rest
    else:
        x_out, acc = rest
    f = pl.program_id(1)

    @pl.when(f == 0)
    def _():
        acc[...] = jnp.zeros_like(acc)

    a = jnp.maximum(jnp.dot(h_ref[...], w1_ref[...], preferred_element_type=F32), 0.0)
    acc[...] += jnp.dot((a * a).astype(BF16), w2_ref[...], preferred_element_type=F32)

    @pl.when(f == pl.num_programs(1) - 1)
    def _():
        x2 = x_ref[...] + gt_ref[...] * _rms(acc[...], gpost_ref[...])
        x_out[...] = x2
        if emit_next:
            h_out[...] = (_rms(x2, gpre_ref[...]) * (1.0 + sc_ref[...]) + sh_ref[...]).astype(h_out.dtype)


def _mlp(h2, w1, w2, layer, x, gt, g_post, nxt, rows_per_batch, tm=512, tf=1024):
    m = h2.shape[0]
    emit_next = nxt is not None
    row = pl.BlockSpec((tm, D_MODEL), lambda i, f: (i, 0))
    vec = pl.BlockSpec((1, D_MODEL), lambda i, f: (0, 0))
    bvec = pl.BlockSpec((None, 1, D_MODEL), lambda i, f: (i * tm // rows_per_batch, 0, 0))
    in_specs = [row, pl.BlockSpec((None, D_MODEL, tf), lambda i, f: (layer, 0, f)),
                pl.BlockSpec((None, tf, D_MODEL), lambda i, f: (layer, f, 0)), row, bvec, vec]
    args = [h2, w1, w2, x, gt, g_post.reshape(1, D_MODEL)]
    out_specs = [row]
    out_shape = [jax.ShapeDtypeStruct((m, D_MODEL), F32)]
    if emit_next:
        g_pre, sh, sc = nxt
        in_specs += [vec, bvec, bvec]
        args += [g_pre.reshape(1, D_MODEL), sh, sc]
        out_specs.append(row)
        out_shape.append(jax.ShapeDtypeStruct((m, D_MODEL), BF16))
    res = pl.pallas_call(
        functools.partial(_mlp_kernel, emit_next),
        grid=(m // tm, D_FF // tf),
        in_specs=in_specs,
        out_specs=out_specs,
        out_shape=out_shape,
        scratch_shapes=[pltpu.VMEM((tm, D_MODEL), F32)],
        compiler_params=_cparams(2, 56 * 1024 * 1024),
        name="mlp",
    )(*args)
    return (res[0], res[1]) if emit_next else (res[0], None)


def kernel(x, c, ctx, c_ctx, w_mod, b_mod, g_pre_mix, g_post_mix, g_pre_mlp, g_post_mlp, w_in,
           ssm_lam_re, ssm_lam_im, ssm_log_dt, ssm_b_re, ssm_b_im, ssm_c_re, ssm_c_im, ssm_d,
           w_ssm_val, w_ssm_glu, na_rpb, w_na_proj, w_out, w_fc1, w_fc2):
    bsz, t_len, _ = x.shape
    c_len = ctx.shape[1]
    m_lat, m_ctx = bsz * t_len, bsz * c_len
    assert bsz < V7X_SUBLANES and t_len % (GRID_W * 8) == 0 and c_len % S5_CHUNK == 0

    c_rows = jnp.concatenate([c, c_ctx[None, :], jnp.zeros((V7X_SUBLANES - bsz - 1, D_MODEL), F32)], axis=0)
    mod = _modulation(c_rows, w_mod, b_mod)

    def mod_vectors(l):
        lat = [v[:, None, :] for v in jnp.split(mod[l, :bsz], N_MOD, axis=-1)]
        cvec = [jnp.broadcast_to(v[None, :, :], (bsz, 1, D_MODEL))
                for v in jnp.split(mod[l, bsz:bsz + 1], N_MOD, axis=-1)]
        return lat, cvec

    mods = [mod_vectors(l) for l in range(DEPTH)]
    rope_tabs = _rope_tables(t_len)
    rpb_rows = _rpb_rows(na_rpb)
    s5_mats = jax.vmap(_s5_matrices)(ssm_lam_re, ssm_lam_im, ssm_log_dt, ssm_b_re, ssm_b_im, ssm_c_re, ssm_c_im, ssm_d)

    xl = x.reshape(m_lat, D_MODEL)
    xc = ctx.reshape(m_ctx, D_MODEL)
    h = _norm_mod(xl, g_pre_mix[0], mods[0][0][0], mods[0][0][1], t_len)
    hc = _norm_mod(xc, g_pre_mix[0], mods[0][1][0], mods[0][1][1], c_len)

    w_in_b, w_val_b, w_glu_b = w_in.astype(BF16), w_ssm_val.astype(BF16), w_ssm_glu.astype(BF16)
    w_na_b, w_out_b = w_na_proj.astype(BF16), w_out.astype(BF16)
    w1_b, w2_b = w_fc1.astype(BF16), w_fc2.astype(BF16)

    def to_pos_major(v):
        m, w = v.shape
        return v.reshape(m // S5_CHUNK, S5_CHUNK, w).transpose(1, 0, 2).reshape(m, w)

    def from_pos_major(v):
        return v.transpose(1, 2, 0, 3).reshape(-1, v.shape[-1])

    for l in range(DEPTH):
        with_ctx_out = l < DEPTH - 1
        (sh1, sc1, gt1, sh2, sc2, gt2), (csh1, csc1, cgt1, csh2, csc2, cgt2) = mods[l]

        u_tiles, gate_tile = SSM_WIDTH // 1024, (SSM_WIDTH + 3 * NA_WIDTH) // 1024
        qkv = _qkv_proj(h, w_in_b, l, rope_tabs, t_len, rope=True)
        qkvc = _qkv_proj(hc, w_in_b, l, rope_tabs, c_len, rope=False)
        gates = _matmul(h, w_in_b, l, 1024, 1024, F32, first_tile=gate_tile)
        u_lat = _matmul(to_pos_major(h), w_in_b, l, 1024, 1024, BF16, n_tiles=u_tiles)
        u_ctx = _matmul(to_pos_major(hc), w_in_b, l, 1024, 1024, BF16, n_tiles=u_tiles)
        qkv3 = qkv.reshape(bsz, t_len, 3 * NA_WIDTH)
        qkvc3 = qkvc.reshape(bsz, c_len, 3 * NA_WIDTH)

        a_lat, a_ctx = _s5_branch(u_lat.reshape(S5_CHUNK, bsz, t_len // S5_CHUNK, SSM_WIDTH),
                                  u_ctx.reshape(S5_CHUNK, bsz, c_len // S5_CHUNK, SSM_WIDTH), s5_mats, l,
                                  bsz, t_len, c_len)

        att = _neighbourhood_attention(qkv3, qkvc3, rpb_rows, l)
        z = _merge(from_pos_major(a_lat), att.reshape(m_lat, NA_WIDTH), gates, l, w_val_b, w_glu_b, w_na_b)
        xl, h2 = _outproj(z, w_out_b, l, xl, gt1, g_post_mix[l], g_pre_mlp[l], sh2, sc2, t_len)
        nxt = (g_pre_mix[l + 1], mods[l + 1][0][0], mods[l + 1][0][1]) if with_ctx_out else None
        xl, h = _mlp(h2, w1_b, w2_b, l, xl, gt2, g_post_mlp[l], nxt, t_len)

        if with_ctx_out:
            attc = _context_attention(qkvc3)
            gates_c = _matmul(hc, w_in_b, l, 1024, 1024, F32, first_tile=gate_tile)
            zc = _merge(from_pos_major(a_ctx), attc.reshape(m_ctx, NA_WIDTH), gates_c, l, w_val_b, w_glu_b, w_na_b)
            xc, h2c = _outproj(zc, w_out_b, l, xc, cgt1, g_post_mix[l], g_pre_mlp[l], csh2, csc2, c_len)
            nxtc = (g_pre_mix[l + 1], mods[l + 1][1][0], mods[l + 1][1][1])
            xc, hc = _mlp(h2c, w1_b, w2_b, l, xc, cgt2, g_post_mlp[l], nxtc, c_len)

    return xl.reshape(bsz, t_len, D_MODEL)
```

```python
import functools
import math

import numpy as np
import jax
import jax.numpy as jnp
from jax import lax
from jax.experimental import pallas as pl
from jax.experimental.pallas import tpu as pltpu

D_MODEL = 2048
DEPTH = 2
GRID_W = 64
N_MOD = 6
SSM_WIDTH = D_MODEL // 2
SSM_GROUP = 16
SSM_GROUPS = SSM_WIDTH // SSM_GROUP
SSM_STATE = 64
NA_HEADS = 16
NA_HEAD_DIM = 64
NA_WIDTH = NA_HEADS * NA_HEAD_DIM
WIN_R = 8
WIN_C = 16
ROPE_BASE = 10000.0
D_FF = 4 * D_MODEL
NORM_EPS = 1e-6
PROJ_WIDTH = SSM_WIDTH + 3 * NA_WIDTH + 2 * D_MODEL

V7X_LANES = 128
V7X_SUBLANES = 8
V7X_VMEM_BYTES = 64 * 1024 * 1024
HEAD_PAIRS = NA_HEADS * NA_HEAD_DIM // V7X_LANES
S5_CHUNK = 16
S5_TILE = S5_CHUNK * SSM_GROUP
S5_SCAN_TILE = V7X_SUBLANES
S5_GROUPS_PER_BLOCK = V7X_LANES // SSM_GROUP
S5_GROUPS_PER_PASS = 4
MASK_VALUE = -1e30

F32 = jnp.float32
BF16 = jnp.bfloat16


def _cparams(n_axes, vmem_bytes):
    assert vmem_bytes <= V7X_VMEM_BYTES
    return pltpu.CompilerParams(dimension_semantics=("arbitrary",) * n_axes, vmem_limit_bytes=vmem_bytes)


def _sigmoid(x):
    return 0.5 * jnp.tanh(0.5 * x) + 0.5


def _gelu_tanh(x):
    return 0.5 * x * (1.0 + jnp.tanh(math.sqrt(2.0 / math.pi) * (x + 0.044715 * (x * x * x))))


def _rms(x, g):
    return x * lax.rsqrt(jnp.mean(x * x, axis=-1, keepdims=True) + NORM_EPS) * g


def _mod_kernel(c_ref, w_ref, b_ref, o_ref):
    c = c_ref[...]
    s = (c * _sigmoid(c)).astype(BF16)
    o_ref[...] = jnp.dot(s, w_ref[...].astype(BF16), preferred_element_type=F32) + b_ref[...]


def _modulation(c_rows, w_mod, b_mod, tn=1024):
    n = w_mod.shape[-1]
    return pl.pallas_call(
        _mod_kernel,
        grid=(DEPTH, n // tn),
        in_specs=[pl.BlockSpec((V7X_SUBLANES, D_MODEL), lambda l, j: (0, 0)),
                  pl.BlockSpec((None, D_MODEL, tn), lambda l, j: (l, 0, j)),
                  pl.BlockSpec((None, 1, tn), lambda l, j: (l, 0, j))],
        out_specs=pl.BlockSpec((None, V7X_SUBLANES, tn), lambda l, j: (l, 0, j)),
        out_shape=jax.ShapeDtypeStruct((DEPTH, V7X_SUBLANES, n), F32),
        compiler_params=_cparams(2, 40 * 1024 * 1024),
        name="modulation",
    )(c_rows, w_mod, b_mod.reshape(DEPTH, 1, n))


def _norm_mod_kernel(x_ref, g_ref, sh_ref, sc_ref, o_ref):
    y = _rms(x_ref[...], g_ref[...])
    o_ref[...] = (y * (1.0 + sc_ref[...]) + sh_ref[...]).astype(o_ref.dtype)


def _norm_mod(x, g, sh, sc, rows_per_batch, tm=512):
    m = x.shape[0]
    bidx = lambda i: (i * tm // rows_per_batch, 0, 0)
    return pl.pallas_call(
        _norm_mod_kernel,
        grid=(m // tm,),
        in_specs=[pl.BlockSpec((tm, D_MODEL), lambda i: (i, 0)),
                  pl.BlockSpec((1, D_MODEL), lambda i: (0, 0)),
                  pl.BlockSpec((None, 1, D_MODEL), bidx),
                  pl.BlockSpec((None, 1, D_MODEL), bidx)],
        out_specs=pl.BlockSpec((tm, D_MODEL), lambda i: (i, 0)),
        out_shape=jax.ShapeDtypeStruct((m, D_MODEL), BF16),
        compiler_params=_cparams(1, 32 * 1024 * 1024),
        name="norm_mod",
    )(x, g.reshape(1, D_MODEL), sh, sc)


def _matmul_kernel(a_ref, w_ref, o_ref):
    o_ref[...] = jnp.dot(a_ref[...], w_ref[...], preferred_element_type=F32).astype(o_ref.dtype)


def _matmul(a, w, layer, tm, tn, out_dtype, first_tile=0, n_tiles=None):
    m, k = a.shape
    n_tiles = w.shape[2] // tn - first_tile if n_tiles is None else n_tiles
    return pl.pallas_call(
        _matmul_kernel,
        grid=(n_tiles, m // tm),
        in_specs=[pl.BlockSpec((tm, k), lambda j, i: (i, 0)),
                  pl.BlockSpec((None, k, tn), lambda j, i: (layer, 0, first_tile + j))],
        out_specs=pl.BlockSpec((tm, tn), lambda j, i: (i, j)),
        out_shape=jax.ShapeDtypeStruct((m, n_tiles * tn), out_dtype),
        compiler_params=_cparams(2, 40 * 1024 * 1024),
        name="in_proj",
    )(a, w)


def _qkv_kernel(rope, a_ref, w_ref, cos_ref, sin_ref, o_ref):
    j = pl.program_id(0)
    scale = jnp.where(j == 0, NA_HEAD_DIM ** -0.5, 1.0)
    sub = 2 * V7X_LANES

    def body(epilogue):
        for r0 in range(0, a_ref.shape[0], sub):
            rows = slice(r0, r0 + sub)
            epilogue(rows, jnp.dot(a_ref[rows, :], w_ref[...], preferred_element_type=F32))

    def plain(rows, acc):
        o_ref[rows, :] = (acc * scale).astype(o_ref.dtype)

    def rotated(rows, acc):
        for c in range(o_ref.shape[1] // V7X_LANES):
            cols = slice(c * V7X_LANES, (c + 1) * V7X_LANES)
            o_ref[rows, cols] = (_rope(acc[:, cols], cos_ref[rows, :], sin_ref[rows, :]) * scale).astype(o_ref.dtype)

    if not rope:
        body(plain)
        return
    pl.when(j == 2)(lambda: body(plain))
    pl.when(j < 2)(lambda: body(rotated))


def _qkv_proj(a, w, layer, rope_tabs, seq_len, rope, tm=1024, tn=1024):
    m, k = a.shape
    cos, sin = rope_tabs
    first = SSM_WIDTH // tn
    pos_blocks = max(seq_len // tm, 1)
    tab = pl.BlockSpec((tm, V7X_LANES), lambda j, i: (i % pos_blocks, 0))
    return pl.pallas_call(
        functools.partial(_qkv_kernel, rope),
        grid=(3 * NA_WIDTH // tn, m // tm),
        in_specs=[pl.BlockSpec((tm, k), lambda j, i: (i, 0)),
                  pl.BlockSpec((None, k, tn), lambda j, i: (layer, 0, first + j)), tab, tab],
        out_specs=pl.BlockSpec((tm, tn), lambda j, i: (i, j)),
        out_shape=jax.ShapeDtypeStruct((m, 3 * NA_WIDTH), BF16),
        compiler_params=_cparams(2, 40 * 1024 * 1024),
        name="qkv_proj",
    )(a, w, cos, sin)


def _s5_matrices(lam_re, lam_im, log_dt, b_re, b_im, c_re, c_im, d_skip):
    L = S5_CHUNK
    lam = (lam_re.astype(F32), lam_im.astype(F32))
    dt = jnp.exp(log_dt.astype(F32))[..., None]
    tau = jnp.arange(L + 1, dtype=F32)[None, None, :, None]
    mag = jnp.exp((lam[0] * dt)[:, :, None, :] * tau)
    ang = (lam[1] * dt)[:, :, None, :] * tau
    pw = (mag * jnp.cos(ang), mag * jnp.sin(ang))
    num = (pw[0][:, :, 1] - 1.0, pw[1][:, :, 1])
    den = lam[0] * lam[0] + lam[1] * lam[1]
    coef = ((num[0] * lam[0] + num[1] * lam[1]) / den, (num[1] * lam[0] - num[0] * lam[1]) / den)
    b32 = (b_re.astype(F32), b_im.astype(F32))
    b_bar = (coef[0][..., None] * b32[0] - coef[1][..., None] * b32[1],
             coef[0][..., None] * b32[1] + coef[1][..., None] * b32[0])
    both = lambda v: jnp.concatenate([v[0], v[1]], axis=-1)
    pw_tab = jnp.stack([both(pw[0]), both(pw[1])], axis=1)
    bc_tab = jnp.stack([both(jnp.swapaxes(b_bar[0], -1, -2)), both(jnp.swapaxes(b_bar[1], -1, -2)),
                        both(c_re.astype(F32)), both(c_im.astype(F32))], axis=1)
    d_rows = jnp.broadcast_to(d_skip.astype(F32).reshape(SSM_GROUPS, SSM_GROUP, 1), (SSM_GROUPS, SSM_GROUP, S5_TILE))
    return (pw_tab, bc_tab, d_rows) + _s5_tables(lam_re, lam_im, log_dt)


def _s5_tables(lam_re, lam_im, log_dt):
    n = S5_SCAN_TILE
    dt = jnp.exp(log_dt.astype(F32))[..., None]
    steps = jnp.arange(n + 1, dtype=F32)[None, None, :, None] * S5_CHUNK
    mag = jnp.exp((lam_re.astype(F32) * dt)[:, :, None, :] * steps)
    ang = (lam_im.astype(F32) * dt)[:, :, None, :] * steps
    pre, pim = mag * jnp.cos(ang), mag * jnp.sin(ang)
    both = lambda f, r: jnp.concatenate([f, r], axis=-1)
    sel = np.array([1, 2, 4, 8])
    apow = jnp.stack([both(pre[0][:, sel], pre[1][:, sel]), both(pim[0][:, sel], pim[1][:, sel])], axis=2)
    coef = jnp.stack([both(pre[0][:, :n], pre[1][:, n - 1::-1][:, :n]),
                      both(pim[0][:, :n], pim[1][:, n - 1::-1][:, :n])], axis=1)
    return apow, coef


def _dot_nt(a, b):
    return lax.dot_general(a, b, (((1,), (1,)), ((), ())), preferred_element_type=F32)


def _shift_lanes(x, s):
    w = V7X_LANES
    lo, hi = x[:, :w], x[:, w:]
    lane = lax.broadcasted_iota(jnp.int32, lo.shape, 1)
    if s == 0:
        return x
    if s > 0:
        q, r = divmod(s, w)
        if r == 0:
            out = (jnp.zeros_like(lo), lo)
        else:
            rl, rh = pltpu.roll(lo, r, 1), pltpu.roll(hi, r, 1)
            if q == 0:
                out = (jnp.where(lane >= r, rl, 0.0), jnp.where(lane >= r, rh, rl))
            else:
                out = (jnp.zeros_like(lo), jnp.where(lane >= r, rl, 0.0))
    else:
        q, r = divmod(-s, w)
        if r == 0:
            out = (hi, jnp.zeros_like(hi))
        else:
            rl, rh = pltpu.roll(lo, w - r, 1), pltpu.roll(hi, w - r, 1)
            if q == 0:
                out = (jnp.where(lane < w - r, rl, rh), jnp.where(lane < w - r, rh, 0.0))
            else:
                out = (jnp.where(lane < w - r, rh, 0.0), jnp.zeros_like(hi))
    return jnp.concatenate(out, axis=1)


def _s5_group_tables(pw, bc, d_rows):
    L = S5_CHUNK
    half = 2 * SSM_STATE
    lane = lax.broadcasted_iota(jnp.int32, (SSM_GROUP, half), 1)
    fwd = lane < SSM_STATE
    b_r, b_i, c_r, c_i = bc[0], bc[1], bc[2], bc[3]

    def powers(tau_f, tau_r):
        sel = lambda part: jnp.where(fwd, jnp.broadcast_to(pw[part, tau_f:tau_f + 1, :], (SSM_GROUP, half)),
                                     jnp.broadcast_to(pw[part, tau_r:tau_r + 1, :], (SSM_GROUP, half)))
        return sel(0), sel(1)

    def times(m_r, m_i, tau_f, tau_r):
        p_r, p_i = powers(tau_f, tau_r)
        return m_r * p_r - m_i * p_i, m_r * p_i + m_i * p_r

    b_in, c_out_t, c_lag = [], [], []
    for t in range(L):
        re, im = times(b_r, b_i, L - 1 - t, t)
        b_in.append(jnp.concatenate([re, im], axis=1).astype(BF16))
        re, im = times(c_r, c_i, t + 1, L - t)
        c_out_t.append(jnp.concatenate([re, -im], axis=1).astype(BF16))
        re, im = times(c_r, c_i, t, L - 1 - t)
        c_lag.append(jnp.concatenate([re, im], axis=1))
    c_lag = jnp.concatenate(c_lag, axis=0)
    zero = jnp.zeros_like(b_r)
    lag_kernels = lambda keep: lax.dot_general(
        jnp.concatenate([jnp.where(keep, b_r, zero), jnp.where(keep, -b_i, zero)], axis=1), c_lag,
        (((1,), (1,)), ((), ())), precision=lax.Precision.HIGHEST, preferred_element_type=F32)
    kl_f, kl_r = lag_kernels(fwd), lag_kernels(jnp.logical_not(fwd))

    lane_t = lax.broadcasted_iota(jnp.int32, (SSM_GROUP, S5_TILE), 1)
    row_t = lax.broadcasted_iota(jnp.int32, (SSM_GROUP, S5_TILE), 0)
    blocks = []
    for t in range(L):
        m_rows = (_shift_lanes(kl_f, t * SSM_GROUP) + _shift_lanes(kl_r, -(L - 1 - t) * SSM_GROUP)
                  + jnp.where(lane_t == t * SSM_GROUP + row_t, d_rows, 0.0))
        blocks.append(jnp.concatenate([m_rows.astype(BF16), b_in[t]], axis=1))
    return jnp.concatenate(blocks, axis=0), jnp.concatenate(c_out_t, axis=0)


def _s5_kernel(bsz, nc_ctx, nc_lat, *refs):
    L = S5_CHUNK
    xl, xc, pw_ref, bc_ref, d_ref, apow_ref, coef_ref, ol, oc = refs[:9]
    xcat, ucat, y_scr, s_scr, ea_scr, eb_scr, acat, esel = refs[9:]
    nb = nc_ctx + nc_lat
    n_rows = bsz * nb
    lanes = V7X_LANES
    half = 2 * SSM_STATE
    tile = S5_SCAN_TILE
    gpb, gpp = S5_GROUPS_PER_BLOCK, S5_GROUPS_PER_PASS

    @pl.when(pl.program_id(0) == 0)
    def _():
        blk = 256
        for rb in range(esel.shape[0] // blk):
            r = lax.broadcasted_iota(jnp.int32, (blk, esel.shape[1]), 0) + rb * blk
            c = lax.broadcasted_iota(jnp.int32, (blk, esel.shape[1]), 1)
            dest = ((r % S5_TILE) // SSM_GROUP) * lanes + (r // S5_TILE) * SSM_GROUP + r % SSM_GROUP
            esel[rb * blk:(rb + 1) * blk, :] = jnp.where(dest == c, 1.0, 0.0).astype(BF16)

    for t in range(L):
        cols = slice(t * lanes, (t + 1) * lanes)
        for b in range(bsz):
            xcat[b * nb:b * nb + nc_ctx, cols] = xc[t, b]
            xcat[b * nb + nc_ctx:(b + 1) * nb, cols] = xl[t, b]

    ucat[...] = _dot_nt(xcat[...], esel[...]).astype(BF16)

    lane = lax.broadcasted_iota(jnp.int32, (tile, half), 1)
    row = lax.broadcasted_iota(jnp.int32, (tile, half), 0)
    is_fwd = lane < SSM_STATE
    shifts = (1, 2, 4)
    keep_dn = {s: is_fwd & (row >= s) for s in shifts}
    keep_up = {s: jnp.logical_not(is_fwd) & (row < tile - s) for s in shifts}

    def shift(d, s):
        return jnp.where(keep_dn[s], pltpu.roll(d, s, 0), jnp.where(keep_up[s], pltpu.roll(d, tile - s, 0), 0.0))

    def cfma(a_re, a_im, x_re, x_im, y_re, y_im):
        return a_re * x_re - a_im * x_im + y_re, a_re * x_im + a_im * x_re + y_im

    for p in range(gpb // gpp):
        readouts = []
        for q in range(gpp):
            g = p * gpp + q
            w_g, c_out_t = _s5_group_tables(pw_ref[g], bc_ref[g], d_ref[g])
            readouts.append(c_out_t)
            ys = jnp.dot(ucat[:, g * S5_TILE:(g + 1) * S5_TILE], w_g, preferred_element_type=F32)
            y_scr[q] = ys[:, :S5_TILE]
            s_scr[q] = ys[:, S5_TILE:]

        bc = lambda v: jnp.broadcast_to(v, (tile, half))
        a_pow = [[(bc(apow_ref[p * gpp + q, k, 0:1, :]), bc(apow_ref[p * gpp + q, k, 1:2, :])) for k in range(4)]
                 for q in range(gpp)]
        coefs = [(coef_ref[p * gpp + q, 0], coef_ref[p * gpp + q, 1]) for q in range(gpp)]

        for b in range(bsz):
            def make_body(rev_base, base=b * nb):
                def body(i, carry):
                    rf = pl.multiple_of(base + i * tile, tile)
                    rr = pl.multiple_of(base + (rev_base - i) * tile, tile)
                    out = []
                    for q in range(gpp):
                        e_re, e_im = carry[2 * q], carry[2 * q + 1]
                        sf = s_scr[q, pl.ds(rf, tile), :]
                        sr = s_scr[q, pl.ds(rr, tile), :]
                        d_re = jnp.where(is_fwd, sf[:, :half], sr[:, :half])
                        d_im = jnp.where(is_fwd, sf[:, half:], sr[:, half:])
                        for k, s in enumerate(shifts):
                            d_re, d_im = cfma(a_pow[q][k][0], a_pow[q][k][1], shift(d_re, s), shift(d_im, s), d_re, d_im)
                        en_re, en_im = cfma(coefs[q][0], coefs[q][1], e_re, e_im, shift(d_re, 1), shift(d_im, 1))
                        enter = jnp.concatenate([en_re, en_im], axis=-1)
                        ea_scr[q, pl.ds(rf, tile), :] = enter
                        eb_scr[q, pl.ds(rr, tile), :] = enter
                        last_re = jnp.where(is_fwd, bc(d_re[tile - 1:tile]), bc(d_re[0:1]))
                        last_im = jnp.where(is_fwd, bc(d_im[tile - 1:tile]), bc(d_im[0:1]))
                        out.extend(cfma(a_pow[q][3][0], a_pow[q][3][1], e_re, e_im, last_re, last_im))
                    return tuple(out)
                return body

            zero = jnp.zeros((tile, half), F32)
            nt_ctx, nt = nc_ctx // tile, nb // tile
            carry = lax.fori_loop(0, nt_ctx, make_body(nt_ctx - 1), tuple(zero for _ in range(2 * gpp)))
            lax.fori_loop(nt_ctx, nt, make_body(nt - 1 + nt_ctx), carry)

        lane_all = lax.broadcasted_iota(jnp.int32, (n_rows, 2 * half), 1)
        fwd_all = (lane_all % half) < SSM_STATE
        for q in range(gpp):
            g = p * gpp + q
            enter = jnp.where(fwd_all, ea_scr[q], eb_scr[q]).astype(BF16)
            y = y_scr[q] + _dot_nt(enter, readouts[q])
            acat[:, g * S5_TILE:(g + 1) * S5_TILE] = _gelu_tanh(y).astype(BF16)

    for t2 in range(L // 2):
        cols = slice(t2 * 2 * lanes, (t2 + 1) * 2 * lanes)
        a_nat = jnp.dot(acat[...], esel[:, cols], preferred_element_type=F32).astype(BF16)
        for dt_ in range(2):
            t = 2 * t2 + dt_
            for b in range(bsz):
                oc[t, b] = a_nat[b * nb:b * nb + nc_ctx, dt_ * lanes:(dt_ + 1) * lanes]
                ol[t, b] = a_nat[b * nb + nc_ctx:(b + 1) * nb, dt_ * lanes:(dt_ + 1) * lanes]


def _s5_branch(u_lat, u_ctx, mats, layer, bsz, t_len, c_len):
    pw_tab, bc_tab, d_rows, apow, coef = mats
    L = S5_CHUNK
    nc_lat, nc_ctx = t_len // L, c_len // L
    assert nc_lat % S5_SCAN_TILE == 0 and nc_ctx % S5_SCAN_TILE == 0
    n_rows = bsz * (nc_lat + nc_ctx)
    gpb = S5_GROUPS_PER_BLOCK
    slab = lambda n: pl.BlockSpec((L, bsz, n, V7X_LANES), lambda j: (0, 0, 0, j))
    grp = lambda *tail: pl.BlockSpec((None, gpb) + tail, lambda j: (layer, j) + (0,) * len(tail))
    return pl.pallas_call(
        functools.partial(_s5_kernel, bsz, nc_ctx, nc_lat),
        grid=(SSM_GROUPS // gpb,),
        in_specs=[slab(nc_lat), slab(nc_ctx), grp(2, L + 1, 2 * SSM_STATE), grp(4, SSM_GROUP, 2 * SSM_STATE),
                  grp(SSM_GROUP, S5_TILE), grp(4, 2, 2 * SSM_STATE), grp(2, S5_SCAN_TILE, 2 * SSM_STATE)],
        out_specs=[slab(nc_lat), slab(nc_ctx)],
        out_shape=[jax.ShapeDtypeStruct((L, bsz, nc_lat, SSM_WIDTH), BF16),
                   jax.ShapeDtypeStruct((L, bsz, nc_ctx, SSM_WIDTH), BF16)],
        scratch_shapes=[pltpu.VMEM((n_rows, L * V7X_LANES), BF16)] * 2
                       + [pltpu.VMEM((S5_GROUPS_PER_PASS, n_rows, S5_TILE), F32)] * 4
                       + [pltpu.VMEM((n_rows, gpb * S5_TILE), BF16),
                          pltpu.VMEM((gpb * S5_TILE, L * V7X_LANES), BF16)],
        compiler_params=_cparams(1, 56 * 1024 * 1024),
        name="s5",
    )(u_lat, u_ctx, pw_tab, bc_tab, d_rows, apow, coef)


def _rope_tables(t_len):
    nf = NA_HEAD_DIM // 4
    inv = ROPE_BASE ** (-jnp.arange(nf, dtype=F32) / nf)
    pos = jnp.arange(t_len, dtype=jnp.int32)
    ang_r = (pos // GRID_W).astype(F32)[:, None] * inv
    ang_c = (pos % GRID_W).astype(F32)[:, None] * inv
    cos = jnp.concatenate([jnp.cos(ang_r)] * 2 + [jnp.cos(ang_c)] * 2, axis=-1)
    sin = jnp.concatenate([-jnp.sin(ang_r), jnp.sin(ang_r), -jnp.sin(ang_c), jnp.sin(ang_c)], axis=-1)
    reps = V7X_LANES // NA_HEAD_DIM
    return jnp.tile(cos, (1, reps)), jnp.tile(sin, (1, reps))


def _rpb_rows(rpb):
    depth, n_h, n_r, n_c = rpb.shape
    padded = jnp.pad(rpb.astype(F32), ((0, 0), (0, 0), (0, 2 * WIN_R - n_r), (0, V7X_LANES - n_c)))
    return padded.reshape(depth, HEAD_PAIRS, n_h // HEAD_PAIRS, 2 * WIN_R, V7X_LANES)


def _build_bias(rpb_ref, bias_scr):
    kc = min(WIN_C, GRID_W)
    w = lax.broadcasted_iota(jnp.int32, (GRID_W, V7X_LANES), 0)
    lane = lax.broadcasted_iota(jnp.int32, (GRID_W, V7X_LANES), 1)
    kcol = lane % GRID_W
    c0 = jnp.clip(w - kc // 2, 0, GRID_W - kc)
    valid = (kcol >= c0) & (kcol < c0 + kc)
    for h in range(rpb_ref.shape[0]):
        lo, hi = [], []
        for ro in range(2 * WIN_R - 1):
            row = jnp.broadcast_to(rpb_ref[h, ro:ro + 1, :], (GRID_W, V7X_LANES))
            lo.append(pltpu.roll(row, V7X_LANES - (WIN_C - 1), 1, stride=1, stride_axis=0))
            hi.append(pltpu.roll(row, V7X_LANES - (WIN_C - 1) + GRID_W, 1, stride=1, stride_axis=0))
        for ro in range(2 * WIN_R - 2):
            both = jnp.where(lane < GRID_W, lo[ro], hi[ro + 1])
            bias_scr[ro, h * GRID_W:(h + 1) * GRID_W, :] = jnp.where(valid, both, MASK_VALUE)


def _rope(x, cos, sin_signed):
    nf = NA_HEAD_DIM // 4
    lane = lax.broadcasted_iota(jnp.int32, x.shape, 1)
    first = (lane % (2 * nf)) < nf
    swapped = jnp.where(first, pltpu.roll(x, V7X_LANES - nf, 1), pltpu.roll(x, nf, 1))
    return x * cos + swapped * sin_signed


def _na_kernel(rows, kr, q_ref, k_ref, v_ref, kc_ref, vc_ref, rpb_ref, o_ref, s_scr, p_scr, bias_scr):
    step = pl.program_id(2)
    dh = NA_HEAD_DIM
    rows_per_step = q_ref.shape[0] // GRID_W
    n_heads = V7X_LANES // dh

    @pl.when(step == 0)
    def _():
        _build_bias(rpb_ref, bias_scr)

    lane = lax.broadcasted_iota(jnp.int32, (GRID_W, V7X_LANES), 1)
    n_loc = kr * GRID_W
    wins = []
    for i in range(rows_per_step):
        r = step * rows_per_step + i
        r0 = jnp.clip(r - kr // 2, 0, rows - kr)
        ro0 = r0 - r + (WIN_R - 1)
        win = pl.ds(pl.multiple_of(r0 * GRID_W, GRID_W), n_loc)
        wins.append(win)
        q = q_ref[pl.ds(i * GRID_W, GRID_W), :]
        q2 = jnp.concatenate([jnp.where((lane // dh) == h, q, jnp.zeros((), q.dtype)) for h in range(n_heads)], axis=0)
        s_loc = _dot_nt(q2, k_ref[win, :])
        for j in range(kr // 2):
            cols = slice(j * V7X_LANES, (j + 1) * V7X_LANES)
            s_scr[i, :, cols] = s_loc[:, cols] + bias_scr[ro0 + 2 * j]
        s_scr[i, :, n_loc:] = _dot_nt(q2, kc_ref[...])
    for i in range(rows_per_step):
        s = s_scr[i]
        m = jnp.max(s, axis=-1, keepdims=True)
        p = jnp.exp(s - m)
        denom = jnp.sum(p, axis=-1, keepdims=True)
        p_scr[i] = p.astype(BF16)
        o = (jnp.dot(p_scr[i, :, :n_loc], v_ref[wins[i], :], preferred_element_type=F32)
             + jnp.dot(p_scr[i, :, n_loc:], vc_ref[...], preferred_element_type=F32)) / denom
        out = o[:GRID_W]
        for h in range(1, n_heads):
            out = jnp.where((lane // dh) == h, o[h * GRID_W:(h + 1) * GRID_W], out)
        o_ref[pl.ds(i * GRID_W, GRID_W), :] = out.astype(o_ref.dtype)


def _neighbourhood_attention(qkv, qkvc, rpb_rows, layer, rows_per_step=8):
    bsz, t_len, _ = qkv.shape
    c_len = qkvc.shape[1]
    rows = t_len // GRID_W
    kr = min(WIN_R, rows)
    assert kr % 2 == 0 and rows % rows_per_step == 0
    qb, kb, vb = 0, NA_WIDTH // V7X_LANES, 2 * NA_WIDTH // V7X_LANES
    tq = rows_per_step * GRID_W
    seq_spec = lambda off: pl.BlockSpec((None, t_len, V7X_LANES), lambda b, hp, r: (b, 0, off + hp))
    ctx_spec = lambda off: pl.BlockSpec((None, c_len, V7X_LANES), lambda b, hp, r: (b, 0, off + hp))
    bias_spec = pl.BlockSpec((None, None) + rpb_rows.shape[2:], lambda b, hp, r: (layer, hp, 0, 0, 0))
    return pl.pallas_call(
        functools.partial(_na_kernel, rows, kr),
        grid=(bsz, HEAD_PAIRS, rows // rows_per_step),
        in_specs=[pl.BlockSpec((None, tq, V7X_LANES), lambda b, hp, r: (b, r, qb + hp)),
                  seq_spec(kb), seq_spec(vb), ctx_spec(kb), ctx_spec(vb), bias_spec],
        out_specs=pl.BlockSpec((None, tq, V7X_LANES), lambda b, hp, r: (b, r, hp)),
        out_shape=jax.ShapeDtypeStruct((bsz, t_len, NA_WIDTH), BF16),
        scratch_shapes=[pltpu.VMEM((rows_per_step, 2 * GRID_W, kr * GRID_W + c_len), F32),
                        pltpu.VMEM((rows_per_step, 2 * GRID_W, kr * GRID_W + c_len), BF16),
                        pltpu.VMEM((2 * WIN_R - 2, 2 * GRID_W, 2 * GRID_W), F32)],
        compiler_params=_cparams(3, 32 * 1024 * 1024),
        name="neighbourhood_attention",
    )(qkv, qkv, qkv, qkvc, qkvc, rpb_rows)


def _ctx_attn_kernel(q_ref, k_ref, v_ref, o_ref):
    dh = NA_HEAD_DIM
    q, k, v = q_ref[...], k_ref[...], v_ref[...]
    for h in range(V7X_LANES // dh):
        hs = slice(h * dh, (h + 1) * dh)
        s = _dot_nt(q[:, hs], k[:, hs])
        p = jnp.exp(s - jnp.max(s, axis=-1, keepdims=True))
        o = jnp.dot(p.astype(BF16), v[:, hs], preferred_element_type=F32)
        o_ref[:, hs] = (o / jnp.sum(p, axis=-1, keepdims=True)).astype(o_ref.dtype)


def _context_attention(qkvc):
    bsz, c_len, _ = qkvc.shape
    qb, kb, vb = 0, NA_WIDTH // V7X_LANES, 2 * NA_WIDTH // V7X_LANES
    spec = lambda off: pl.BlockSpec((None, c_len, V7X_LANES), lambda b, hp: (b, 0, off + hp))
    return pl.pallas_call(
        _ctx_attn_kernel,
        grid=(bsz, HEAD_PAIRS),
        in_specs=[spec(qb), spec(kb), spec(vb)],
        out_specs=spec(0),
        out_shape=jax.ShapeDtypeStruct((bsz, c_len, NA_WIDTH), BF16),
        compiler_params=_cparams(2, 32 * 1024 * 1024),
        name="context_attention",
    )(qkvc, qkvc, qkvc)


def _merge_kernel(a_ref, att_ref, gs_ref, gn_ref, wv_ref, wg_ref, wn_ref, o_ref):
    a = a_ref[...]
    val = jnp.dot(a, wv_ref[...], preferred_element_type=F32)
    glu = jnp.dot(a, wg_ref[...], preferred_element_type=F32)
    br_n = jnp.dot(att_ref[...], wn_ref[...], preferred_element_type=F32)
    z = _sigmoid(gs_ref[...]) * (val * _sigmoid(glu)) + _sigmoid(gn_ref[...]) * br_n
    o_ref[...] = z.astype(o_ref.dtype)


def _merge(a, att, gates, layer, w_val, w_glu, w_na, tm=512, tn=512):
    m = a.shape[0]
    gs_off = 0
    gn_off = gs_off + D_MODEL // tn
    act = pl.BlockSpec((tm, SSM_WIDTH), lambda j, i: (i, 0))
    wsp = pl.BlockSpec((None, SSM_WIDTH, tn), lambda j, i: (layer, 0, j))
    return pl.pallas_call(
        _merge_kernel,
        grid=(D_MODEL // tn, m // tm),
        in_specs=[act, act,
                  pl.BlockSpec((tm, tn), lambda j, i: (i, gs_off + j)),
                  pl.BlockSpec((tm, tn), lambda j, i: (i, gn_off + j)),
                  wsp, wsp, wsp],
        out_specs=pl.BlockSpec((tm, tn), lambda j, i: (i, j)),
        out_shape=jax.ShapeDtypeStruct((m, D_MODEL), BF16),
        compiler_params=_cparams(2, 40 * 1024 * 1024),
        name="merge",
    )(a, att, gates, gates, w_val, w_glu, w_na)


def _outproj_kernel(z_ref, w_ref, x_ref, gt_ref, gpost_ref, gpre_ref, sh_ref, sc_ref, x_out, h_out):
    sub = V7X_LANES
    for r0 in range(0, z_ref.shape[0], sub):
        rows = slice(r0, r0 + sub)
        out = jnp.dot(z_ref[rows, :], w_ref[...], preferred_element_type=F32)
        x1 = x_ref[rows, :] + gt_ref[...] * _rms(out, gpost_ref[...])
        x_out[rows, :] = x1
        h_out[rows, :] = (_rms(x1, gpre_ref[...]) * (1.0 + sc_ref[...]) + sh_ref[...]).astype(h_out.dtype)


def _outproj(z, w_out, layer, x, gt, g_post, g_pre, sh, sc, rows_per_batch, tm=512):
    m = z.shape[0]
    row = pl.BlockSpec((tm, D_MODEL), lambda i: (i, 0))
    vec = pl.BlockSpec((1, D_MODEL), lambda i: (0, 0))
    bvec = pl.BlockSpec((None, 1, D_MODEL), lambda i: (i * tm // rows_per_batch, 0, 0))
    return pl.pallas_call(
        _outproj_kernel,
        grid=(m // tm,),
        in_specs=[row, pl.BlockSpec((None, D_MODEL, D_MODEL), lambda i: (layer, 0, 0)), row, bvec, vec, vec, bvec,
                  bvec],
        out_specs=[row, row],
        out_shape=[jax.ShapeDtypeStruct((m, D_MODEL), F32), jax.ShapeDtypeStruct((m, D_MODEL), BF16)],
        compiler_params=_cparams(1, 48 * 1024 * 1024),
        name="out_proj",
    )(z, w_out, x, gt, g_post.reshape(1, D_MODEL), g_pre.reshape(1, D_MODEL), sh, sc)


def _mlp_kernel(emit_next, h_ref, w1_ref, w2_ref, x_ref, gt_ref, gpost_ref, *rest):
    if emit_next:
        gpre_ref, sh_ref, sc_ref, x_out, h_out, acc = rest
    else:
        x_out, acc = rest
    f = pl.program_id(1)

    @pl.when(f == 0)
    def _():
        acc[...] = jnp.zeros_like(acc)

    a = jnp.maximum(jnp.dot(h_ref[...], w1_ref[...], preferred_element_type=F32), 0.0)
    acc[...] += jnp.dot((a * a).astype(BF16), w2_ref[...], preferred_element_type=F32)

    @pl.when(f == pl.num_programs(1) - 1)
    def _():
        x2 = x_ref[...] + gt_ref[...] * _rms(acc[...], gpost_ref[...])
        x_out[...] = x2
        if emit_next:
            h_out[...] = (_rms(x2, gpre_ref[...]) * (1.0 + sc_ref[...]) + sh_ref[...]).astype(h_out.dtype)


def _mlp(h2, w1, w2, layer, x, gt, g_post, nxt, rows_per_batch, tm=512, tf=1024):
    m = h2.shape[0]
    emit_next = nxt is not None
    row = pl.BlockSpec((tm, D_MODEL), lambda i, f: (i, 0))
    vec = pl.BlockSpec((1, D_MODEL), lambda i, f: (0, 0))
    bvec = pl.BlockSpec((None, 1, D_MODEL), lambda i, f: (i * tm // rows_per_batch, 0, 0))
    in_specs = [row, pl.BlockSpec((None, D_MODEL, tf), lambda i, f: (layer, 0, f)),
                pl.BlockSpec((None, tf, D_MODEL), lambda i, f: (layer, f, 0)), row, bvec, vec]
    args = [h2, w1, w2, x, gt, g_post.reshape(1, D_MODEL)]
    out_specs = [row]
    out_shape = [jax.ShapeDtypeStruct((m, D_MODEL), F32)]
    if emit_next:
        g_pre, sh, sc = nxt
        in_specs += [vec, bvec, bvec]
        args += [g_pre.reshape(1, D_MODEL), sh, sc]
        out_specs.append(row)
        out_shape.append(jax.ShapeDtypeStruct((m, D_MODEL), BF16))
    res = pl.pallas_call(
        functools.partial(_mlp_kernel, emit_next),
        grid=(m // tm, D_FF // tf),
        in_specs=in_specs,
        out_specs=out_specs,
        out_shape=out_shape,
        scratch_shapes=[pltpu.VMEM((tm, D_MODEL), F32)],
        compiler_params=_cparams(2, 56 * 1024 * 1024),
        name="mlp",
    )(*args)
    return (res[0], res[1]) if emit_next else (res[0], None)


def kernel(x, c, ctx, c_ctx, w_mod, b_mod, g_pre_mix, g_post_mix, g_pre_mlp, g_post_mlp, w_in,
           ssm_lam_re, ssm_lam_im, ssm_log_dt, ssm_b_re, ssm_b_im, ssm_c_re, ssm_c_im, ssm_d,
           w_ssm_val, w_ssm_glu, na_rpb, w_na_proj, w_out, w_fc1, w_fc2):
    bsz, t_len, _ = x.shape
    c_len = ctx.shape[1]
    m_lat, m_ctx = bsz * t_len, bsz * c_len
    assert bsz < V7X_SUBLANES and t_len % (GRID_W * 8) == 0 and c_len % S5_CHUNK == 0

    c_rows = jnp.concatenate([c, c_ctx[None, :], jnp.zeros((V7X_SUBLANES - bsz - 1, D_MODEL), F32)], axis=0)
    mod = _modulation(c_rows, w_mod, b_mod)

    def mod_vectors(l):
        lat = [v[:, None, :] for v in jnp.split(mod[l, :bsz], N_MOD, axis=-1)]
        cvec = [jnp.broadcast_to(v[None, :, :], (bsz, 1, D_MODEL))
                for v in jnp.split(mod[l, bsz:bsz + 1], N_MOD, axis=-1)]
        return lat, cvec

    mods = [mod_vectors(l) for l in range(DEPTH)]
    rope_tabs = _rope_tables(t_len)
    rpb_rows = _rpb_rows(na_rpb)
    s5_mats = jax.vmap(_s5_matrices)(ssm_lam_re, ssm_lam_im, ssm_log_dt, ssm_b_re, ssm_b_im, ssm_c_re, ssm_c_im, ssm_d)

    xl = x.reshape(m_lat, D_MODEL)
    xc = ctx.reshape(m_ctx, D_MODEL)
    h = _norm_mod(xl, g_pre_mix[0], mods[0][0][0], mods[0][0][1], t_len)
    hc = _norm_mod(xc, g_pre_mix[0], mods[0][1][0], mods[0][1][1], c_len)

    w_in_b, w_val_b, w_glu_b = w_in.astype(BF16), w_ssm_val.astype(BF16), w_ssm_glu.astype(BF16)
    w_na_b, w_out_b = w_na_proj.astype(BF16), w_out.astype(BF16)
    w1_b, w2_b = w_fc1.astype(BF16), w_fc2.astype(BF16)

    def to_pos_major(v):
        m, w = v.shape
        return v.reshape(m // S5_CHUNK, S5_CHUNK, w).transpose(1, 0, 2).reshape(m, w)

    def from_pos_major(v):
        return v.transpose(1, 2, 0, 3).reshape(-1, v.shape[-1])

    for l in range(DEPTH):
        with_ctx_out = l < DEPTH - 1
        (sh1, sc1, gt1, sh2, sc2, gt2), (csh1, csc1, cgt1, csh2, csc2, cgt2) = mods[l]

        u_tiles, gate_tile = SSM_WIDTH // 1024, (SSM_WIDTH + 3 * NA_WIDTH) // 1024
        qkv = _qkv_proj(h, w_in_b, l, rope_tabs, t_len, rope=True)
        qkvc = _qkv_proj(hc, w_in_b, l, rope_tabs, c_len, rope=False)
        gates = _matmul(h, w_in_b, l, 1024, 1024, F32, first_tile=gate_tile)
        u_lat = _matmul(to_pos_major(h), w_in_b, l, 1024, 1024, BF16, n_tiles=u_tiles)
        u_ctx = _matmul(to_pos_major(hc), w_in_b, l, 1024, 1024, BF16, n_tiles=u_tiles)
        qkv3 = qkv.reshape(bsz, t_len, 3 * NA_WIDTH)
        qkvc3 = qkvc.reshape(bsz, c_len, 3 * NA_WIDTH)

        a_lat, a_ctx = _s5_branch(u_lat.reshape(S5_CHUNK, bsz, t_len // S5_CHUNK, SSM_WIDTH),
                                  u_ctx.reshape(S5_CHUNK, bsz, c_len // S5_CHUNK, SSM_WIDTH), s5_mats, l,
                                  bsz, t_len, c_len)

        att = _neighbourhood_attention(qkv3, qkvc3, rpb_rows, l)
        z = _merge(from_pos_major(a_lat), att.reshape(m_lat, NA_WIDTH), gates, l, w_val_b, w_glu_b, w_na_b)
        xl, h2 = _outproj(z, w_out_b, l, xl, gt1, g_post_mix[l], g_pre_mlp[l], sh2, sc2, t_len)
        nxt = (g_pre_mix[l + 1], mods[l + 1][0][0], mods[l + 1][0][1]) if with_ctx_out else None
        xl, h = _mlp(h2, w1_b, w2_b, l, xl, gt2, g_post_mlp[l], nxt, t_len)

        if with_ctx_out:
            attc = _context_attention(qkvc3)
            gates_c = _matmul(hc, w_in_b, l, 1024, 1024, F32, first_tile=gate_tile)
            zc = _merge(from_pos_major(a_ctx), attc.reshape(m_ctx, NA_WIDTH), gates_c, l, w_val_b, w_glu_b, w_na_b)
            xc, h2c = _outproj(zc, w_out_b, l, xc, cgt1, g_post_mix[l], g_pre_mlp[l], csh2, csc2, c_len)
            nxtc = (g_pre_mix[l + 1], mods[l + 1][1][0], mods[l + 1][1][1])
            xc, hc = _mlp(h2c, w1_b, w2_b, l, xc, cgt2, g_post_mlp[l], nxtc, c_len)

    return xl.reshape(bsz, t_len, D_MODEL)
```

```python
import functools
import math

import numpy as np
import jax
import jax.numpy as jnp
from jax import lax
from jax.experimental import pallas as pl
from jax.experimental.pallas import tpu as pltpu

D_MODEL = 2048
DEPTH = 2
GRID_W = 64
N_MOD = 6
SSM_WIDTH = D_MODEL // 2
SSM_GROUP = 16
SSM_GROUPS = SSM_WIDTH // SSM_GROUP
SSM_STATE = 64
NA_HEADS = 16
NA_HEAD_DIM = 64
NA_WIDTH = NA_HEADS * NA_HEAD_DIM
WIN_R = 8
WIN_C = 16
ROPE_BASE = 10000.0
D_FF = 4 * D_MODEL
NORM_EPS = 1e-6
PROJ_WIDTH = SSM_WIDTH + 3 * NA_WIDTH + 2 * D_MODEL

V7X_LANES = 128
V7X_SUBLANES = 8
V7X_VMEM_BYTES = 64 * 1024 * 1024
HEAD_PAIRS = NA_HEADS * NA_HEAD_DIM // V7X_LANES
S5_CHUNK = 16
S5_TILE = S5_CHUNK * SSM_GROUP
S5_SCAN_TILE = V7X_SUBLANES
S5_GROUPS_PER_BLOCK = V7X_LANES // SSM_GROUP
S5_GROUPS_PER_PASS = 4
MASK_VALUE = -1e30

F32 = jnp.float32
BF16 = jnp.bfloat16


def _cparams(n_axes, vmem_bytes):
    assert vmem_bytes <= V7X_VMEM_BYTES
    return pltpu.CompilerParams(dimension_semantics=("arbitrary",) * n_axes, vmem_limit_bytes=vmem_bytes)


def _sigmoid(x):
    return 0.5 * jnp.tanh(0.5 * x) + 0.5


def _gelu_tanh(x):
    return 0.5 * x * (1.0 + jnp.tanh(math.sqrt(2.0 / math.pi) * (x + 0.044715 * (x * x * x))))


def _rms(x, g):
    return x * lax.rsqrt(jnp.mean(x * x, axis=-1, keepdims=True) + NORM_EPS) * g


def _mod_kernel(c_ref, w_ref, b_ref, o_ref):
    c = c_ref[...]
    s = (c * _sigmoid(c)).astype(BF16)
    o_ref[...] = jnp.dot(s, w_ref[...].astype(BF16), preferred_element_type=F32) + b_ref[...]


def _modulation(c_rows, w_mod, b_mod, tn=1024):
    n = w_mod.shape[-1]
    return pl.pallas_call(
        _mod_kernel,
        grid=(DEPTH, n // tn),
        in_specs=[pl.BlockSpec((V7X_SUBLANES, D_MODEL), lambda l, j: (0, 0)),
                  pl.BlockSpec((None, D_MODEL, tn), lambda l, j: (l, 0, j)),
                  pl.BlockSpec((None, 1, tn), lambda l, j: (l, 0, j))],
        out_specs=pl.BlockSpec((None, V7X_SUBLANES, tn), lambda l, j: (l, 0, j)),
        out_shape=jax.ShapeDtypeStruct((DEPTH, V7X_SUBLANES, n), F32),
        compiler_params=_cparams(2, 40 * 1024 * 1024),
        name="modulation",
    )(c_rows, w_mod, b_mod.reshape(DEPTH, 1, n))


def _norm_mod_kernel(x_ref, g_ref, sh_ref, sc_ref, o_ref):
    y = _rms(x_ref[...], g_ref[...])
    o_ref[...] = (y * (1.0 + sc_ref[...]) + sh_ref[...]).astype(o_ref.dtype)


def _norm_mod(x, g, sh, sc, rows_per_batch, tm=512):
    m = x.shape[0]
    bidx = lambda i: (i * tm // rows_per_batch, 0, 0)
    return pl.pallas_call(
        _norm_mod_kernel,
        grid=(m // tm,),
        in_specs=[pl.BlockSpec((tm, D_MODEL), lambda i: (i, 0)),
                  pl.BlockSpec((1, D_MODEL), lambda i: (0, 0)),
                  pl.BlockSpec((None, 1, D_MODEL), bidx),
                  pl.BlockSpec((None, 1, D_MODEL), bidx)],
        out_specs=pl.BlockSpec((tm, D_MODEL), lambda i: (i, 0)),
        out_shape=jax.ShapeDtypeStruct((m, D_MODEL), BF16),
        compiler_params=_cparams(1, 32 * 1024 * 1024),
        name="norm_mod",
    )(x, g.reshape(1, D_MODEL), sh, sc)


def _matmul_kernel(a_ref, w_ref, o_ref):
    o_ref[...] = jnp.dot(a_ref[...], w_ref[...], preferred_element_type=F32).astype(o_ref.dtype)


def _matmul(a, w, layer, tm, tn, out_dtype, first_tile=0, n_tiles=None):
    m, k = a.shape
    n_tiles = w.shape[2] // tn - first_tile if n_tiles is None else n_tiles
    return pl.pallas_call(
        _matmul_kernel,
        grid=(n_tiles, m // tm),
        in_specs=[pl.BlockSpec((tm, k), lambda j, i: (i, 0)),
                  pl.BlockSpec((None, k, tn), lambda j, i: (layer, 0, first_tile + j))],
        out_specs=pl.BlockSpec((tm, tn), lambda j, i: (i, j)),
        out_shape=jax.ShapeDtypeStruct((m, n_tiles * tn), out_dtype),
        compiler_params=_cparams(2, 40 * 1024 * 1024),
        name="in_proj",
    )(a, w)


def _qkv_kernel(rope, a_ref, w_ref, cos_ref, sin_ref, o_ref):
    j = pl.program_id(0)
    scale = jnp.where(j == 0, NA_HEAD_DIM ** -0.5, 1.0)
    sub = 2 * V7X_LANES

    def body(epilogue):
        for r0 in range(0, a_ref.shape[0], sub):
            rows = slice(r0, r0 + sub)
            epilogue(rows, jnp.dot(a_ref[rows, :], w_ref[...], preferred_element_type=F32))

    def plain(rows, acc):
        o_ref[rows, :] = (acc * scale).astype(o_ref.dtype)

    def rotated(rows, acc):
        for c in range(o_ref.shape[1] // V7X_LANES):
            cols = slice(c * V7X_LANES, (c + 1) * V7X_LANES)
            o_ref[rows, cols] = (_rope(acc[:, cols], cos_ref[rows, :], sin_ref[rows, :]) * scale).astype(o_ref.dtype)

    if not rope:
        body(plain)
        return
    pl.when(j == 2)(lambda: body(plain))
    pl.when(j < 2)(lambda: body(rotated))


def _qkv_proj(a, w, layer, rope_tabs, seq_len, rope, tm=1024, tn=1024):
    m, k = a.shape
    cos, sin = rope_tabs
    first = SSM_WIDTH // tn
    pos_blocks = max(seq_len // tm, 1)
    tab = pl.BlockSpec((tm, V7X_LANES), lambda j, i: (i % pos_blocks, 0))
    return pl.pallas_call(
        functools.partial(_qkv_kernel, rope),
        grid=(3 * NA_WIDTH // tn, m // tm),
        in_specs=[pl.BlockSpec((tm, k), lambda j, i: (i, 0)),
                  pl.BlockSpec((None, k, tn), lambda j, i: (layer, 0, first + j)), tab, tab],
        out_specs=pl.BlockSpec((tm, tn), lambda j, i: (i, j)),
        out_shape=jax.ShapeDtypeStruct((m, 3 * NA_WIDTH), BF16),
        compiler_params=_cparams(2, 40 * 1024 * 1024),
        name="qkv_proj",
    )(a, w, cos, sin)


def _s5_matrices(lam_re, lam_im, log_dt, b_re, b_im, c_re, c_im, d_skip):
    L = S5_CHUNK
    lam = (lam_re.astype(F32), lam_im.astype(F32))
    dt = jnp.exp(log_dt.astype(F32))[..., None]
    tau = jnp.arange(L + 1, dtype=F32)[None, None, :, None]
    mag = jnp.exp((lam[0] * dt)[:, :, None, :] * tau)
    ang = (lam[1] * dt)[:, :, None, :] * tau
    pw = (mag * jnp.cos(ang), mag * jnp.sin(ang))
    num = (pw[0][:, :, 1] - 1.0, pw[1][:, :, 1])
    den = lam[0] * lam[0] + lam[1] * lam[1]
    coef = ((num[0] * lam[0] + num[1] * lam[1]) / den, (num[1] * lam[0] - num[0] * lam[1]) / den)
    b32 = (b_re.astype(F32), b_im.astype(F32))
    b_bar = (coef[0][..., None] * b32[0] - coef[1][..., None] * b32[1],
             coef[0][..., None] * b32[1] + coef[1][..., None] * b32[0])
    both = lambda v: jnp.concatenate([v[0], v[1]], axis=-1)
    pw_tab = jnp.stack([both(pw[0]), both(pw[1])], axis=1)
    bc_tab = jnp.stack([both(jnp.swapaxes(b_bar[0], -1, -2)), both(jnp.swapaxes(b_bar[1], -1, -2)),
                        both(c_re.astype(F32)), both(c_im.astype(F32))], axis=1)
    d_rows = jnp.broadcast_to(d_skip.astype(F32).reshape(SSM_GROUPS, SSM_GROUP, 1), (SSM_GROUPS, SSM_GROUP, S5_TILE))
    return (pw_tab, bc_tab, d_rows) + _s5_tables(lam_re, lam_im, log_dt)


def _s5_tables(lam_re, lam_im, log_dt):
    n = S5_SCAN_TILE
    dt = jnp.exp(log_dt.astype(F32))[..., None]
    steps = jnp.arange(n + 1, dtype=F32)[None, None, :, None] * S5_CHUNK
    mag = jnp.exp((lam_re.astype(F32) * dt)[:, :, None, :] * steps)
    ang = (lam_im.astype(F32) * dt)[:, :, None, :] * steps
    pre, pim = mag * jnp.cos(ang), mag * jnp.sin(ang)
    both = lambda f, r: jnp.concatenate([f, r], axis=-1)
    sel = np.array([1, 2, 4, 8])
    apow = jnp.stack([both(pre[0][:, sel], pre[1][:, sel]), both(pim[0][:, sel], pim[1][:, sel])], axis=2)
    coef = jnp.stack([both(pre[0][:, :n], pre[1][:, n - 1::-1][:, :n]),
                      both(pim[0][:, :n], pim[1][:, n - 1::-1][:, :n])], axis=1)
    return apow, coef


def _dot_nt(a, b):
    return lax.dot_general(a, b, (((1,), (1,)), ((), ())), preferred_element_type=F32)


def _shift_lanes(x, s):
    w = V7X_LANES
    lo, hi = x[:, :w], x[:, w:]
    lane = lax.broadcasted_iota(jnp.int32, lo.shape, 1)
    if s == 0:
        return x
    if s > 0:
        q, r = divmod(s, w)
        if r == 0:
            out = (jnp.zeros_like(lo), lo)
        else:
            rl, rh = pltpu.roll(lo, r, 1), pltpu.roll(hi, r, 1)
            if q == 0:
                out = (jnp.where(lane >= r, rl, 0.0), jnp.where(lane >= r, rh, rl))
            else:
                out = (jnp.zeros_like(lo), jnp.where(lane >= r, rl, 0.0))
    else:
        q, r = divmod(-s, w)
        if r == 0:
            out = (hi, jnp.zeros_like(hi))
        else:
            rl, rh = pltpu.roll(lo, w - r, 1), pltpu.roll(hi, w - r, 1)
            if q == 0:
                out = (jnp.where(lane < w - r, rl, rh), jnp.where(lane < w - r, rh, 0.0))
            else:
                out = (jnp.where(lane < w - r, rh, 0.0), jnp.zeros_like(hi))
    return jnp.concatenate(out, axis=1)


def _s5_group_tables(pw, bc, d_rows):
    L = S5_CHUNK
    half = 2 * SSM_STATE
    lane = lax.broadcasted_iota(jnp.int32, (SSM_GROUP, half), 1)
    fwd = lane < SSM_STATE
    b_r, b_i, c_r, c_i = bc[0], bc[1], bc[2], bc[3]

    def powers(tau_f, tau_r):
        sel = lambda part: jnp.where(fwd, jnp.broadcast_to(pw[part, tau_f:tau_f + 1, :], (SSM_GROUP, half)),
                                     jnp.broadcast_to(pw[part, tau_r:tau_r + 1, :], (SSM_GROUP, half)))
        return sel(0), sel(1)

    def times(m_r, m_i, tau_f, tau_r):
        p_r, p_i = powers(tau_f, tau_r)
        return m_r * p_r - m_i * p_i, m_r * p_i + m_i * p_r

    b_in, c_out_t, c_lag = [], [], []
    for t in range(L):
        re, im = times(b_r, b_i, L - 1 - t, t)
        b_in.append(jnp.concatenate([re, im], axis=1).astype(BF16))
        re, im = times(c_r, c_i, t + 1, L - t)
        c_out_t.append(jnp.concatenate([re, -im], axis=1).astype(BF16))
        re, im = times(c_r, c_i, t, L - 1 - t)
        c_lag.append(jnp.concatenate([re, im], axis=1))
    c_lag = jnp.concatenate(c_lag, axis=0)
    zero = jnp.zeros_like(b_r)
    lag_kernels = lambda keep: lax.dot_general(
        jnp.concatenate([jnp.where(keep, b_r, zero), jnp.where(keep, -b_i, zero)], axis=1), c_lag,
        (((1,), (1,)), ((), ())), precision=lax.Precision.HIGHEST, preferred_element_type=F32)
    kl_f, kl_r = lag_kernels(fwd), lag_kernels(jnp.logical_not(fwd))

    lane_t = lax.broadcasted_iota(jnp.int32, (SSM_GROUP, S5_TILE), 1)
    row_t = lax.broadcasted_iota(jnp.int32, (SSM_GROUP, S5_TILE), 0)
    blocks = []
    for t in range(L):
        m_rows = (_shift_lanes(kl_f, t * SSM_GROUP) + _shift_lanes(kl_r, -(L - 1 - t) * SSM_GROUP)
                  + jnp.where(lane_t == t * SSM_GROUP + row_t, d_rows, 0.0))
        blocks.append(jnp.concatenate([m_rows.astype(BF16), b_in[t]], axis=1))
    return jnp.concatenate(blocks, axis=0), jnp.concatenate(c_out_t, axis=0)


def _s5_kernel(bsz, nc_ctx, nc_lat, *refs):
    L = S5_CHUNK
    xl, xc, pw_ref, bc_ref, d_ref, apow_ref, coef_ref, ol, oc = refs[:9]
    xcat, ucat, y_scr, s_scr, ea_scr, eb_scr, acat, esel = refs[9:]
    nb = nc_ctx + nc_lat
    n_rows = bsz * nb
    lanes = V7X_LANES
    half = 2 * SSM_STATE
    tile = S5_SCAN_TILE
    gpb, gpp = S5_GROUPS_PER_BLOCK, S5_GROUPS_PER_PASS

    @pl.when(pl.program_id(0) == 0)
    def _():
        blk = 256
        for rb in range(esel.shape[0] // blk):
            r = lax.broadcasted_iota(jnp.int32, (blk, esel.shape[1]), 0) + rb * blk
            c = lax.broadcasted_iota(jnp.int32, (blk, esel.shape[1]), 1)
            dest = ((r % S5_TILE) // SSM_GROUP) * lanes + (r // S5_TILE) * SSM_GROUP + r % SSM_GROUP
            esel[rb * blk:(rb + 1) * blk, :] = jnp.where(dest == c, 1.0, 0.0).astype(BF16)

    for t in range(L):
        cols = slice(t * lanes, (t + 1) * lanes)
        for b in range(bsz):
            xcat[b * nb:b * nb + nc_ctx, cols] = xc[t, b]
            xcat[b * nb + nc_ctx:(b + 1) * nb, cols] = xl[t, b]

    ucat[...] = _dot_nt(xcat[...], esel[...]).astype(BF16)

    lane = lax.broadcasted_iota(jnp.int32, (tile, half), 1)
    row = lax.broadcasted_iota(jnp.int32, (tile, half), 0)
    is_fwd = lane < SSM_STATE
    shifts = (1, 2, 4)
    keep_dn = {s: is_fwd & (row >= s) for s in shifts}
    keep_up = {s: jnp.logical_not(is_fwd) & (row < tile - s) for s in shifts}

    def shift(d, s):
        return jnp.where(keep_dn[s], pltpu.roll(d, s, 0), jnp.where(keep_up[s], pltpu.roll(d, tile - s, 0), 0.0))

    def cfma(a_re, a_im, x_re, x_im, y_re, y_im):
        return a_re * x_re - a_im * x_im + y_re, a_re * x_im + a_im * x_re + y_im

    for p in range(gpb // gpp):
        readouts = []
        for q in range(gpp):
            g = p * gpp + q
            w_g, c_out_t = _s5_group_tables(pw_ref[g], bc_ref[g], d_ref[g])
            readouts.append(c_out_t)
            ys = jnp.dot(ucat[:, g * S5_TILE:(g + 1) * S5_TILE], w_g, preferred_element_type=F32)
            y_scr[q] = ys[:, :S5_TILE]
            s_scr[q] = ys[:, S5_TILE:]

        bc = lambda v: jnp.broadcast_to(v, (tile, half))
        a_pow = [[(bc(apow_ref[p * gpp + q, k, 0:1, :]), bc(apow_ref[p * gpp + q, k, 1:2, :])) for k in range(4)]
                 for q in range(gpp)]
        coefs = [(coef_ref[p * gpp + q, 0], coef_ref[p * gpp + q, 1]) for q in range(gpp)]

        for b in range(bsz):
            def make_body(rev_base, base=b * nb):
                def body(i, carry):
                    rf = pl.multiple_of(base + i * tile, tile)
                    rr = pl.multiple_of(base + (rev_base - i) * tile, tile)
                    out = []
                    for q in range(gpp):
                        e_re, e_im = carry[2 * q], carry[2 * q + 1]
                        sf = s_scr[q, pl.ds(rf, tile), :]
                        sr = s_scr[q, pl.ds(rr, tile), :]
                        d_re = jnp.where(is_fwd, sf[:, :half], sr[:, :half])
                        d_im = jnp.where(is_fwd, sf[:, half:], sr[:, half:])
                        for k, s in enumerate(shifts):
                            d_re, d_im = cfma(a_pow[q][k][0], a_pow[q][k][1], shift(d_re, s), shift(d_im, s), d_re, d_im)
                        en_re, en_im = cfma(coefs[q][0], coefs[q][1], e_re, e_im, shift(d_re, 1), shift(d_im, 1))
                        enter = jnp.concatenate([en_re, en_im], axis=-1)
                        ea_scr[q, pl.ds(rf, tile), :] = enter
                        eb_scr[q, pl.ds(rr, tile), :] = enter
                        last_re = jnp.where(is_fwd, bc(d_re[tile - 1:tile]), bc(d_re[0:1]))
                        last_im = jnp.where(is_fwd, bc(d_im[tile - 1:tile]), bc(d_im[0:1]))
                        out.extend(cfma(a_pow[q][3][0], a_pow[q][3][1], e_re, e_im, last_re, last_im))
                    return tuple(out)
                return body

            zero = jnp.zeros((tile, half), F32)
            nt_ctx, nt = nc_ctx // tile, nb // tile
            carry = lax.fori_loop(0, nt_ctx, make_body(nt_ctx - 1), tuple(zero for _ in range(2 * gpp)))
            lax.fori_loop(nt_ctx, nt, make_body(nt - 1 + nt_ctx), carry)

        lane_all = lax.broadcasted_iota(jnp.int32, (n_rows, 2 * half), 1)
        fwd_all = (lane_all % half) < SSM_STATE
        for q in range(gpp):
            g = p * gpp + q
            enter = jnp.where(fwd_all, ea_scr[q], eb_scr[q]).astype(BF16)
            y = y_scr[q] + _dot_nt(enter, readouts[q])
            acat[:, g * S5_TILE:(g + 1) * S5_TILE] = _gelu_tanh(y).astype(BF16)

    for t2 in range(L // 2):
        cols = slice(t2 * 2 * lanes, (t2 + 1) * 2 * lanes)
        a_nat = jnp.dot(acat[...], esel[:, cols], preferred_element_type=F32).astype(BF16)
        for dt_ in range(2):
            t = 2 * t2 + dt_
            for b in range(bsz):
                oc[t, b] = a_nat[b * nb:b * nb + nc_ctx, dt_ * lanes:(dt_ + 1) * lanes]
                ol[t, b] = a_nat[b * nb + nc_ctx:(b + 1) * nb, dt_ * lanes:(dt_ + 1) * lanes]


def _s5_branch(u_lat, u_ctx, mats, layer, bsz, t_len, c_len):
    pw_tab, bc_tab, d_rows, apow, coef = mats
    L = S5_CHUNK
    nc_lat, nc_ctx = t_len // L, c_len // L
    assert nc_lat % S5_SCAN_TILE == 0 and nc_ctx % S5_SCAN_TILE == 0
    n_rows = bsz * (nc_lat + nc_ctx)
    gpb = S5_GROUPS_PER_BLOCK
    slab = lambda n: pl.BlockSpec((L, bsz, n, V7X_LANES), lambda j: (0, 0, 0, j))
    grp = lambda *tail: pl.BlockSpec((None, gpb) + tail, lambda j: (layer, j) + (0,) * len(tail))
    return pl.pallas_call(
        functools.partial(_s5_kernel, bsz, nc_ctx, nc_lat),
        grid=(SSM_GROUPS // gpb,),
        in_specs=[slab(nc_lat), slab(nc_ctx), grp(2, L + 1, 2 * SSM_STATE), grp(4, SSM_GROUP, 2 * SSM_STATE),
                  grp(SSM_GROUP, S5_TILE), grp(4, 2, 2 * SSM_STATE), grp(2, S5_SCAN_TILE, 2 * SSM_STATE)],
        out_specs=[slab(nc_lat), slab(nc_ctx)],
        out_shape=[jax.ShapeDtypeStruct((L, bsz, nc_lat, SSM_WIDTH), BF16),
                   jax.ShapeDtypeStruct((L, bsz, nc_ctx, SSM_WIDTH), BF16)],
        scratch_shapes=[pltpu.VMEM((n_rows, L * V7X_LANES), BF16)] * 2
                       + [pltpu.VMEM((S5_GROUPS_PER_PASS, n_rows, S5_TILE), F32)] * 4
                       + [pltpu.VMEM((n_rows, gpb * S5_TILE), BF16),
                          pltpu.VMEM((gpb * S5_TILE, L * V7X_LANES), BF16)],
        compiler_params=_cparams(1, 56 * 1024 * 1024),
        name="s5",
    )(u_lat, u_ctx, pw_tab, bc_tab, d_rows, apow, coef)


def _rope_tables(t_len):
    nf = NA_HEAD_DIM // 4
    inv = ROPE_BASE ** (-jnp.arange(nf, dtype=F32) / nf)
    pos = jnp.arange(t_len, dtype=jnp.int32)
    ang_r = (pos // GRID_W).astype(F32)[:, None] * inv
    ang_c = (pos % GRID_W).astype(F32)[:, None] * inv
    cos = jnp.concatenate([jnp.cos(ang_r)] * 2 + [jnp.cos(ang_c)] * 2, axis=-1)
    sin = jnp.concatenate([-jnp.sin(ang_r), jnp.sin(ang_r), -jnp.sin(ang_c), jnp.sin(ang_c)], axis=-1)
    reps = V7X_LANES // NA_HEAD_DIM
    return jnp.tile(cos, (1, reps)), jnp.tile(sin, (1, reps))


def _rpb_rows(rpb):
    depth, n_h, n_r, n_c = rpb.shape
    padded = jnp.pad(rpb.astype(F32), ((0, 0), (0, 0), (0, 2 * WIN_R - n_r), (0, V7X_LANES - n_c)))
    return padded.reshape(depth, HEAD_PAIRS, n_h // HEAD_PAIRS, 2 * WIN_R, V7X_LANES)


def _build_bias(rpb_ref, bias_scr):
    kc = min(WIN_C, GRID_W)
    w = lax.broadcasted_iota(jnp.int32, (GRID_W, V7X_LANES), 0)
    lane = lax.broadcasted_iota(jnp.int32, (GRID_W, V7X_LANES), 1)
    kcol = lane % GRID_W
    c0 = jnp.clip(w - kc // 2, 0, GRID_W - kc)
    valid = (kcol >= c0) & (kcol < c0 + kc)
    for h in range(rpb_ref.shape[0]):
        lo, hi = [], []
        for ro in range(2 * WIN_R - 1):
            row = jnp.broadcast_to(rpb_ref[h, ro:ro + 1, :], (GRID_W, V7X_LANES))
            lo.append(pltpu.roll(row, V7X_LANES - (WIN_C - 1), 1, stride=1, stride_axis=0))
            hi.append(pltpu.roll(row, V7X_LANES - (WIN_C - 1) + GRID_W, 1, stride=1, stride_axis=0))
        for ro in range(2 * WIN_R - 2):
            both = jnp.where(lane < GRID_W, lo[ro], hi[ro + 1])
            bias_scr[ro, h * GRID_W:(h + 1) * GRID_W, :] = jnp.where(valid, both, MASK_VALUE)


def _rope(x, cos, sin_signed):
    nf = NA_HEAD_DIM // 4
    lane = lax.broadcasted_iota(jnp.int32, x.shape, 1)
    first = (lane % (2 * nf)) < nf
    swapped = jnp.where(first, pltpu.roll(x, V7X_LANES - nf, 1), pltpu.roll(x, nf, 1))
    return x * cos + swapped * sin_signed


def _na_kernel(rows, kr, q_ref, k_ref, v_ref, kc_ref, vc_ref, rpb_ref, w1_ref, w2_ref, o_ref, w1b_ref, w2b_ref,
               s_scr, p_scr, bias_scr):
    step = pl.program_id(2)
    w1b_ref[...] = w1_ref[...].astype(BF16)
    w2b_ref[...] = w2_ref[...].astype(BF16)
    dh = NA_HEAD_DIM
    rows_per_step = q_ref.shape[0] // GRID_W
    n_heads = V7X_LANES // dh

    @pl.when(step == 0)
    def _():
        _build_bias(rpb_ref, bias_scr)

    lane = lax.broadcasted_iota(jnp.int32, (GRID_W, V7X_LANES), 1)
    n_loc = kr * GRID_W
    wins = []
    for i in range(rows_per_step):
        r = step * rows_per_step + i
        r0 = jnp.clip(r - kr // 2, 0, rows - kr)
        ro0 = r0 - r + (WIN_R - 1)
        win = pl.ds(pl.multiple_of(r0 * GRID_W, GRID_W), n_loc)
        wins.append(win)
        q = q_ref[pl.ds(i * GRID_W, GRID_W), :]
        q2 = jnp.concatenate([jnp.where((lane // dh) == h, q, jnp.zeros((), q.dtype)) for h in range(n_heads)], axis=0)
        s_loc = _dot_nt(q2, k_ref[win, :])
        for j in range(kr // 2):
            cols = slice(j * V7X_LANES, (j + 1) * V7X_LANES)
            s_scr[i, :, cols] = s_loc[:, cols] + bias_scr[ro0 + 2 * j]
        s_scr[i, :, n_loc:] = _dot_nt(q2, kc_ref[...])
    for i in range(rows_per_step):
        s = s_scr[i]
        m = jnp.max(s, axis=-1, keepdims=True)
        p = jnp.exp(s - m)
        denom = jnp.sum(p, axis=-1, keepdims=True)
        p_scr[i] = p.astype(BF16)
        o = (jnp.dot(p_scr[i, :, :n_loc], v_ref[wins[i], :], preferred_element_type=F32)
             + jnp.dot(p_scr[i, :, n_loc:], vc_ref[...], preferred_element_type=F32)) / denom
        out = o[:GRID_W]
        for h in range(1, n_heads):
            out = jnp.where((lane // dh) == h, o[h * GRID_W:(h + 1) * GRID_W], out)
        o_ref[pl.ds(i * GRID_W, GRID_W), :] = out.astype(o_ref.dtype)


def _neighbourhood_attention(qkv, qkvc, rpb_rows, layer, w_fc1, w_fc2, rows_per_step=8):
    bsz, t_len, _ = qkv.shape
    c_len = qkvc.shape[1]
    rows = t_len // GRID_W
    kr = min(WIN_R, rows)
    assert kr % 2 == 0 and rows % rows_per_step == 0
    qb, kb, vb = 0, NA_WIDTH // V7X_LANES, 2 * NA_WIDTH // V7X_LANES
    tq = rows_per_step * GRID_W
    seq_spec = lambda off: pl.BlockSpec((None, t_len, V7X_LANES), lambda b, hp, r: (b, 0, off + hp))
    ctx_spec = lambda off: pl.BlockSpec((None, c_len, V7X_LANES), lambda b, hp, r: (b, 0, off + hp))
    bias_spec = pl.BlockSpec((None, None) + rpb_rows.shape[2:], lambda b, hp, r: (layer, hp, 0, 0, 0))
    n_r = rows // rows_per_step
    n_steps = bsz * HEAD_PAIRS * n_r
    step_of = lambda b, hp, r: (b * HEAD_PAIRS + hp) * n_r + r
    (_, k1, n1), (_, k2, n2) = w_fc1.shape, w_fc2.shape
    assert k1 % (n_steps * 16) == 0 and k2 % (n_steps * 16) == 0
    return pl.pallas_call(
        functools.partial(_na_kernel, rows, kr),
        grid=(bsz, HEAD_PAIRS, n_r),
        in_specs=[pl.BlockSpec((None, tq, V7X_LANES), lambda b, hp, r: (b, r, qb + hp)),
                  seq_spec(kb), seq_spec(vb), ctx_spec(kb), ctx_spec(vb), bias_spec,
                  pl.BlockSpec((None, k1 // n_steps, n1), lambda b, hp, r: (layer, step_of(b, hp, r), 0)),
                  pl.BlockSpec((None, k2 // n_steps, n2), lambda b, hp, r: (layer, step_of(b, hp, r), 0))],
        out_specs=[pl.BlockSpec((None, tq, V7X_LANES), lambda b, hp, r: (b, r, hp)),
                   pl.BlockSpec((k1 // n_steps, n1), lambda b, hp, r: (step_of(b, hp, r), 0)),
                   pl.BlockSpec((k2 // n_steps, n2), lambda b, hp, r: (step_of(b, hp, r), 0))],
        out_shape=[jax.ShapeDtypeStruct((bsz, t_len, NA_WIDTH), BF16),
                   jax.ShapeDtypeStruct((k1, n1), BF16), jax.ShapeDtypeStruct((k2, n2), BF16)],
        scratch_shapes=[pltpu.VMEM((rows_per_step, 2 * GRID_W, kr * GRID_W + c_len), F32),
                        pltpu.VMEM((rows_per_step, 2 * GRID_W, kr * GRID_W + c_len), BF16),
                        pltpu.VMEM((2 * WIN_R - 2, 2 * GRID_W, 2 * GRID_W), F32)],
        compiler_params=_cparams(3, 32 * 1024 * 1024),
        name="neighbourhood_attention",
    )(qkv, qkv, qkv, qkvc, qkvc, rpb_rows, w_fc1, w_fc2)


def _ctx_attn_kernel(q_ref, k_ref, v_ref, o_ref):
    dh = NA_HEAD_DIM
    q, k, v = q_ref[...], k_ref[...], v_ref[...]
    for h in range(V7X_LANES // dh):
        hs = slice(h * dh, (h + 1) * dh)
        s = _dot_nt(q[:, hs], k[:, hs])
        p = jnp.exp(s - jnp.max(s, axis=-1, keepdims=True))
        o = jnp.dot(p.astype(BF16), v[:, hs], preferred_element_type=F32)
        o_ref[:, hs] = (o / jnp.sum(p, axis=-1, keepdims=True)).astype(o_ref.dtype)


def _context_attention(qkvc):
    bsz, c_len, _ = qkvc.shape
    qb, kb, vb = 0, NA_WIDTH // V7X_LANES, 2 * NA_WIDTH // V7X_LANES
    spec = lambda off: pl.BlockSpec((None, c_len, V7X_LANES), lambda b, hp: (b, 0, off + hp))
    return pl.pallas_call(
        _ctx_attn_kernel,
        grid=(bsz, HEAD_PAIRS),
        in_specs=[spec(qb), spec(kb), spec(vb)],
        out_specs=spec(0),
        out_shape=jax.ShapeDtypeStruct((bsz, c_len, NA_WIDTH), BF16),
        compiler_params=_cparams(2, 32 * 1024 * 1024),
        name="context_attention",
    )(qkvc, qkvc, qkvc)


def _merge_kernel(a_ref, att_ref, gs_ref, gn_ref, wv_ref, wg_ref, wn_ref, o_ref):
    a = a_ref[...]
    val = jnp.dot(a, wv_ref[...], preferred_element_type=F32)
    glu = jnp.dot(a, wg_ref[...], preferred_element_type=F32)
    br_n = jnp.dot(att_ref[...], wn_ref[...], preferred_element_type=F32)
    z = _sigmoid(gs_ref[...]) * (val * _sigmoid(glu)) + _sigmoid(gn_ref[...]) * br_n
    o_ref[...] = z.astype(o_ref.dtype)


def _merge(a, att, gates, layer, w_val, w_glu, w_na, tm=512, tn=512):
    m = a.shape[0]
    gs_off = 0
    gn_off = gs_off + D_MODEL // tn
    act = pl.BlockSpec((tm, SSM_WIDTH), lambda j, i: (i, 0))
    wsp = pl.BlockSpec((None, SSM_WIDTH, tn), lambda j, i: (layer, 0, j))
    return pl.pallas_call(
        _merge_kernel,
        grid=(D_MODEL // tn, m // tm),
        in_specs=[act, act,
                  pl.BlockSpec((tm, tn), lambda j, i: (i, gs_off + j)),
                  pl.BlockSpec((tm, tn), lambda j, i: (i, gn_off + j)),
                  wsp, wsp, wsp],
        out_specs=pl.BlockSpec((tm, tn), lambda j, i: (i, j)),
        out_shape=jax.ShapeDtypeStruct((m, D_MODEL), BF16),
        compiler_params=_cparams(2, 40 * 1024 * 1024),
        name="merge",
    )(a, att, gates, gates, w_val, w_glu, w_na)


def _outproj_kernel(z_ref, w_ref, x_ref, gt_ref, gpost_ref, gpre_ref, sh_ref, sc_ref, x_out, h_out):
    sub = V7X_LANES
    for r0 in range(0, z_ref.shape[0], sub):
        rows = slice(r0, r0 + sub)
        out = jnp.dot(z_ref[rows, :], w_ref[...], preferred_element_type=F32)
        x1 = x_ref[rows, :] + gt_ref[...] * _rms(out, gpost_ref[...])
        x_out[rows, :] = x1
        h_out[rows, :] = (_rms(x1, gpre_ref[...]) * (1.0 + sc_ref[...]) + sh_ref[...]).astype(h_out.dtype)


def _outproj(z, w_out, layer, x, gt, g_post, g_pre, sh, sc, rows_per_batch, tm=512):
    m = z.shape[0]
    row = pl.BlockSpec((tm, D_MODEL), lambda i: (i, 0))
    vec = pl.BlockSpec((1, D_MODEL), lambda i: (0, 0))
    bvec = pl.BlockSpec((None, 1, D_MODEL), lambda i: (i * tm // rows_per_batch, 0, 0))
    return pl.pallas_call(
        _outproj_kernel,
        grid=(m // tm,),
        in_specs=[row, pl.BlockSpec((None, D_MODEL, D_MODEL), lambda i: (layer, 0, 0)), row, bvec, vec, vec, bvec,
                  bvec],
        out_specs=[row, row],
        out_shape=[jax.ShapeDtypeStruct((m, D_MODEL), F32), jax.ShapeDtypeStruct((m, D_MODEL), BF16)],
        compiler_params=_cparams(1, 48 * 1024 * 1024),
        name="out_proj",
    )(z, w_out, x, gt, g_post.reshape(1, D_MODEL), g_pre.reshape(1, D_MODEL), sh, sc)


def _mlp_kernel(emit_next, h_ref, w1_ref, w2_ref, x_ref, gt_ref, gpost_ref, *rest):
    if emit_next:
        gpre_ref, sh_ref, sc_ref, x_out, h_out, acc = rest
    else:
        x_out, acc = rest
    f = pl.program_id(1)

    @pl.when(f == 0)
    def _():
        acc[...] = jnp.zeros_like(acc)

    a = jnp.maximum(jnp.dot(h_ref[...], w1_ref[...], preferred_element_type=F32), 0.0)
    acc[...] += jnp.dot((a * a).astype(BF16), w2_ref[...], preferred_element_type=F32)

    @pl.when(f == pl.num_programs(1) - 1)
    def _():
        x2 = x_ref[...] + gt_ref[...] * _rms(acc[...], gpost_ref[...])
        x_out[...] = x2
        if emit_next:
            h_out[...] = (_rms(x2, gpre_ref[...]) * (1.0 + sc_ref[...]) + sh_ref[...]).astype(h_out.dtype)


def _mlp(h2, w1, w2, layer, x, gt, g_post, nxt, rows_per_batch, tm=512, tf=1024):
    m = h2.shape[0]
    emit_next = nxt is not None
    row = pl.BlockSpec((tm, D_MODEL), lambda i, f: (i, 0))
    vec = pl.BlockSpec((1, D_MODEL), lambda i, f: (0, 0))
    bvec = pl.BlockSpec((None, 1, D_MODEL), lambda i, f: (i * tm // rows_per_batch, 0, 0))
    in_specs = [row, pl.BlockSpec((None, D_MODEL, tf), lambda i, f: (layer, 0, f)),
                pl.BlockSpec((None, tf, D_MODEL), lambda i, f: (layer, f, 0)), row, bvec, vec]
    args = [h2, w1, w2, x, gt, g_post.reshape(1, D_MODEL)]
    out_specs = [row]
    out_shape = [jax.ShapeDtypeStruct((m, D_MODEL), F32)]
    if emit_next:
        g_pre, sh, sc = nxt
        in_specs += [vec, bvec, bvec]
        args += [g_pre.reshape(1, D_MODEL), sh, sc]
        out_specs.append(row)
        out_shape.append(jax.ShapeDtypeStruct((m, D_MODEL), BF16))
    res = pl.pallas_call(
        functools.partial(_mlp_kernel, emit_next),
        grid=(m // tm, D_FF // tf),
        in_specs=in_specs,
        out_specs=out_specs,
        out_shape=out_shape,
        scratch_shapes=[pltpu.VMEM((tm, D_MODEL), F32)],
        compiler_params=_cparams(2, 56 * 1024 * 1024),
        name="mlp",
    )(*args)
    return (res[0], res[1]) if emit_next else (res[0], None)


def kernel(x, c, ctx, c_ctx, w_mod, b_mod, g_pre_mix, g_post_mix, g_pre_mlp, g_post_mlp, w_in,
           ssm_lam_re, ssm_lam_im, ssm_log_dt, ssm_b_re, ssm_b_im, ssm_c_re, ssm_c_im, ssm_d,
           w_ssm_val, w_ssm_glu, na_rpb, w_na_proj, w_out, w_fc1, w_fc2):
    bsz, t_len, _ = x.shape
    c_len = ctx.shape[1]
    m_lat, m_ctx = bsz * t_len, bsz * c_len
    assert bsz < V7X_SUBLANES and t_len % (GRID_W * 8) == 0 and c_len % S5_CHUNK == 0

    c_rows = jnp.concatenate([c, c_ctx[None, :], jnp.zeros((V7X_SUBLANES - bsz - 1, D_MODEL), F32)], axis=0)
    mod = _modulation(c_rows, w_mod, b_mod)

    def mod_vectors(l):
        lat = [v[:, None, :] for v in jnp.split(mod[l, :bsz], N_MOD, axis=-1)]
        cvec = [jnp.broadcast_to(v[None, :, :], (bsz, 1, D_MODEL))
                for v in jnp.split(mod[l, bsz:bsz + 1], N_MOD, axis=-1)]
        return lat, cvec

    mods = [mod_vectors(l) for l in range(DEPTH)]
    rope_tabs = _rope_tables(t_len)
    rpb_rows = _rpb_rows(na_rpb)
    s5_mats = jax.vmap(_s5_matrices)(ssm_lam_re, ssm_lam_im, ssm_log_dt, ssm_b_re, ssm_b_im, ssm_c_re, ssm_c_im, ssm_d)

    xl = x.reshape(m_lat, D_MODEL)
    xc = ctx.reshape(m_ctx, D_MODEL)
    h = _norm_mod(xl, g_pre_mix[0], mods[0][0][0], mods[0][0][1], t_len)
    hc = _norm_mod(xc, g_pre_mix[0], mods[0][1][0], mods[0][1][1], c_len)

    w_in_b, w_val_b, w_glu_b = w_in.astype(BF16), w_ssm_val.astype(BF16), w_ssm_glu.astype(BF16)
    w_na_b, w_out_b = w_na_proj.astype(BF16), w_out.astype(BF16)

    def to_pos_major(v):
        m, w = v.shape
        return v.reshape(m // S5_CHUNK, S5_CHUNK, w).transpose(1, 0, 2).reshape(m, w)

    def from_pos_major(v):
        return v.transpose(1, 2, 0, 3).reshape(-1, v.shape[-1])

    for l in range(DEPTH):
        with_ctx_out = l < DEPTH - 1
        (sh1, sc1, gt1, sh2, sc2, gt2), (csh1, csc1, cgt1, csh2, csc2, cgt2) = mods[l]

        u_tiles, gate_tile = SSM_WIDTH // 1024, (SSM_WIDTH + 3 * NA_WIDTH) // 1024
        qkv = _qkv_proj(h, w_in_b, l, rope_tabs, t_len, rope=True)
        qkvc = _qkv_proj(hc, w_in_b, l, rope_tabs, c_len, rope=False)
        gates = _matmul(h, w_in_b, l, 1024, 1024, F32, first_tile=gate_tile)
        u_lat = _matmul(to_pos_major(h), w_in_b, l, 1024, 1024, BF16, n_tiles=u_tiles)
        u_ctx = _matmul(to_pos_major(hc), w_in_b, l, 1024, 1024, BF16, n_tiles=u_tiles)
        qkv3 = qkv.reshape(bsz, t_len, 3 * NA_WIDTH)
        qkvc3 = qkvc.reshape(bsz, c_len, 3 * NA_WIDTH)

        a_lat, a_ctx = _s5_branch(u_lat.reshape(S5_CHUNK, bsz, t_len // S5_CHUNK, SSM_WIDTH),
                                  u_ctx.reshape(S5_CHUNK, bsz, c_len // S5_CHUNK, SSM_WIDTH), s5_mats, l,
                                  bsz, t_len, c_len)

        att, w1_l, w2_l = _neighbourhood_attention(qkv3, qkvc3, rpb_rows, l, w_fc1, w_fc2)
        w1_l, w2_l = w1_l[None], w2_l[None]
        z = _merge(from_pos_major(a_lat), att.reshape(m_lat, NA_WIDTH), gates, l, w_val_b, w_glu_b, w_na_b)
        xl, h2 = _outproj(z, w_out_b, l, xl, gt1, g_post_mix[l], g_pre_mlp[l], sh2, sc2, t_len)
        nxt = (g_pre_mix[l + 1], mods[l + 1][0][0], mods[l + 1][0][1]) if with_ctx_out else None
        xl, h = _mlp(h2, w1_l, w2_l, 0, xl, gt2, g_post_mlp[l], nxt, t_len)

        if with_ctx_out:
            attc = _context_attention(qkvc3)
            gates_c = _matmul(hc, w_in_b, l, 1024, 1024, F32, first_tile=gate_tile)
            zc = _merge(from_pos_major(a_ctx), attc.reshape(m_ctx, NA_WIDTH), gates_c, l, w_val_b, w_glu_b, w_na_b)
            xc, h2c = _outproj(zc, w_out_b, l, xc, cgt1, g_post_mix[l], g_pre_mlp[l], csh2, csc2, c_len)
            nxtc = (g_pre_mix[l + 1], mods[l + 1][1][0], mods[l + 1][1][1])
            xc, hc = _mlp(h2c, w1_l, w2_l, 0, xc, cgt2, g_post_mlp[l], nxtc, c_len)

    return xl.reshape(bsz, t_len, D_MODEL)
```

```python
import functools
import math

import numpy as np
import jax
import jax.numpy as jnp
from jax import lax
from jax.experimental import pallas as pl
from jax.experimental.pallas import tpu as pltpu

D_MODEL = 2048
DEPTH = 2
GRID_W = 64
N_MOD = 6
SSM_WIDTH = D_MODEL // 2
SSM_GROUP = 16
SSM_GROUPS = SSM_WIDTH // SSM_GROUP
SSM_STATE = 64
NA_HEADS = 16
NA_HEAD_DIM = 64
NA_WIDTH = NA_HEADS * NA_HEAD_DIM
WIN_R = 8
WIN_C = 16
ROPE_BASE = 10000.0
D_FF = 4 * D_MODEL
NORM_EPS = 1e-6
PROJ_WIDTH = SSM_WIDTH + 3 * NA_WIDTH + 2 * D_MODEL

V7X_LANES = 128
V7X_SUBLANES = 8
V7X_VMEM_BYTES = 64 * 1024 * 1024
HEAD_PAIRS = NA_HEADS * NA_HEAD_DIM // V7X_LANES
S5_CHUNK = 16
S5_TILE = S5_CHUNK * SSM_GROUP
S5_SCAN_TILE = V7X_SUBLANES
S5_GROUPS_PER_BLOCK = V7X_LANES // SSM_GROUP
S5_GROUPS_PER_PASS = 4
MASK_VALUE = -1e30

F32 = jnp.float32
BF16 = jnp.bfloat16


def _cparams(n_axes, vmem_bytes):
    assert vmem_bytes <= V7X_VMEM_BYTES
    return pltpu.CompilerParams(dimension_semantics=("arbitrary",) * n_axes, vmem_limit_bytes=vmem_bytes)


def _sigmoid(x):
    return 0.5 * jnp.tanh(0.5 * x) + 0.5


def _gelu_tanh(x):
    return 0.5 * x * (1.0 + jnp.tanh(math.sqrt(2.0 / math.pi) * (x + 0.044715 * (x * x * x))))


def _rms(x, g):
    return x * lax.rsqrt(jnp.mean(x * x, axis=-1, keepdims=True) + NORM_EPS) * g


def _mod_kernel(c_ref, w_ref, b_ref, o_ref):
    c = c_ref[...]
    s = (c * _sigmoid(c)).astype(BF16)
    o_ref[...] = jnp.dot(s, w_ref[...].astype(BF16), preferred_element_type=F32) + b_ref[...]


def _modulation(c_rows, w_mod, b_mod, tn=1024):
    n = w_mod.shape[-1]
    return pl.pallas_call(
        _mod_kernel,
        grid=(DEPTH, n // tn),
        in_specs=[pl.BlockSpec((V7X_SUBLANES, D_MODEL), lambda l, j: (0, 0)),
                  pl.BlockSpec((None, D_MODEL, tn), lambda l, j: (l, 0, j)),
                  pl.BlockSpec((None, 1, tn), lambda l, j: (l, 0, j))],
        out_specs=pl.BlockSpec((None, V7X_SUBLANES, tn), lambda l, j: (l, 0, j)),
        out_shape=jax.ShapeDtypeStruct((DEPTH, V7X_SUBLANES, n), F32),
        compiler_params=_cparams(2, 40 * 1024 * 1024),
        name="modulation",
    )(c_rows, w_mod, b_mod.reshape(DEPTH, 1, n))


def _norm_mod_kernel(x_ref, g_ref, sh_ref, sc_ref, o_ref):
    y = _rms(x_ref[...], g_ref[...])
    o_ref[...] = (y * (1.0 + sc_ref[...]) + sh_ref[...]).astype(o_ref.dtype)


def _norm_mod(x, g, sh, sc, rows_per_batch, tm=512):
    m = x.shape[0]
    bidx = lambda i: (i * tm // rows_per_batch, 0, 0)
    return pl.pallas_call(
        _norm_mod_kernel,
        grid=(m // tm,),
        in_specs=[pl.BlockSpec((tm, D_MODEL), lambda i: (i, 0)),
                  pl.BlockSpec((1, D_MODEL), lambda i: (0, 0)),
                  pl.BlockSpec((None, 1, D_MODEL), bidx),
                  pl.BlockSpec((None, 1, D_MODEL), bidx)],
        out_specs=pl.BlockSpec((tm, D_MODEL), lambda i: (i, 0)),
        out_shape=jax.ShapeDtypeStruct((m, D_MODEL), BF16),
        compiler_params=_cparams(1, 32 * 1024 * 1024),
        name="norm_mod",
    )(x, g.reshape(1, D_MODEL), sh, sc)


def _matmul_kernel(a_ref, w_ref, o_ref):
    o_ref[...] = jnp.dot(a_ref[...], w_ref[...], preferred_element_type=F32).astype(o_ref.dtype)


def _matmul(a, w, layer, tm, tn, out_dtype, first_tile=0, n_tiles=None):
    m, k = a.shape
    n_tiles = w.shape[2] // tn - first_tile if n_tiles is None else n_tiles
    return pl.pallas_call(
        _matmul_kernel,
        grid=(n_tiles, m // tm),
        in_specs=[pl.BlockSpec((tm, k), lambda j, i: (i, 0)),
                  pl.BlockSpec((None, k, tn), lambda j, i: (layer, 0, first_tile + j))],
        out_specs=pl.BlockSpec((tm, tn), lambda j, i: (i, j)),
        out_shape=jax.ShapeDtypeStruct((m, n_tiles * tn), out_dtype),
        compiler_params=_cparams(2, 40 * 1024 * 1024),
        name="in_proj",
    )(a, w)


def _qkv_kernel(rope, a_ref, w_ref, cos_ref, sin_ref, o_ref):
    j = pl.program_id(0)
    scale = jnp.where(j == 0, NA_HEAD_DIM ** -0.5, 1.0)
    sub = 2 * V7X_LANES

    def body(epilogue):
        for r0 in range(0, a_ref.shape[0], sub):
            rows = slice(r0, r0 + sub)
            epilogue(rows, jnp.dot(a_ref[rows, :], w_ref[...], preferred_element_type=F32))

    def plain(rows, acc):
        o_ref[rows, :] = (acc * scale).astype(o_ref.dtype)

    def rotated(rows, acc):
        for c in range(o_ref.shape[1] // V7X_LANES):
            cols = slice(c * V7X_LANES, (c + 1) * V7X_LANES)
            o_ref[rows, cols] = (_rope(acc[:, cols], cos_ref[rows, :], sin_ref[rows, :]) * scale).astype(o_ref.dtype)

    if not rope:
        body(plain)
        return
    pl.when(j == 2)(lambda: body(plain))
    pl.when(j < 2)(lambda: body(rotated))


def _qkv_proj(a, w, layer, rope_tabs, seq_len, rope, tm=1024, tn=1024):
    m, k = a.shape
    cos, sin = rope_tabs
    first = SSM_WIDTH // tn
    pos_blocks = max(seq_len // tm, 1)
    tab = pl.BlockSpec((tm, V7X_LANES), lambda j, i: (i % pos_blocks, 0))
    return pl.pallas_call(
        functools.partial(_qkv_kernel, rope),
        grid=(3 * NA_WIDTH // tn, m // tm),
        in_specs=[pl.BlockSpec((tm, k), lambda j, i: (i, 0)),
                  pl.BlockSpec((None, k, tn), lambda j, i: (layer, 0, first + j)), tab, tab],
        out_specs=pl.BlockSpec((tm, tn), lambda j, i: (i, j)),
        out_shape=jax.ShapeDtypeStruct((m, 3 * NA_WIDTH), BF16),
        compiler_params=_cparams(2, 40 * 1024 * 1024),
        name="qkv_proj",
    )(a, w, cos, sin)


def _s5_matrices(lam_re, lam_im, log_dt, b_re, b_im, c_re, c_im, d_skip):
    L = S5_CHUNK
    lam = (lam_re.astype(F32), lam_im.astype(F32))
    dt = jnp.exp(log_dt.astype(F32))[..., None]
    tau = jnp.arange(L + 1, dtype=F32)[None, None, :, None]
    mag = jnp.exp((lam[0] * dt)[:, :, None, :] * tau)
    ang = (lam[1] * dt)[:, :, None, :] * tau
    pw = (mag * jnp.cos(ang), mag * jnp.sin(ang))
    num = (pw[0][:, :, 1] - 1.0, pw[1][:, :, 1])
    den = lam[0] * lam[0] + lam[1] * lam[1]
    coef = ((num[0] * lam[0] + num[1] * lam[1]) / den, (num[1] * lam[0] - num[0] * lam[1]) / den)
    b32 = (b_re.astype(F32), b_im.astype(F32))
    b_bar = (coef[0][..., None] * b32[0] - coef[1][..., None] * b32[1],
             coef[0][..., None] * b32[1] + coef[1][..., None] * b32[0])
    both = lambda v: jnp.concatenate([v[0], v[1]], axis=-1)
    pw_tab = jnp.stack([both(pw[0]), both(pw[1])], axis=1)
    bc_tab = jnp.stack([both(jnp.swapaxes(b_bar[0], -1, -2)), both(jnp.swapaxes(b_bar[1], -1, -2)),
                        both(c_re.astype(F32)), both(c_im.astype(F32))], axis=1)
    d_rows = jnp.broadcast_to(d_skip.astype(F32).reshape(SSM_GROUPS, SSM_GROUP, 1), (SSM_GROUPS, SSM_GROUP, S5_TILE))
    return (pw_tab, bc_tab, d_rows) + _s5_tables(lam_re, lam_im, log_dt)


def _s5_tables(lam_re, lam_im, log_dt):
    n = S5_SCAN_TILE
    dt = jnp.exp(log_dt.astype(F32))[..., None]
    steps = jnp.arange(n + 1, dtype=F32)[None, None, :, None] * S5_CHUNK
    mag = jnp.exp((lam_re.astype(F32) * dt)[:, :, None, :] * steps)
    ang = (lam_im.astype(F32) * dt)[:, :, None, :] * steps
    pre, pim = mag * jnp.cos(ang), mag * jnp.sin(ang)
    both = lambda f, r: jnp.concatenate([f, r], axis=-1)
    sel = np.array([1, 2, 4, 8])
    apow = jnp.stack([both(pre[0][:, sel], pre[1][:, sel]), both(pim[0][:, sel], pim[1][:, sel])], axis=2)
    coef = jnp.stack([both(pre[0][:, :n], pre[1][:, n - 1::-1][:, :n]),
                      both(pim[0][:, :n], pim[1][:, n - 1::-1][:, :n])], axis=1)
    return apow, coef


def _dot_nt(a, b):
    return lax.dot_general(a, b, (((1,), (1,)), ((), ())), preferred_element_type=F32)


def _shift_lanes(x, s):
    w = V7X_LANES
    lo, hi = x[:, :w], x[:, w:]
    lane = lax.broadcasted_iota(jnp.int32, lo.shape, 1)
    if s == 0:
        return x
    if s > 0:
        q, r = divmod(s, w)
        if r == 0:
            out = (jnp.zeros_like(lo), lo)
        else:
            rl, rh = pltpu.roll(lo, r, 1), pltpu.roll(hi, r, 1)
            if q == 0:
                out = (jnp.where(lane >= r, rl, 0.0), jnp.where(lane >= r, rh, rl))
            else:
                out = (jnp.zeros_like(lo), jnp.where(lane >= r, rl, 0.0))
    else:
        q, r = divmod(-s, w)
        if r == 0:
            out = (hi, jnp.zeros_like(hi))
        else:
            rl, rh = pltpu.roll(lo, w - r, 1), pltpu.roll(hi, w - r, 1)
            if q == 0:
                out = (jnp.where(lane < w - r, rl, rh), jnp.where(lane < w - r, rh, 0.0))
            else:
                out = (jnp.where(lane < w - r, rh, 0.0), jnp.zeros_like(hi))
    return jnp.concatenate(out, axis=1)


def _s5_group_tables(pw, bc, d_rows):
    L = S5_CHUNK
    half = 2 * SSM_STATE
    lane = lax.broadcasted_iota(jnp.int32, (SSM_GROUP, half), 1)
    fwd = lane < SSM_STATE
    b_r, b_i, c_r, c_i = bc[0], bc[1], bc[2], bc[3]

    def powers(tau_f, tau_r):
        sel = lambda part: jnp.where(fwd, jnp.broadcast_to(pw[part, tau_f:tau_f + 1, :], (SSM_GROUP, half)),
                                     jnp.broadcast_to(pw[part, tau_r:tau_r + 1, :], (SSM_GROUP, half)))
        return sel(0), sel(1)

    def times(m_r, m_i, tau_f, tau_r):
        p_r, p_i = powers(tau_f, tau_r)
        return m_r * p_r - m_i * p_i, m_r * p_i + m_i * p_r

    b_in, c_out_t, c_lag = [], [], []
    for t in range(L):
        re, im = times(b_r, b_i, L - 1 - t, t)
        b_in.append(jnp.concatenate([re, im], axis=1).astype(BF16))
        re, im = times(c_r, c_i, t + 1, L - t)
        c_out_t.append(jnp.concatenate([re, -im], axis=1).astype(BF16))
        re, im = times(c_r, c_i, t, L - 1 - t)
        c_lag.append(jnp.concatenate([re, im], axis=1))
    c_lag = jnp.concatenate(c_lag, axis=0)
    zero = jnp.zeros_like(b_r)
    lag_kernels = lambda keep: lax.dot_general(
        jnp.concatenate([jnp.where(keep, b_r, zero), jnp.where(keep, -b_i, zero)], axis=1), c_lag,
        (((1,), (1,)), ((), ())), precision=lax.Precision.HIGHEST, preferred_element_type=F32)
    kl_f, kl_r = lag_kernels(fwd), lag_kernels(jnp.logical_not(fwd))

    lane_t = lax.broadcasted_iota(jnp.int32, (SSM_GROUP, S5_TILE), 1)
    row_t = lax.broadcasted_iota(jnp.int32, (SSM_GROUP, S5_TILE), 0)
    blocks = []
    for t in range(L):
        m_rows = (_shift_lanes(kl_f, t * SSM_GROUP) + _shift_lanes(kl_r, -(L - 1 - t) * SSM_GROUP)
                  + jnp.where(lane_t == t * SSM_GROUP + row_t, d_rows, 0.0))
        blocks.append(jnp.concatenate([m_rows.astype(BF16), b_in[t]], axis=1))
    return jnp.concatenate(blocks, axis=0), jnp.concatenate(c_out_t, axis=0)


def _s5_kernel(bsz, nc_ctx, nc_lat, *refs):
    L = S5_CHUNK
    xl, xc, pw_ref, bc_ref, d_ref, apow_ref, coef_ref, ol, oc = refs[:9]
    xcat, ucat, y_scr, s_scr, ea_scr, eb_scr, acat, esel = refs[9:]
    nb = nc_ctx + nc_lat
    n_rows = bsz * nb
    lanes = V7X_LANES
    half = 2 * SSM_STATE
    tile = S5_SCAN_TILE
    gpb, gpp = S5_GROUPS_PER_BLOCK, S5_GROUPS_PER_PASS

    @pl.when(pl.program_id(0) == 0)
    def _():
        blk = 256
        for rb in range(esel.shape[0] // blk):
            r = lax.broadcasted_iota(jnp.int32, (blk, esel.shape[1]), 0) + rb * blk
            c = lax.broadcasted_iota(jnp.int32, (blk, esel.shape[1]), 1)
            dest = ((r % S5_TILE) // SSM_GROUP) * lanes + (r // S5_TILE) * SSM_GROUP + r % SSM_GROUP
            esel[rb * blk:(rb + 1) * blk, :] = jnp.where(dest == c, 1.0, 0.0).astype(BF16)

    for t in range(L):
        cols = slice(t * lanes, (t + 1) * lanes)
        for b in range(bsz):
            xcat[b * nb:b * nb + nc_ctx, cols] = xc[t, b]
            xcat[b * nb + nc_ctx:(b + 1) * nb, cols] = xl[t, b]

    ucat[...] = _dot_nt(xcat[...], esel[...]).astype(BF16)

    lane = lax.broadcasted_iota(jnp.int32, (tile, half), 1)
    row = lax.broadcasted_iota(jnp.int32, (tile, half), 0)
    is_fwd = lane < SSM_STATE
    shifts = (1, 2, 4)
    keep_dn = {s: is_fwd & (row >= s) for s in shifts}
    keep_up = {s: jnp.logical_not(is_fwd) & (row < tile - s) for s in shifts}

    def shift(d, s):
        return jnp.where(keep_dn[s], pltpu.roll(d, s, 0), jnp.where(keep_up[s], pltpu.roll(d, tile - s, 0), 0.0))

    def cfma(a_re, a_im, x_re, x_im, y_re, y_im):
        return a_re * x_re - a_im * x_im + y_re, a_re * x_im + a_im * x_re + y_im

    for p in range(gpb // gpp):
        readouts = []
        for q in range(gpp):
            g = p * gpp + q
            w_g, c_out_t = _s5_group_tables(pw_ref[g], bc_ref[g], d_ref[g])
            readouts.append(c_out_t)
            ys = jnp.dot(ucat[:, g * S5_TILE:(g + 1) * S5_TILE], w_g, preferred_element_type=F32)
            y_scr[q] = ys[:, :S5_TILE]
            s_scr[q] = ys[:, S5_TILE:]

        bc = lambda v: jnp.broadcast_to(v, (tile, half))
        a_pow = [[(bc(apow_ref[p * gpp + q, k, 0:1, :]), bc(apow_ref[p * gpp + q, k, 1:2, :])) for k in range(4)]
                 for q in range(gpp)]
        coefs = [(coef_ref[p * gpp + q, 0], coef_ref[p * gpp + q, 1]) for q in range(gpp)]

        for b in range(bsz):
            def make_body(rev_base, base=b * nb):
                def body(i, carry):
                    rf = pl.multiple_of(base + i * tile, tile)
                    rr = pl.multiple_of(base + (rev_base - i) * tile, tile)
                    out = []
                    for q in range(gpp):
                        e_re, e_im = carry[2 * q], carry[2 * q + 1]
                        sf = s_scr[q, pl.ds(rf, tile), :]
                        sr = s_scr[q, pl.ds(rr, tile), :]
                        d_re = jnp.where(is_fwd, sf[:, :half], sr[:, :half])
                        d_im = jnp.where(is_fwd, sf[:, half:], sr[:, half:])
                        for k, s in enumerate(shifts):
                            d_re, d_im = cfma(a_pow[q][k][0], a_pow[q][k][1], shift(d_re, s), shift(d_im, s), d_re, d_im)
                        en_re, en_im = cfma(coefs[q][0], coefs[q][1], e_re, e_im, shift(d_re, 1), shift(d_im, 1))
                        enter = jnp.concatenate([en_re, en_im], axis=-1)
                        ea_scr[q, pl.ds(rf, tile), :] = enter
                        eb_scr[q, pl.ds(rr, tile), :] = enter
                        last_re = jnp.where(is_fwd, bc(d_re[tile - 1:tile]), bc(d_re[0:1]))
                        last_im = jnp.where(is_fwd, bc(d_im[tile - 1:tile]), bc(d_im[0:1]))
                        out.extend(cfma(a_pow[q][3][0], a_pow[q][3][1], e_re, e_im, last_re, last_im))
                    return tuple(out)
                return body

            zero = jnp.zeros((tile, half), F32)
            nt_ctx, nt = nc_ctx // tile, nb // tile
            carry = lax.fori_loop(0, nt_ctx, make_body(nt_ctx - 1), tuple(zero for _ in range(2 * gpp)))
            lax.fori_loop(nt_ctx, nt, make_body(nt - 1 + nt_ctx), carry)

        lane_all = lax.broadcasted_iota(jnp.int32, (n_rows, 2 * half), 1)
        fwd_all = (lane_all % half) < SSM_STATE
        for q in range(gpp):
            g = p * gpp + q
            enter = jnp.where(fwd_all, ea_scr[q], eb_scr[q]).astype(BF16)
            y = y_scr[q] + _dot_nt(enter, readouts[q])
            acat[:, g * S5_TILE:(g + 1) * S5_TILE] = _gelu_tanh(y).astype(BF16)

    for t2 in range(L // 2):
        cols = slice(t2 * 2 * lanes, (t2 + 1) * 2 * lanes)
        a_nat = jnp.dot(acat[...], esel[:, cols], preferred_element_type=F32).astype(BF16)
        for dt_ in range(2):
            t = 2 * t2 + dt_
            for b in range(bsz):
                oc[t, b] = a_nat[b * nb:b * nb + nc_ctx, dt_ * lanes:(dt_ + 1) * lanes]
                ol[t, b] = a_nat[b * nb + nc_ctx:(b + 1) * nb, dt_ * lanes:(dt_ + 1) * lanes]


def _s5_branch(u_lat, u_ctx, mats, layer, bsz, t_len, c_len):
    pw_tab, bc_tab, d_rows, apow, coef = mats
    L = S5_CHUNK
    nc_lat, nc_ctx = t_len // L, c_len // L
    assert nc_lat % S5_SCAN_TILE == 0 and nc_ctx % S5_SCAN_TILE == 0
    n_rows = bsz * (nc_lat + nc_ctx)
    gpb = S5_GROUPS_PER_BLOCK
    slab = lambda n: pl.BlockSpec((L, bsz, n, V7X_LANES), lambda j: (0, 0, 0, j))
    grp = lambda *tail: pl.BlockSpec((None, gpb) + tail, lambda j: (layer, j) + (0,) * len(tail))
    return pl.pallas_call(
        functools.partial(_s5_kernel, bsz, nc_ctx, nc_lat),
        grid=(SSM_GROUPS // gpb,),
        in_specs=[slab(nc_lat), slab(nc_ctx), grp(2, L + 1, 2 * SSM_STATE), grp(4, SSM_GROUP, 2 * SSM_STATE),
                  grp(SSM_GROUP, S5_TILE), grp(4, 2, 2 * SSM_STATE), grp(2, S5_SCAN_TILE, 2 * SSM_STATE)],
        out_specs=[slab(nc_lat), slab(nc_ctx)],
        out_shape=[jax.ShapeDtypeStruct((L, bsz, nc_lat, SSM_WIDTH), BF16),
                   jax.ShapeDtypeStruct((L, bsz, nc_ctx, SSM_WIDTH), BF16)],
        scratch_shapes=[pltpu.VMEM((n_rows, L * V7X_LANES), BF16)] * 2
                       + [pltpu.VMEM((S5_GROUPS_PER_PASS, n_rows, S5_TILE), F32)] * 4
                       + [pltpu.VMEM((n_rows, gpb * S5_TILE), BF16),
                          pltpu.VMEM((gpb * S5_TILE, L * V7X_LANES), BF16)],
        compiler_params=_cparams(1, 56 * 1024 * 1024),
        name="s5",
    )(u_lat, u_ctx, pw_tab, bc_tab, d_rows, apow, coef)


def _rope_tables(t_len):
    nf = NA_HEAD_DIM // 4
    inv = ROPE_BASE ** (-jnp.arange(nf, dtype=F32) / nf)
    pos = jnp.arange(t_len, dtype=jnp.int32)
    ang_r = (pos // GRID_W).astype(F32)[:, None] * inv
    ang_c = (pos % GRID_W).astype(F32)[:, None] * inv
    cos = jnp.concatenate([jnp.cos(ang_r)] * 2 + [jnp.cos(ang_c)] * 2, axis=-1)
    sin = jnp.concatenate([-jnp.sin(ang_r), jnp.sin(ang_r), -jnp.sin(ang_c), jnp.sin(ang_c)], axis=-1)
    reps = V7X_LANES // NA_HEAD_DIM
    return jnp.tile(cos, (1, reps)), jnp.tile(sin, (1, reps))


def _rpb_rows(rpb):
    depth, n_h, n_r, n_c = rpb.shape
    padded = jnp.pad(rpb.astype(F32), ((0, 0), (0, 0), (0, 2 * WIN_R - n_r), (0, V7X_LANES - n_c)))
    return padded.reshape(depth, HEAD_PAIRS, n_h // HEAD_PAIRS, 2 * WIN_R, V7X_LANES)


def _build_bias(rpb_ref, bias_scr):
    kc = min(WIN_C, GRID_W)
    w = lax.broadcasted_iota(jnp.int32, (GRID_W, V7X_LANES), 0)
    lane = lax.broadcasted_iota(jnp.int32, (GRID_W, V7X_LANES), 1)
    kcol = lane % GRID_W
    c0 = jnp.clip(w - kc // 2, 0, GRID_W - kc)
    valid = (kcol >= c0) & (kcol < c0 + kc)
    for h in range(rpb_ref.shape[0]):
        lo, hi = [], []
        for ro in range(2 * WIN_R - 1):
            row = jnp.broadcast_to(rpb_ref[h, ro:ro + 1, :], (GRID_W, V7X_LANES))
            lo.append(pltpu.roll(row, V7X_LANES - (WIN_C - 1), 1, stride=1, stride_axis=0))
            hi.append(pltpu.roll(row, V7X_LANES - (WIN_C - 1) + GRID_W, 1, stride=1, stride_axis=0))
        for ro in range(2 * WIN_R - 2):
            both = jnp.where(lane < GRID_W, lo[ro], hi[ro + 1])
            bias_scr[ro, h * GRID_W:(h + 1) * GRID_W, :] = jnp.where(valid, both, MASK_VALUE)


def _rope(x, cos, sin_signed):
    nf = NA_HEAD_DIM // 4
    lane = lax.broadcasted_iota(jnp.int32, x.shape, 1)
    first = (lane % (2 * nf)) < nf
    swapped = jnp.where(first, pltpu.roll(x, V7X_LANES - nf, 1), pltpu.roll(x, nf, 1))
    return x * cos + swapped * sin_signed


def _na_kernel(rows, kr, n_cast, q_ref, k_ref, v_ref, kc_ref, vc_ref, rpb_ref, *rest):
    cast_in, o_ref, cast_out = rest[:n_cast], rest[n_cast], rest[n_cast + 1:2 * n_cast + 1]
    s_scr, p_scr, bias_scr = rest[2 * n_cast + 1:]
    step = pl.program_id(2)
    for w_ref, wb_ref in zip(cast_in, cast_out):
        wb_ref[...] = w_ref[...].astype(BF16)
    dh = NA_HEAD_DIM
    rows_per_step = q_ref.shape[0] // GRID_W
    n_heads = V7X_LANES // dh

    @pl.when(step == 0)
    def _():
        _build_bias(rpb_ref, bias_scr)

    lane = lax.broadcasted_iota(jnp.int32, (GRID_W, V7X_LANES), 1)
    n_loc = kr * GRID_W
    wins = []
    for i in range(rows_per_step):
        r = step * rows_per_step + i
        r0 = jnp.clip(r - kr // 2, 0, rows - kr)
        ro0 = r0 - r + (WIN_R - 1)
        win = pl.ds(pl.multiple_of(r0 * GRID_W, GRID_W), n_loc)
        wins.append(win)
        q = q_ref[pl.ds(i * GRID_W, GRID_W), :]
        q2 = jnp.concatenate([jnp.where((lane // dh) == h, q, jnp.zeros((), q.dtype)) for h in range(n_heads)], axis=0)
        s_loc = _dot_nt(q2, k_ref[win, :])
        for j in range(kr // 2):
            cols = slice(j * V7X_LANES, (j + 1) * V7X_LANES)
            s_scr[i, :, cols] = s_loc[:, cols] + bias_scr[ro0 + 2 * j]
        s_scr[i, :, n_loc:] = _dot_nt(q2, kc_ref[...])
    for i in range(rows_per_step):
        s = s_scr[i]
        m = jnp.max(s, axis=-1, keepdims=True)
        p = jnp.exp(s - m)
        denom = jnp.sum(p, axis=-1, keepdims=True)
        p_scr[i] = p.astype(BF16)
        o = (jnp.dot(p_scr[i, :, :n_loc], v_ref[wins[i], :], preferred_element_type=F32)
             + jnp.dot(p_scr[i, :, n_loc:], vc_ref[...], preferred_element_type=F32)) / denom
        out = o[:GRID_W]
        for h in range(1, n_heads):
            out = jnp.where((lane // dh) == h, o[h * GRID_W:(h + 1) * GRID_W], out)
        o_ref[pl.ds(i * GRID_W, GRID_W), :] = out.astype(o_ref.dtype)


def _neighbourhood_attention(qkv, qkvc, rpb_rows, layer, casts, rows_per_step=8):
    bsz, t_len, _ = qkv.shape
    c_len = qkvc.shape[1]
    rows = t_len // GRID_W
    kr = min(WIN_R, rows)
    assert kr % 2 == 0 and rows % rows_per_step == 0
    qb, kb, vb = 0, NA_WIDTH // V7X_LANES, 2 * NA_WIDTH // V7X_LANES
    tq = rows_per_step * GRID_W
    seq_spec = lambda off: pl.BlockSpec((None, t_len, V7X_LANES), lambda b, hp, r: (b, 0, off + hp))
    ctx_spec = lambda off: pl.BlockSpec((None, c_len, V7X_LANES), lambda b, hp, r: (b, 0, off + hp))
    bias_spec = pl.BlockSpec((None, None) + rpb_rows.shape[2:], lambda b, hp, r: (layer, hp, 0, 0, 0))
    n_r = rows // rows_per_step
    n_steps = bsz * HEAD_PAIRS * n_r
    step_of = lambda b, hp, r: (b * HEAD_PAIRS + hp) * n_r + r
    assert all(w.shape[1] % (n_steps * 16) == 0 for w, _ in casts)
    cast_in = [pl.BlockSpec((None, w.shape[1] // n_steps, w.shape[2]),
                            lambda b, hp, r, wl=wl: (wl, step_of(b, hp, r), 0)) for w, wl in casts]
    cast_out = [pl.BlockSpec((w.shape[1] // n_steps, w.shape[2]), lambda b, hp, r: (step_of(b, hp, r), 0))
                for w, _ in casts]
    return pl.pallas_call(
        functools.partial(_na_kernel, rows, kr, len(casts)),
        grid=(bsz, HEAD_PAIRS, n_r),
        in_specs=[pl.BlockSpec((None, tq, V7X_LANES), lambda b, hp, r: (b, r, qb + hp)),
                  seq_spec(kb), seq_spec(vb), ctx_spec(kb), ctx_spec(vb), bias_spec] + cast_in,
        out_specs=[pl.BlockSpec((None, tq, V7X_LANES), lambda b, hp, r: (b, r, hp))] + cast_out,
        out_shape=[jax.ShapeDtypeStruct((bsz, t_len, NA_WIDTH), BF16)]
                  + [jax.ShapeDtypeStruct(w.shape[1:], BF16) for w, _ in casts],
        scratch_shapes=[pltpu.VMEM((rows_per_step, 2 * GRID_W, kr * GRID_W + c_len), F32),
                        pltpu.VMEM((rows_per_step, 2 * GRID_W, kr * GRID_W + c_len), BF16),
                        pltpu.VMEM((2 * WIN_R - 2, 2 * GRID_W, 2 * GRID_W), F32)],
        compiler_params=_cparams(3, 32 * 1024 * 1024),
        name="neighbourhood_attention",
    )(qkv, qkv, qkv, qkvc, qkvc, rpb_rows, *[w for w, _ in casts])


def _ctx_attn_kernel(q_ref, k_ref, v_ref, o_ref):
    dh = NA_HEAD_DIM
    q, k, v = q_ref[...], k_ref[...], v_ref[...]
    for h in range(V7X_LANES // dh):
        hs = slice(h * dh, (h + 1) * dh)
        s = _dot_nt(q[:, hs], k[:, hs])
        p = jnp.exp(s - jnp.max(s, axis=-1, keepdims=True))
        o = jnp.dot(p.astype(BF16), v[:, hs], preferred_element_type=F32)
        o_ref[:, hs] = (o / jnp.sum(p, axis=-1, keepdims=True)).astype(o_ref.dtype)


def _context_attention(qkvc):
    bsz, c_len, _ = qkvc.shape
    qb, kb, vb = 0, NA_WIDTH // V7X_LANES, 2 * NA_WIDTH // V7X_LANES
    spec = lambda off: pl.BlockSpec((None, c_len, V7X_LANES), lambda b, hp: (b, 0, off + hp))
    return pl.pallas_call(
        _ctx_attn_kernel,
        grid=(bsz, HEAD_PAIRS),
        in_specs=[spec(qb), spec(kb), spec(vb)],
        out_specs=spec(0),
        out_shape=jax.ShapeDtypeStruct((bsz, c_len, NA_WIDTH), BF16),
        compiler_params=_cparams(2, 32 * 1024 * 1024),
        name="context_attention",
    )(qkvc, qkvc, qkvc)


def _merge_kernel(a_ref, att_ref, gs_ref, gn_ref, wv_ref, wg_ref, wn_ref, o_ref):
    a = a_ref[...]
    val = jnp.dot(a, wv_ref[...], preferred_element_type=F32)
    glu = jnp.dot(a, wg_ref[...], preferred_element_type=F32)
    br_n = jnp.dot(att_ref[...], wn_ref[...], preferred_element_type=F32)
    z = _sigmoid(gs_ref[...]) * (val * _sigmoid(glu)) + _sigmoid(gn_ref[...]) * br_n
    o_ref[...] = z.astype(o_ref.dtype)


def _merge(a, att, gates, layer, w_val, w_glu, w_na, tm=512, tn=512):
    m = a.shape[0]
    gs_off = 0
    gn_off = gs_off + D_MODEL // tn
    act = pl.BlockSpec((tm, SSM_WIDTH), lambda j, i: (i, 0))
    wsp = pl.BlockSpec((None, SSM_WIDTH, tn), lambda j, i: (layer, 0, j))
    return pl.pallas_call(
        _merge_kernel,
        grid=(D_MODEL // tn, m // tm),
        in_specs=[act, act,
                  pl.BlockSpec((tm, tn), lambda j, i: (i, gs_off + j)),
                  pl.BlockSpec((tm, tn), lambda j, i: (i, gn_off + j)),
                  wsp, wsp, wsp],
        out_specs=pl.BlockSpec((tm, tn), lambda j, i: (i, j)),
        out_shape=jax.ShapeDtypeStruct((m, D_MODEL), BF16),
        compiler_params=_cparams(2, 40 * 1024 * 1024),
        name="merge",
    )(a, att, gates, gates, w_val, w_glu, w_na)


def _outproj_kernel(z_ref, w_ref, x_ref, gt_ref, gpost_ref, gpre_ref, sh_ref, sc_ref, x_out, h_out):
    sub = V7X_LANES
    for r0 in range(0, z_ref.shape[0], sub):
        rows = slice(r0, r0 + sub)
        out = jnp.dot(z_ref[rows, :], w_ref[...], preferred_element_type=F32)
        x1 = x_ref[rows, :] + gt_ref[...] * _rms(out, gpost_ref[...])
        x_out[rows, :] = x1
        h_out[rows, :] = (_rms(x1, gpre_ref[...]) * (1.0 + sc_ref[...]) + sh_ref[...]).astype(h_out.dtype)


def _outproj(z, w_out, layer, x, gt, g_post, g_pre, sh, sc, rows_per_batch, tm=512):
    m = z.shape[0]
    row = pl.BlockSpec((tm, D_MODEL), lambda i: (i, 0))
    vec = pl.BlockSpec((1, D_MODEL), lambda i: (0, 0))
    bvec = pl.BlockSpec((None, 1, D_MODEL), lambda i: (i * tm // rows_per_batch, 0, 0))
    return pl.pallas_call(
        _outproj_kernel,
        grid=(m // tm,),
        in_specs=[row, pl.BlockSpec((None, D_MODEL, D_MODEL), lambda i: (layer, 0, 0)), row, bvec, vec, vec, bvec,
                  bvec],
        out_specs=[row, row],
        out_shape=[jax.ShapeDtypeStruct((m, D_MODEL), F32), jax.ShapeDtypeStruct((m, D_MODEL), BF16)],
        compiler_params=_cparams(1, 48 * 1024 * 1024),
        name="out_proj",
    )(z, w_out, x, gt, g_post.reshape(1, D_MODEL), g_pre.reshape(1, D_MODEL), sh, sc)


def _mlp_kernel(emit_next, h_ref, w1_ref, w2_ref, x_ref, gt_ref, gpost_ref, *rest):
    if emit_next:
        gpre_ref, sh_ref, sc_ref, x_out, h_out, acc = rest
    else:
        x_out, acc = rest
    f = pl.program_id(1)

    @pl.when(f == 0)
    def _():
        acc[...] = jnp.zeros_like(acc)

    a = jnp.maximum(jnp.dot(h_ref[...], w1_ref[...], preferred_element_type=F32), 0.0)
    acc[...] += jnp.dot((a * a).astype(BF16), w2_ref[...], preferred_element_type=F32)

    @pl.when(f == pl.num_programs(1) - 1)
    def _():
        x2 = x_ref[...] + gt_ref[...] * _rms(acc[...], gpost_ref[...])
        x_out[...] = x2
        if emit_next:
            h_out[...] = (_rms(x2, gpre_ref[...]) * (1.0 + sc_ref[...]) + sh_ref[...]).astype(h_out.dtype)


def _mlp(h2, w1, w2, layer, x, gt, g_post, nxt, rows_per_batch, tm=512, tf=1024):
    m = h2.shape[0]
    emit_next = nxt is not None
    row = pl.BlockSpec((tm, D_MODEL), lambda i, f: (i, 0))
    vec = pl.BlockSpec((1, D_MODEL), lambda i, f: (0, 0))
    bvec = pl.BlockSpec((None, 1, D_MODEL), lambda i, f: (i * tm // rows_per_batch, 0, 0))
    in_specs = [row, pl.BlockSpec((None, D_MODEL, tf), lambda i, f: (layer, 0, f)),
                pl.BlockSpec((None, tf, D_MODEL), lambda i, f: (layer, f, 0)), row, bvec, vec]
    args = [h2, w1, w2, x, gt, g_post.reshape(1, D_MODEL)]
    out_specs = [row]
    out_shape = [jax.ShapeDtypeStruct((m, D_MODEL), F32)]
    if emit_next:
        g_pre, sh, sc = nxt
        in_specs += [vec, bvec, bvec]
        args += [g_pre.reshape(1, D_MODEL), sh, sc]
        out_specs.append(row)
        out_shape.append(jax.ShapeDtypeStruct((m, D_MODEL), BF16))
    res = pl.pallas_call(
        functools.partial(_mlp_kernel, emit_next),
        grid=(m // tm, D_FF // tf),
        in_specs=in_specs,
        out_specs=out_specs,
        out_shape=out_shape,
        scratch_shapes=[pltpu.VMEM((tm, D_MODEL), F32)],
        compiler_params=_cparams(2, 56 * 1024 * 1024),
        name="mlp",
    )(*args)
    return (res[0], res[1]) if emit_next else (res[0], None)


def kernel(x, c, ctx, c_ctx, w_mod, b_mod, g_pre_mix, g_post_mix, g_pre_mlp, g_post_mlp, w_in,
           ssm_lam_re, ssm_lam_im, ssm_log_dt, ssm_b_re, ssm_b_im, ssm_c_re, ssm_c_im, ssm_d,
           w_ssm_val, w_ssm_glu, na_rpb, w_na_proj, w_out, w_fc1, w_fc2):
    bsz, t_len, _ = x.shape
    c_len = ctx.shape[1]
    m_lat, m_ctx = bsz * t_len, bsz * c_len
    assert bsz < V7X_SUBLANES and t_len % (GRID_W * 8) == 0 and c_len % S5_CHUNK == 0

    c_rows = jnp.concatenate([c, c_ctx[None, :], jnp.zeros((V7X_SUBLANES - bsz - 1, D_MODEL), F32)], axis=0)
    mod = _modulation(c_rows, w_mod, b_mod)

    def mod_vectors(l):
        lat = [v[:, None, :] for v in jnp.split(mod[l, :bsz], N_MOD, axis=-1)]
        cvec = [jnp.broadcast_to(v[None, :, :], (bsz, 1, D_MODEL))
                for v in jnp.split(mod[l, bsz:bsz + 1], N_MOD, axis=-1)]
        return lat, cvec

    mods = [mod_vectors(l) for l in range(DEPTH)]
    rope_tabs = _rope_tables(t_len)
    rpb_rows = _rpb_rows(na_rpb)
    s5_mats = jax.vmap(_s5_matrices)(ssm_lam_re, ssm_lam_im, ssm_log_dt, ssm_b_re, ssm_b_im, ssm_c_re, ssm_c_im, ssm_d)

    xl = x.reshape(m_lat, D_MODEL)
    xc = ctx.reshape(m_ctx, D_MODEL)
    h = _norm_mod(xl, g_pre_mix[0], mods[0][0][0], mods[0][0][1], t_len)
    hc = _norm_mod(xc, g_pre_mix[0], mods[0][1][0], mods[0][1][1], c_len)

    w_in_b, w_val_b, w_glu_b = w_in[:1].astype(BF16), w_ssm_val.astype(BF16), w_ssm_glu.astype(BF16)
    w_na_b, w_out_b = w_na_proj.astype(BF16), w_out.astype(BF16)

    def to_pos_major(v):
        m, w = v.shape
        return v.reshape(m // S5_CHUNK, S5_CHUNK, w).transpose(1, 0, 2).reshape(m, w)

    def from_pos_major(v):
        return v.transpose(1, 2, 0, 3).reshape(-1, v.shape[-1])

    for l in range(DEPTH):
        with_ctx_out = l < DEPTH - 1
        (sh1, sc1, gt1, sh2, sc2, gt2), (csh1, csc1, cgt1, csh2, csc2, cgt2) = mods[l]

        u_tiles, gate_tile = SSM_WIDTH // 1024, (SSM_WIDTH + 3 * NA_WIDTH) // 1024
        qkv = _qkv_proj(h, w_in_b, 0, rope_tabs, t_len, rope=True)
        qkvc = _qkv_proj(hc, w_in_b, 0, rope_tabs, c_len, rope=False)
        gates = _matmul(h, w_in_b, 0, 1024, 1024, F32, first_tile=gate_tile)
        u_lat = _matmul(to_pos_major(h), w_in_b, 0, 1024, 1024, BF16, n_tiles=u_tiles)
        u_ctx = _matmul(to_pos_major(hc), w_in_b, 0, 1024, 1024, BF16, n_tiles=u_tiles)
        qkv3 = qkv.reshape(bsz, t_len, 3 * NA_WIDTH)
        qkvc3 = qkvc.reshape(bsz, c_len, 3 * NA_WIDTH)

        a_lat, a_ctx = _s5_branch(u_lat.reshape(S5_CHUNK, bsz, t_len // S5_CHUNK, SSM_WIDTH),
                                  u_ctx.reshape(S5_CHUNK, bsz, c_len // S5_CHUNK, SSM_WIDTH), s5_mats, l,
                                  bsz, t_len, c_len)

        casts = [(w_fc1, l), (w_fc2, l)] + ([(w_in, l + 1)] if with_ctx_out else [])
        att, w1_l, w2_l, *w_in_next = _neighbourhood_attention(qkv3, qkvc3, rpb_rows, l, casts)
        w1_l, w2_l = w1_l[None], w2_l[None]
        z = _merge(from_pos_major(a_lat), att.reshape(m_lat, NA_WIDTH), gates, l, w_val_b, w_glu_b, w_na_b)
        xl, h2 = _outproj(z, w_out_b, l, xl, gt1, g_post_mix[l], g_pre_mlp[l], sh2, sc2, t_len)
        nxt = (g_pre_mix[l + 1], mods[l + 1][0][0], mods[l + 1][0][1]) if with_ctx_out else None
        xl, h = _mlp(h2, w1_l, w2_l, 0, xl, gt2, g_post_mlp[l], nxt, t_len)

        if with_ctx_out:
            attc = _context_attention(qkvc3)
            gates_c = _matmul(hc, w_in_b, 0, 1024, 1024, F32, first_tile=gate_tile)
            zc = _merge(from_pos_major(a_ctx), attc.reshape(m_ctx, NA_WIDTH), gates_c, l, w_val_b, w_glu_b, w_na_b)
            xc, h2c = _outproj(zc, w_out_b, l, xc, cgt1, g_post_mix[l], g_pre_mlp[l], csh2, csc2, c_len)
            nxtc = (g_pre_mix[l + 1], mods[l + 1][1][0], mods[l + 1][1][1])
            xc, hc = _mlp(h2c, w1_l, w2_l, 0, xc, cgt2, g_post_mlp[l], nxtc, c_len)
            w_in_b = w_in_next[0][None]

    return xl.reshape(bsz, t_len, D_MODEL)
```
